```python
import math
import jax, jax.numpy as jnp
from jax import lax
import numpy as np

D_MODEL = 1024
BATCH = 8
SEQ = 4096
DEPTH = 1

MEM_LEN = 256
GLA_HEADS = 4
GLA_DK = D_MODEL // 8
GLA_DV = D_MODEL // 4
GLA_QK = GLA_HEADS * GLA_DK
GLA_VW = GLA_HEADS * GLA_DV
GLA_RANK = 16
GLA_GATE_TEMP = 16.0
CHUNK = 64
POOL_GROUPS = 4
POOL_WIDTH = D_MODEL // 2
POOL_GC = POOL_WIDTH // POOL_GROUPS
POOL_WINDOWS = (2, 4, 8, 16)
XA_HEADS = 4
XA_HD = D_MODEL // XA_HEADS
D_FF = 2816
EPS = 1e-6
IN_SIZES = (GLA_QK, GLA_QK, GLA_VW, GLA_VW, GLA_RANK, POOL_WIDTH, D_MODEL, D_MODEL)
IN_TOTAL = sum(IN_SIZES)

kernel_name = "hybrid_gla_pool_gated_macaron"


def rmsnorm(x, g):
    x32 = x.astype(jnp.float32)
    r = lax.rsqrt(jnp.mean(x32 * x32, axis=-1, keepdims=True) + EPS)
    return (x32 * r * g.astype(jnp.float32)).astype(x.dtype)


def swiglu(h, w1, w3, w2):
    return (jax.nn.silu(h @ w1) * (h @ w3)) @ w2


def split_cols(t, sizes):
    out, o = [], 0
    for s in sizes:
        out.append(t[..., o:o + s])
        o += s
    return out


def gla_chunked(q, k, v, log_a):
    B, H, S, dk = q.shape
    dv = v.shape[-1]
    n_chunks = S // CHUNK

    def to_chunks(t):
        return jnp.moveaxis(t.astype(jnp.float32).reshape(B, H, n_chunks, CHUNK, t.shape[-1]), 2, 0)

    qc, kc, vc, ac = to_chunks(q), to_chunks(k), to_chunks(v), to_chunks(log_a)
    causal = jnp.tril(jnp.ones((CHUNK, CHUNK), dtype=bool))[None, None, :, :, None]

    def step(state, inp):
        qi, ki, vi, ai = inp
        b = jnp.cumsum(ai, axis=2)
        diff = b[:, :, :, None, :] - b[:, :, None, :, :]
        decay = jnp.exp(jnp.where(causal, diff, -jnp.inf))
        scores = jnp.einsum('bhijk,bhjk->bhij', qi[:, :, :, None, :] * decay, ki)
        o_intra = jnp.einsum('bhij,bhjv->bhiv', scores, vi)
        o_inter = jnp.einsum('bhik,bhkv->bhiv', qi * jnp.exp(b), state)
        b_last = b[:, :, -1:, :]
        k_dec = ki * jnp.exp(b_last - b)
        new_state = jnp.exp(b_last[:, :, 0, :])[..., None] * state + jnp.einsum('bhjk,bhjv->bhkv', k_dec, vi)
        return new_state, o_intra + o_inter

    state0 = jnp.zeros((B, H, dk, dv), jnp.float32)
    _, ys = lax.scan(step, state0, (qc, kc, vc, ac))
    return jnp.moveaxis(ys, 0, 2).reshape(B, H, S, dv).astype(v.dtype)


def trailing_mean(u, w):
    S = u.shape[1]
    cs = jnp.cumsum(u.astype(jnp.float32), axis=1)
    shifted = jnp.pad(cs, ((0, 0), (w, 0), (0, 0)))[:, :S]
    count = jnp.minimum(jnp.arange(1, S + 1), w).astype(jnp.float32)
    return ((cs - shifted) / count[None, :, None]).astype(u.dtype)


def hybrid_mixer(h, w_in, w_alpha, b_alpha, gla_head_norm, w_up_a, pool_mix, pool_scale, w_up_b, w_mix_out):
    B, S, _ = h.shape
    proj = h @ w_in
    q, k, v, r, a_code, u, ga, gb = split_cols(proj, IN_SIZES)

    log_a = jax.nn.log_sigmoid((a_code @ w_alpha + b_alpha).astype(jnp.float32)) / GLA_GATE_TEMP

    def heads(t, d):
        return t.reshape(B, S, GLA_HEADS, d).transpose(0, 2, 1, 3)

    o = gla_chunked(heads(q * (GLA_DK ** -0.5), GLA_DK), heads(k, GLA_DK), heads(v, GLA_DV), heads(log_a, GLA_DK))
    o = rmsnorm(o.transpose(0, 2, 1, 3), gla_head_norm)
    o = o.reshape(B, S, GLA_VW) * jax.nn.silu(r)
    y_a = o @ w_up_a

    ug = u.reshape(B, S, POOL_GROUPS, POOL_GC)
    pooled = jnp.stack([trailing_mean(ug[:, :, g, :], POOL_WINDOWS[g]) for g in range(POOL_GROUPS)], axis=2)
    z = jnp.einsum('bsgc,gcd->bsgd', pooled - ug, pool_mix)
    z = z * pool_scale.reshape(POOL_GROUPS, POOL_GC)
    y_b = z.reshape(B, S, POOL_WIDTH) @ w_up_b

    merged = jax.nn.sigmoid(ga) * y_a + jax.nn.sigmoid(gb) * y_b
    return merged @ w_mix_out


def cross_attention(h, m, wq, wk, wv, wo):
    B, S, _ = h.shape
    M = m.shape[1]
    q = (h @ wq).reshape(B, S, XA_HEADS, XA_HD)
    k = (m @ wk).reshape(B, M, XA_HEADS, XA_HD)
    v = (m @ wv).reshape(B, M, XA_HEADS, XA_HD)
    s = jnp.einsum('bshd,bmhd->bhsm', q, k).astype(jnp.float32) * (XA_HD ** -0.5)
    p = jax.nn.softmax(s, axis=-1).astype(v.dtype)
    o = jnp.einsum('bhsm,bmhd->bshd', p, v).reshape(B, S, D_MODEL)
    return o @ wo


def setup_inputs(seed: int = 0) -> dict:
    key = jax.random.key(seed)
    ks = iter(jax.random.split(key, 40))

    def nrm(shape, fan_in):
        return jax.random.normal(next(ks), shape, jnp.float32) * (fan_in ** -0.5)

    def gain(shape):
        return 1.0 + 0.02 * jax.random.normal(next(ks), shape, jnp.float32)

    L = DEPTH
    return {
        "x": jax.random.normal(next(ks), (BATCH, SEQ, D_MODEL), jnp.float32),
        "mem": jax.random.normal(next(ks), (BATCH, MEM_LEN, D_MODEL), jnp.float32),
        "ffn1_norm": gain((L, D_MODEL)),
        "ffn1_w1": nrm((L, D_MODEL, D_FF), D_MODEL),
        "ffn1_w3": nrm((L, D_MODEL, D_FF), D_MODEL),
        "ffn1_w2": nrm((L, D_FF, D_MODEL), D_FF),
        "mix_norm": gain((L, D_MODEL)),
        "w_in": nrm((L, D_MODEL, IN_TOTAL), D_MODEL),
        "w_alpha": nrm((L, GLA_RANK, GLA_QK), GLA_RANK),
        "b_alpha": 2.0 + 0.5 * jax.random.normal(next(ks), (L, GLA_QK), jnp.float32),
        "gla_head_norm": gain((L, GLA_DV)),
        "w_up_a": nrm((L, GLA_VW, D_MODEL), GLA_VW),
        "pool_mix": nrm((L, POOL_GROUPS, POOL_GC, POOL_GC), POOL_GC),
        "pool_scale": 0.5 + 0.1 * jax.random.normal(next(ks), (L, POOL_WIDTH), jnp.float32),
        "w_up_b": nrm((L, POOL_WIDTH, D_MODEL), POOL_WIDTH),
        "w_mix_out": nrm((L, D_MODEL, D_MODEL), D_MODEL),
        "xa_norm": gain((L, D_MODEL)),
        "mem_norm": gain((L, D_MODEL)),
        "xa_wq": nrm((L, D_MODEL, D_MODEL), D_MODEL),
        "xa_wk": nrm((L, D_MODEL, D_MODEL), D_MODEL),
        "xa_wv": nrm((L, D_MODEL, D_MODEL), D_MODEL),
        "xa_wo": nrm((L, D_MODEL, D_MODEL), D_MODEL),
        "ffn2_norm": gain((L, D_MODEL)),
        "ffn2_w1": nrm((L, D_MODEL, D_FF), D_MODEL),
        "ffn2_w3": nrm((L, D_MODEL, D_FF), D_MODEL),
        "ffn2_w2": nrm((L, D_FF, D_MODEL), D_FF),
        "final_norm": gain((D_MODEL,)),
    }


def reference(x, mem, ffn1_norm, ffn1_w1, ffn1_w3, ffn1_w2, mix_norm, w_in, w_alpha, b_alpha,
              gla_head_norm, w_up_a, pool_mix, pool_scale, w_up_b, w_mix_out, xa_norm, mem_norm,
              xa_wq, xa_wk, xa_wv, xa_wo, ffn2_norm, ffn2_w1, ffn2_w3, ffn2_w2, final_norm):
    for l in range(DEPTH):
        x = x + 0.5 * swiglu(rmsnorm(x, ffn1_norm[l]), ffn1_w1[l], ffn1_w3[l], ffn1_w2[l])
        x = x + hybrid_mixer(rmsnorm(x, mix_norm[l]), w_in[l], w_alpha[l], b_alpha[l], gla_head_norm[l],
                             w_up_a[l], pool_mix[l], pool_scale[l], w_up_b[l], w_mix_out[l])
        x = x + cross_attention(rmsnorm(x, xa_norm[l]), rmsnorm(mem, mem_norm[l]),
                                xa_wq[l], xa_wk[l], xa_wv[l], xa_wo[l])
        x = x + 0.5 * swiglu(rmsnorm(x, ffn2_norm[l]), ffn2_w1[l], ffn2_w3[l], ffn2_w2[l])
    return rmsnorm(x, final_norm)
```

```python
import functools

import jax
import jax.numpy as jnp
from jax import lax
from jax.experimental import pallas as pl
from jax.experimental.pallas import tpu as pltpu

F32 = jnp.float32
BF16 = jnp.bfloat16

D_MODEL = 1024
D_FF = 2816
EPS = 1e-6
GLA_HEADS = 4
GLA_DK = 128
GLA_DV = 256
GLA_QK = GLA_HEADS * GLA_DK
GLA_VW = GLA_HEADS * GLA_DV
GLA_RANK = 16
GLA_GATE_TEMP = 16.0
GLA_CHUNK = 64
GLA_SUB = 16
POOL_GROUPS = 4
POOL_GC = 128
POOL_WIDTH = POOL_GROUPS * POOL_GC
POOL_WINDOWS = (2, 4, 8, 16)
POOL_HALO = 16
XA_HEADS = 4
XA_HD = 256

LANE = 128
ROW_TILE = 512
GLA_ROWS = 512
FF_SPLITS = (0, 1536, D_FF)
VMEM_LIMIT = 56 * 1024 * 1024


def _rms(x, g):
    return x * lax.rsqrt(jnp.mean(x * x, axis=-1, keepdims=True) + EPS) * g


def _dot(a, b):
    return jnp.dot(a, b, preferred_element_type=F32)


def _resident(shape):
    nd = len(shape)
    return pl.BlockSpec(shape, lambda *_: (0,) * nd, pipeline_mode=pl.Buffered(1))


def _params():
    return pltpu.CompilerParams(dimension_semantics=("arbitrary",), vmem_limit_bytes=VMEM_LIMIT)


def _ffn_kernel(x_ref, g_ref, w1_ref, w3_ref, w2_ref, fg_ref, o_ref, *, final_norm):
    x = x_ref[...]
    h = _rms(x, g_ref[...]).astype(BF16)
    acc = jnp.zeros_like(x)
    for lo, hi in zip(FF_SPLITS[:-1], FF_SPLITS[1:]):
        a = _dot(h, w1_ref[:, lo:hi])
        b = _dot(h, w3_ref[:, lo:hi])
        act = (a * jax.nn.sigmoid(a) * b).astype(BF16)
        acc = acc + _dot(act, w2_ref[lo:hi, :])
    y = x + 0.5 * acc
    if final_norm:
        y = _rms(y, fg_ref[...])
    o_ref[...] = y


def _ffn(x, g, w1, w3, w2, fg, final_norm):
    t = x.shape[0]
    row = pl.BlockSpec((ROW_TILE, D_MODEL), lambda i: (i, 0))
    return pl.pallas_call(
        functools.partial(_ffn_kernel, final_norm=final_norm),
        grid=(t // ROW_TILE,),
        in_specs=[row, _resident((1, D_MODEL)), _resident((D_MODEL, D_FF)), _resident((D_MODEL, D_FF)),
                  _resident((D_FF, D_MODEL)), _resident((1, D_MODEL))],
        out_specs=row,
        out_shape=jax.ShapeDtypeStruct((t, D_MODEL), F32),
        compiler_params=_params(),
        name="ffn_final" if final_norm else "ffn",
    )(x, g, w1, w3, w2, fg)


def _mix_in_kernel(x_ref, g_ref, wq_ref, wk_ref, wv_ref, wr_ref, wa_ref, wu_ref, wga_ref, wgb_ref,
                   wal_ref, bal_ref,
                   q_ref, k_ref, v_ref, sr_ref, la_ref, z_ref, sga_ref, sgb_ref,
                   ubuf, *, tiles_per_seq):
    tile_in_seq = pl.program_id(0) % tiles_per_seq
    h = _rms(x_ref[...], g_ref[...]).astype(BF16)
    q_ref[...] = (_dot(h, wq_ref[...]) * (GLA_DK ** -0.5)).astype(BF16)
    k_ref[...] = _dot(h, wk_ref[...]).astype(BF16)
    v_ref[...] = _dot(h, wv_ref[...]).astype(BF16)
    r = _dot(h, wr_ref[...])
    sr_ref[...] = (r * jax.nn.sigmoid(r)).astype(BF16)
    sga_ref[...] = jax.nn.sigmoid(_dot(h, wga_ref[...])).astype(BF16)
    sgb_ref[...] = jax.nn.sigmoid(_dot(h, wgb_ref[...])).astype(BF16)

    a_code = _dot(h, wa_ref[...]).astype(BF16)
    zg = _dot(a_code, wal_ref[...]) + bal_ref[...]
    la_ref[...] = (jnp.minimum(zg, 0.0) - jnp.log1p(jnp.exp(-jnp.abs(zg)))) * (1.0 / GLA_GATE_TEMP)

    u = _dot(h, wu_ref[...])

    @pl.when(tile_in_seq == 0)
    def _():
        ubuf[0:POOL_HALO, :] = jnp.zeros((POOL_HALO, POOL_WIDTH), F32)

    ubuf[POOL_HALO:POOL_HALO + ROW_TILE, :] = u
    pos = lax.broadcasted_iota(jnp.int32, (ROW_TILE, 1), 0) + tile_in_seq * ROW_TILE
    for g, w in enumerate(POOL_WINDOWS):
        cols = slice(g * POOL_GC, (g + 1) * POOL_GC)
        ug = u[:, cols]
        s = ug
        for sh in range(1, w):
            s = s + ubuf[POOL_HALO - sh:POOL_HALO - sh + ROW_TILE, cols]
        cnt = jnp.minimum(pos + 1, w).astype(F32)
        z_ref[:, cols] = (s / cnt - ug).astype(BF16)
    ubuf[0:POOL_HALO, :] = ubuf[ROW_TILE:ROW_TILE + POOL_HALO, :]


def _mix_in(x, g, wq, wk, wv, wr, wa, wu, wga, wgb, wal, bal, seq):
    t = x.shape[0]

    def row(n):
        return pl.BlockSpec((ROW_TILE, n), lambda i: (i, 0))

    def out(n, dt):
        return jax.ShapeDtypeStruct((t, n), dt)

    return pl.pallas_call(
        functools.partial(_mix_in_kernel, tiles_per_seq=seq // ROW_TILE),
        grid=(t // ROW_TILE,),
        in_specs=[row(D_MODEL), _resident((1, D_MODEL)),
                  _resident((D_MODEL, GLA_QK)), _resident((D_MODEL, GLA_QK)), _resident((D_MODEL, GLA_VW)),
                  _resident((D_MODEL, GLA_VW)), _resident((D_MODEL, LANE)), _resident((D_MODEL, POOL_WIDTH)),
                  _resident((D_MODEL, D_MODEL)), _resident((D_MODEL, D_MODEL)),
                  _resident((LANE, GLA_QK)), _resident((1, GLA_QK))],
        out_specs=[row(GLA_QK), row(GLA_QK), row(GLA_VW), row(GLA_VW), row(GLA_QK), row(POOL_WIDTH),
                   row(D_MODEL), row(D_MODEL)],
        out_shape=[out(GLA_QK, BF16), out(GLA_QK, BF16), out(GLA_VW, BF16), out(GLA_VW, BF16),
                   out(GLA_QK, F32), out(POOL_WIDTH, BF16), out(D_MODEL, BF16), out(D_MODEL, BF16)],
        scratch_shapes=[pltpu.VMEM((POOL_HALO + ROW_TILE, POOL_WIDTH), F32)],
        compiler_params=_params(),
        name="mix_in",
    )(x, g, wq, wk, wv, wr, wa, wu, wga, wgb, wal, bal)


def _gla_kernel(q_ref, k_ref, v_ref, la_ref, gn_ref, o_ref, s_ref):
    c = GLA_CHUNK

    @pl.when(pl.program_id(1) == 0)
    def _():
        s_ref[...] = jnp.zeros_like(s_ref)

    ri = lax.broadcasted_iota(jnp.int32, (c, c), 0)
    ci = lax.broadcasted_iota(jnp.int32, (c, c), 1)
    causal = ci <= ri
    cum_incl = causal.astype(BF16)
    cum_anchor = (ci < (ri // GLA_SUB) * GLA_SUB).astype(BF16)
    krow = lax.broadcasted_iota(jnp.int32, (c, 1), 0)
    gn = gn_ref[...]

    def chunk(ic, carry):
        r0 = pl.multiple_of(ic * c, c)
        la = la_ref[pl.ds(r0, c), :]
        la_hi = la.astype(BF16)
        la_lo = (la - la_hi.astype(F32)).astype(BF16)
        b_all = _dot(cum_incl, la_hi) + _dot(cum_incl, la_lo)
        beta_all = _dot(cum_anchor, la_hi) + _dot(cum_anchor, la_lo)
        for hd in range(GLA_HEADS):
            kc = slice(hd * GLA_DK, (hd + 1) * GLA_DK)
            vc = slice(hd * GLA_DV, (hd + 1) * GLA_DV)
            q = q_ref[pl.ds(r0, c), kc].astype(F32)
            k = k_ref[pl.ds(r0, c), kc].astype(F32)
            v = v_ref[pl.ds(r0, c), vc]
            b = b_all[:, kc]
            beta = beta_all[:, kc]
            qa = (q * jnp.exp(b - beta)).astype(BF16)
            rows = []
            for blk in range(c // GLA_SUB):
                r_lo, r_hi = blk * GLA_SUB, (blk + 1) * GLA_SUB
                e = jnp.where(krow < r_hi, beta[r_lo:r_lo + 1, :] - b, 0.0)
                ka = (k * jnp.exp(e)).astype(BF16)
                rows.append(lax.dot_general(qa[r_lo:r_hi, :], ka, (((1,), (1,)), ((), ())),
                                            preferred_element_type=F32))
            p = jnp.where(causal, jnp.concatenate(rows, axis=0), 0.0).astype(BF16)
            o = _dot(p, v)
            s_old = s_ref[hd]
            o = o + _dot((q * jnp.exp(b)).astype(BF16), s_old.astype(BF16))
            b_last = b[c - 1:c, :]
            k_dec = (k * jnp.exp(b_last - b)).astype(BF16)
            upd = lax.dot_general(k_dec, v, (((0,), (0,)), ((), ())), preferred_element_type=F32)
            d_col = jnp.transpose(jnp.broadcast_to(jnp.exp(b_last), (8, GLA_DK)))[:, 0:1]
            s_ref[hd] = s_old * d_col + upd
            o_ref[pl.ds(r0, c), vc] = _rms(o, gn).astype(BF16)
        return carry

    lax.fori_loop(0, GLA_ROWS // c, chunk, 0)


def _gla(q, k, v, la, gn, batch, seq):
    t = q.shape[0]
    steps = seq // GLA_ROWS

    def row(n):
        return pl.BlockSpec((GLA_ROWS, n), lambda b, i: (b * steps + i, 0))

    return pl.pallas_call(
        _gla_kernel,
        grid=(batch, steps),
        in_specs=[row(GLA_QK), row(GLA_QK), row(GLA_VW), row(GLA_QK), _resident((1, GLA_DV))],
        out_specs=row(GLA_VW),
        out_shape=jax.ShapeDtypeStruct((t, GLA_VW), BF16),
        scratch_shapes=[pltpu.VMEM((GLA_HEADS, GLA_DK, GLA_DV), F32)],
        compiler_params=pltpu.CompilerParams(dimension_semantics=("arbitrary", "arbitrary"),
                                             vmem_limit_bytes=VMEM_LIMIT),
        name="gla",
    )(q, k, v, la, gn)


def _mix_out_kernel(x_ref, o_ref, sr_ref, z_ref, sga_ref, sgb_ref, wua_ref, pm_ref, ps_ref, wub_ref, wo_ref,
                    y_ref):
    ya = _dot(o_ref[...] * sr_ref[...], wua_ref[...])
    zs = []
    for g in range(POOL_GROUPS):
        cols = slice(g * POOL_GC, (g + 1) * POOL_GC)
        zs.append((_dot(z_ref[:, cols], pm_ref[g]) * ps_ref[:, cols]).astype(BF16))
    yb = _dot(jnp.concatenate(zs, axis=1), wub_ref[...])
    merged = (sga_ref[...].astype(F32) * ya + sgb_ref[...].astype(F32) * yb).astype(BF16)
    y_ref[...] = x_ref[...] + _dot(merged, wo_ref[...])


def _mix_out(x, o, sr, z, sga, sgb, wua, pm, ps, wub, wo):
    t = x.shape[0]

    def row(n):
        return pl.BlockSpec((ROW_TILE, n), lambda i: (i, 0))

    return pl.pallas_call(
        _mix_out_kernel,
        grid=(t // ROW_TILE,),
        in_specs=[row(D_MODEL), row(GLA_VW), row(GLA_VW), row(POOL_WIDTH), row(D_MODEL), row(D_MODEL),
                  _resident((GLA_VW, D_MODEL)), _resident((POOL_GROUPS, POOL_GC, POOL_GC)),
                  _resident((1, POOL_WIDTH)), _resident((POOL_WIDTH, D_MODEL)), _resident((D_MODEL, D_MODEL))],
        out_specs=row(D_MODEL),
        out_shape=jax.ShapeDtypeStruct((t, D_MODEL), F32),
        compiler_params=_params(),
        name="mix_out",
    )(x, o, sr, z, sga, sgb, wua, pm, ps, wub, wo)


def _mem_kv_kernel(m_ref, g_ref, wk_ref, wv_ref, k_ref, v_ref):
    m = _rms(m_ref[...], g_ref[...]).astype(BF16)
    k_ref[...] = _dot(m, wk_ref[...]).astype(BF16)
    v_ref[...] = _dot(m, wv_ref[...]).astype(BF16)


def _mem_kv(mem, g, wk, wv, mem_len):
    t = mem.shape[0]
    row = pl.BlockSpec((mem_len, D_MODEL), lambda i: (i, 0))
    out = jax.ShapeDtypeStruct((t, D_MODEL), BF16)
    return pl.pallas_call(
        _mem_kv_kernel,
        grid=(t // mem_len,),
        in_specs=[row, _resident((1, D_MODEL)), _resident((D_MODEL, D_MODEL)), _resident((D_MODEL, D_MODEL))],
        out_specs=[row, row],
        out_shape=[out, out],
        compiler_params=_params(),
        name="mem_kv",
    )(mem, g, wk, wv)


def _xattn_kernel(x_ref, g_ref, wq_ref, k_ref, v_ref, wo_ref, y_ref):
    x = x_ref[...]
    h = _rms(x, g_ref[...]).astype(BF16)
    q = (_dot(h, wq_ref[...]) * (XA_HD ** -0.5)).astype(BF16)
    outs = []
    for hd in range(XA_HEADS):
        cols = slice(hd * XA_HD, (hd + 1) * XA_HD)
        s = lax.dot_general(q[:, cols], k_ref[:, cols], (((1,), (1,)), ((), ())), preferred_element_type=F32)
        e = jnp.exp(s - jnp.max(s, axis=-1, keepdims=True))
        p = (e / jnp.sum(e, axis=-1, keepdims=True)).astype(BF16)
        outs.append(_dot(p, v_ref[:, cols]).astype(BF16))
    y_ref[...] = x + _dot(jnp.concatenate(outs, axis=1), wo_ref[...])


def _xattn(x, g, wq, km, vm, wo, seq, mem_len):
    t = x.shape[0]
    tiles_per_seq = seq // ROW_TILE
    row = pl.BlockSpec((ROW_TILE, D_MODEL), lambda i: (i, 0))
    mem = pl.BlockSpec((mem_len, D_MODEL), lambda i: (i // tiles_per_seq, 0))
    return pl.pallas_call(
        _xattn_kernel,
        grid=(t // ROW_TILE,),
        in_specs=[row, _resident((1, D_MODEL)), _resident((D_MODEL, D_MODEL)), mem, mem,
                  _resident((D_MODEL, D_MODEL))],
        out_specs=row,
        out_shape=jax.ShapeDtypeStruct((t, D_MODEL), F32),
        compiler_params=_params(),
        name="xattn",
    )(x, g, wq, km, vm, wo)


def _layer(x, mem, ffn1_norm, ffn1_w1, ffn1_w3, ffn1_w2, mix_norm, w_in, w_alpha, b_alpha, gla_head_norm,
           w_up_a, pool_mix, pool_scale, w_up_b, w_mix_out, xa_norm, mem_norm, xa_wq, xa_wk, xa_wv, xa_wo,
           ffn2_norm, ffn2_w1, ffn2_w3, ffn2_w2, final_norm, last):
    batch, seq, _ = x.shape
    mem_len = mem.shape[1]
    assert seq % ROW_TILE == 0 and seq % GLA_ROWS == 0 and GLA_ROWS % GLA_CHUNK == 0

    def vec(a):
        return a.reshape(1, -1).astype(F32)

    def bf(a):
        return a.astype(BF16)

    xt = x.reshape(batch * seq, D_MODEL)
    x1 = _ffn(xt, vec(ffn1_norm), bf(ffn1_w1), bf(ffn1_w3), bf(ffn1_w2), vec(final_norm), False)

    sizes = (GLA_QK, GLA_QK, GLA_VW, GLA_VW, GLA_RANK, POOL_WIDTH, D_MODEL, D_MODEL)
    offs = [0]
    for s in sizes:
        offs.append(offs[-1] + s)
    wq, wk, wv, wr, wa, wu, wga, wgb = (bf(w_in[:, offs[i]:offs[i + 1]]) for i in range(len(sizes)))
    wa = jnp.pad(wa, ((0, 0), (0, LANE - GLA_RANK)))
    wal = jnp.pad(bf(w_alpha), ((0, LANE - GLA_RANK), (0, 0)))
    q, k, v, sr, la, z, sga, sgb = _mix_in(x1, vec(mix_norm), wq, wk, wv, wr, wa, wu, wga, wgb, wal,
                                           vec(b_alpha), seq)
    o = _gla(q, k, v, la, vec(gla_head_norm), batch, seq)
    x2 = _mix_out(x1, o, sr, z, sga, sgb, bf(w_up_a), bf(pool_mix), vec(pool_scale), bf(w_up_b), bf(w_mix_out))

    km, vm = _mem_kv(mem.reshape(batch * mem_len, D_MODEL), vec(mem_norm), bf(xa_wk), bf(xa_wv), mem_len)
    x3 = _xattn(x2, vec(xa_norm), bf(xa_wq), km, vm, bf(xa_wo), seq, mem_len)

    x4 = _ffn(x3, vec(ffn2_norm), bf(ffn2_w1), bf(ffn2_w3), bf(ffn2_w2), vec(final_norm), last)
    return x4.reshape(batch, seq, D_MODEL)


def kernel(x, mem, ffn1_norm, ffn1_w1, ffn1_w3, ffn1_w2, mix_norm, w_in, w_alpha, b_alpha, gla_head_norm, w_up_a,
           pool_mix, pool_scale, w_up_b, w_mix_out, xa_norm, mem_norm, xa_wq, xa_wk, xa_wv, xa_wo, ffn2_norm,
           ffn2_w1, ffn2_w3, ffn2_w2, final_norm):
    depth = ffn1_norm.shape[0]
    for l in range(depth):
        last = l == depth - 1
        x = _layer(x, mem, ffn1_norm[l], ffn1_w1[l], ffn1_w3[l], ffn1_w2[l], mix_norm[l], w_in[l], w_alpha[l],
                   b_alpha[l], gla_head_norm[l], w_up_a[l], pool_mix[l], pool_scale[l], w_up_b[l], w_mix_out[l],
                   xa_norm[l], mem_norm[l], xa_wq[l], xa_wk[l], xa_wv[l], xa_wo[l], ffn2_norm[l], ffn2_w1[l],
                   ffn2_w3[l], ffn2_w2[l], final_norm, last)
    return x
```

```python
import functools

import jax
import jax.numpy as jnp
from jax import lax
from jax.experimental import pallas as pl
from jax.experimental.pallas import tpu as pltpu

F32 = jnp.float32
BF16 = jnp.bfloat16

D_MODEL = 1024
D_FF = 2816
EPS = 1e-6
GLA_HEADS = 4
GLA_DK = 128
GLA_DV = 256
GLA_QK = GLA_HEADS * GLA_DK
GLA_VW = GLA_HEADS * GLA_DV
GLA_RANK = 16
GLA_GATE_TEMP = 16.0
GLA_CHUNK = 64
GLA_SUB = 16
GLA_WIDE = 256
GLA_WIDE_MAX_SPAN = 60.0
POOL_GROUPS = 4
POOL_GC = 128
POOL_WIDTH = POOL_GROUPS * POOL_GC
POOL_WINDOWS = (2, 4, 8, 16)
POOL_HALO = 16
XA_HEADS = 4
XA_HD = 256

LANE = 128
ROW_TILE = 512
GLA_ROWS = 512
FF_SPLITS = (0, 1536, D_FF)
VMEM_LIMIT = 56 * 1024 * 1024


def _rms(x, g):
    return x * lax.rsqrt(jnp.mean(x * x, axis=-1, keepdims=True) + EPS) * g


def _dot(a, b):
    return jnp.dot(a, b, preferred_element_type=F32)


def _resident(shape):
    nd = len(shape)
    return pl.BlockSpec(shape, lambda *_: (0,) * nd, pipeline_mode=pl.Buffered(1))


def _params():
    return pltpu.CompilerParams(dimension_semantics=("arbitrary",), vmem_limit_bytes=VMEM_LIMIT)


def _ffn_kernel(x_ref, g_ref, w1_ref, w3_ref, w2_ref, fg_ref, o_ref, *, final_norm):
    x = x_ref[...]
    h = _rms(x, g_ref[...]).astype(BF16)
    acc = jnp.zeros_like(x)
    for lo, hi in zip(FF_SPLITS[:-1], FF_SPLITS[1:]):
        a = _dot(h, w1_ref[:, lo:hi])
        b = _dot(h, w3_ref[:, lo:hi])
        act = (a * jax.nn.sigmoid(a) * b).astype(BF16)
        acc = acc + _dot(act, w2_ref[lo:hi, :])
    y = x + 0.5 * acc
    if final_norm:
        y = _rms(y, fg_ref[...])
    o_ref[...] = y


def _ffn(x, g, w1, w3, w2, fg, final_norm):
    t = x.shape[0]
    row = pl.BlockSpec((ROW_TILE, D_MODEL), lambda i: (i, 0))
    return pl.pallas_call(
        functools.partial(_ffn_kernel, final_norm=final_norm),
        grid=(t // ROW_TILE,),
        in_specs=[row, _resident((1, D_MODEL)), _resident((D_MODEL, D_FF)), _resident((D_MODEL, D_FF)),
                  _resident((D_FF, D_MODEL)), _resident((1, D_MODEL))],
        out_specs=row,
        out_shape=jax.ShapeDtypeStruct((t, D_MODEL), F32),
        compiler_params=_params(),
        name="ffn_final" if final_norm else "ffn",
    )(x, g, w1, w3, w2, fg)


def _mix_in_kernel(x_ref, g_ref, wq_ref, wk_ref, wv_ref, wr_ref, wa_ref, wu_ref, wga_ref, wgb_ref,
                   wal_ref, bal_ref,
                   q_ref, k_ref, v_ref, sr_ref, la_ref, z_ref, sga_ref, sgb_ref,
                   ubuf, *, tiles_per_seq):
    tile_in_seq = pl.program_id(0) % tiles_per_seq
    h = _rms(x_ref[...], g_ref[...]).astype(BF16)
    q_ref[...] = (_dot(h, wq_ref[...]) * (GLA_DK ** -0.5)).astype(BF16)
    k_ref[...] = _dot(h, wk_ref[...]).astype(BF16)
    v_ref[...] = _dot(h, wv_ref[...]).astype(BF16)
    r = _dot(h, wr_ref[...])
    sr_ref[...] = (r * jax.nn.sigmoid(r)).astype(BF16)
    sga_ref[...] = jax.nn.sigmoid(_dot(h, wga_ref[...])).astype(BF16)
    sgb_ref[...] = jax.nn.sigmoid(_dot(h, wgb_ref[...])).astype(BF16)

    a_code = _dot(h, wa_ref[...]).astype(BF16)
    zg = _dot(a_code, wal_ref[...]) + bal_ref[...]
    la_ref[...] = (jnp.minimum(zg, 0.0) - jnp.log1p(jnp.exp(-jnp.abs(zg)))) * (1.0 / GLA_GATE_TEMP)

    u = _dot(h, wu_ref[...])

    @pl.when(tile_in_seq == 0)
    def _():
        ubuf[0:POOL_HALO, :] = jnp.zeros((POOL_HALO, POOL_WIDTH), F32)

    ubuf[POOL_HALO:POOL_HALO + ROW_TILE, :] = u
    pos = lax.broadcasted_iota(jnp.int32, (ROW_TILE, 1), 0) + tile_in_seq * ROW_TILE
    for g, w in enumerate(POOL_WINDOWS):
        cols = slice(g * POOL_GC, (g + 1) * POOL_GC)
        ug = u[:, cols]
        s = ug
        for sh in range(1, w):
            s = s + ubuf[POOL_HALO - sh:POOL_HALO - sh + ROW_TILE, cols]
        cnt = jnp.minimum(pos + 1, w).astype(F32)
        z_ref[:, cols] = (s / cnt - ug).astype(BF16)
    ubuf[0:POOL_HALO, :] = ubuf[ROW_TILE:ROW_TILE + POOL_HALO, :]


def _mix_in(x, g, wq, wk, wv, wr, wa, wu, wga, wgb, wal, bal, seq):
    t = x.shape[0]

    def row(n):
        return pl.BlockSpec((ROW_TILE, n), lambda i: (i, 0))

    def out(n, dt):
        return jax.ShapeDtypeStruct((t, n), dt)

    return pl.pallas_call(
        functools.partial(_mix_in_kernel, tiles_per_seq=seq // ROW_TILE),
        grid=(t // ROW_TILE,),
        in_specs=[row(D_MODEL), _resident((1, D_MODEL)),
                  _resident((D_MODEL, GLA_QK)), _resident((D_MODEL, GLA_QK)), _resident((D_MODEL, GLA_VW)),
                  _resident((D_MODEL, GLA_VW)), _resident((D_MODEL, LANE)), _resident((D_MODEL, POOL_WIDTH)),
                  _resident((D_MODEL, D_MODEL)), _resident((D_MODEL, D_MODEL)),
                  _resident((LANE, GLA_QK)), _resident((1, GLA_QK))],
        out_specs=[row(GLA_QK), row(GLA_QK), row(GLA_VW), row(GLA_VW), row(GLA_QK), row(POOL_WIDTH),
                   row(D_MODEL), row(D_MODEL)],
        out_shape=[out(GLA_QK, BF16), out(GLA_QK, BF16), out(GLA_VW, BF16), out(GLA_VW, BF16),
                   out(GLA_QK, F32), out(POOL_WIDTH, BF16), out(D_MODEL, BF16), out(D_MODEL, BF16)],
        scratch_shapes=[pltpu.VMEM((POOL_HALO + ROW_TILE, POOL_WIDTH), F32)],
        compiler_params=_params(),
        name="mix_in",
    )(x, g, wq, wk, wv, wr, wa, wu, wga, wgb, wal, bal)


def _cumsum_rows(mask_bf16, la):
    hi = la.astype(BF16)
    lo = (la - hi.astype(F32)).astype(BF16)
    return _dot(mask_bf16, hi) + _dot(mask_bf16, lo)


def _decay_column(b_last_row):
    return jnp.transpose(jnp.broadcast_to(jnp.exp(b_last_row), (8, GLA_DK)))[:, 0:1]


def _gla_wide_chunks(q_ref, k_ref, v_ref, la_ref, gn, o_ref, s_ref):
    c = GLA_WIDE
    ri = lax.broadcasted_iota(jnp.int32, (c, c), 0)
    ci = lax.broadcasted_iota(jnp.int32, (c, c), 1)
    causal = ci <= ri
    cum_incl = causal.astype(BF16)
    for ic in range(GLA_ROWS // c):
        rows = slice(ic * c, (ic + 1) * c)
        b_all = _cumsum_rows(cum_incl, la_ref[rows, :])
        for hd in range(GLA_HEADS):
            kc = slice(hd * GLA_DK, (hd + 1) * GLA_DK)
            vc = slice(hd * GLA_DV, (hd + 1) * GLA_DV)
            b = b_all[:, kc]
            qg = (q_ref[rows, kc].astype(F32) * jnp.exp(b)).astype(BF16)
            kg = (k_ref[rows, kc].astype(F32) * jnp.exp(-b)).astype(BF16)
            v = v_ref[rows, vc]
            s = lax.dot_general(qg, kg, (((1,), (1,)), ((), ())), preferred_element_type=F32)
            p = jnp.where(causal, s, 0.0).astype(BF16)
            s_old = s_ref[hd]
            o = _dot(p, v) + _dot(qg, s_old.astype(BF16))
            upd = lax.dot_general(kg, v, (((0,), (0,)), ((), ())), preferred_element_type=F32)
            s_ref[hd] = (s_old + upd) * _decay_column(b[c - 1:c, :])
            o_ref[rows, vc] = _rms(o, gn).astype(BF16)


def _gla_anchored_chunks(q_ref, k_ref, v_ref, la_ref, gn, o_ref, s_ref):
    c = GLA_CHUNK
    ri = lax.broadcasted_iota(jnp.int32, (c, c), 0)
    ci = lax.broadcasted_iota(jnp.int32, (c, c), 1)
    causal = ci <= ri
    cum_incl = causal.astype(BF16)
    cum_anchor = (ci < (ri // GLA_SUB) * GLA_SUB).astype(BF16)
    krow = lax.broadcasted_iota(jnp.int32, (c, 1), 0)

    def chunk(ic, carry):
        r0 = pl.multiple_of(ic * c, c)
        la = la_ref[pl.ds(r0, c), :]
        b_all = _cumsum_rows(cum_incl, la)
        beta_all = _cumsum_rows(cum_anchor, la)
        for hd in range(GLA_HEADS):
            kc = slice(hd * GLA_DK, (hd + 1) * GLA_DK)
            vc = slice(hd * GLA_DV, (hd + 1) * GLA_DV)
            q = q_ref[pl.ds(r0, c), kc].astype(F32)
            k = k_ref[pl.ds(r0, c), kc].astype(F32)
            v = v_ref[pl.ds(r0, c), vc]
            b = b_all[:, kc]
            beta = beta_all[:, kc]
            qa = (q * jnp.exp(b - beta)).astype(BF16)
            rows = []
            for blk in range(c // GLA_SUB):
                r_lo, r_hi = blk * GLA_SUB, (blk + 1) * GLA_SUB
                e = jnp.where(krow < r_hi, beta[r_lo:r_lo + 1, :] - b, 0.0)
                ka = (k * jnp.exp(e)).astype(BF16)
                rows.append(lax.dot_general(qa[r_lo:r_hi, :], ka, (((1,), (1,)), ((), ())),
                                            preferred_element_type=F32))
            p = jnp.where(causal, jnp.concatenate(rows, axis=0), 0.0).astype(BF16)
            s_old = s_ref[hd]
            o = _dot(p, v) + _dot((q * jnp.exp(b)).astype(BF16), s_old.astype(BF16))
            b_last = b[c - 1:c, :]
            k_dec = (k * jnp.exp(b_last - b)).astype(BF16)
            upd = lax.dot_general(k_dec, v, (((0,), (0,)), ((), ())), preferred_element_type=F32)
            s_ref[hd] = s_old * _decay_column(b_last) + upd
            o_ref[pl.ds(r0, c), vc] = _rms(o, gn).astype(BF16)
        return carry

    lax.fori_loop(0, GLA_ROWS // c, chunk, 0)


def _gla_kernel(q_ref, k_ref, v_ref, la_ref, gn_ref, o_ref, s_ref):
    @pl.when(pl.program_id(1) == 0)
    def _():
        s_ref[...] = jnp.zeros_like(s_ref)

    span = None
    for ic in range(GLA_ROWS // GLA_WIDE):
        tot = jnp.sum(la_ref[ic * GLA_WIDE:(ic + 1) * GLA_WIDE, :], axis=0, keepdims=True)
        span = jnp.max(-tot) if span is None else jnp.maximum(span, jnp.max(-tot))
    wide_ok = span < GLA_WIDE_MAX_SPAN
    args = (q_ref, k_ref, v_ref, la_ref, gn_ref[...], o_ref, s_ref)

    @pl.when(wide_ok)
    def _():
        _gla_wide_chunks(*args)

    @pl.when(jnp.logical_not(wide_ok))
    def _():
        _gla_anchored_chunks(*args)


def _gla(q, k, v, la, gn, batch, seq):
    t = q.shape[0]
    steps = seq // GLA_ROWS

    def row(n):
        return pl.BlockSpec((GLA_ROWS, n), lambda b, i: (b * steps + i, 0))

    return pl.pallas_call(
        _gla_kernel,
        grid=(batch, steps),
        in_specs=[row(GLA_QK), row(GLA_QK), row(GLA_VW), row(GLA_QK), _resident((1, GLA_DV))],
        out_specs=row(GLA_VW),
        out_shape=jax.ShapeDtypeStruct((t, GLA_VW), BF16),
        scratch_shapes=[pltpu.VMEM((GLA_HEADS, GLA_DK, GLA_DV), F32)],
        compiler_params=pltpu.CompilerParams(dimension_semantics=("arbitrary", "arbitrary"),
                                             vmem_limit_bytes=VMEM_LIMIT),
        name="gla",
    )(q, k, v, la, gn)


def _mix_out_kernel(x_ref, o_ref, sr_ref, z_ref, sga_ref, sgb_ref, wua_ref, pm_ref, ps_ref, wub_ref, wo_ref,
                    y_ref):
    ya = _dot(o_ref[...] * sr_ref[...], wua_ref[...])
    zs = []
    for g in range(POOL_GROUPS):
        cols = slice(g * POOL_GC, (g + 1) * POOL_GC)
        zs.append((_dot(z_ref[:, cols], pm_ref[g]) * ps_ref[:, cols]).astype(BF16))
    yb = _dot(jnp.concatenate(zs, axis=1), wub_ref[...])
    merged = (sga_ref[...].astype(F32) * ya + sgb_ref[...].astype(F32) * yb).astype(BF16)
    y_ref[...] = x_ref[...] + _dot(merged, wo_ref[...])


def _mix_out(x, o, sr, z, sga, sgb, wua, pm, ps, wub, wo):
    t = x.shape[0]

    def row(n):
        return pl.BlockSpec((ROW_TILE, n), lambda i: (i, 0))

    return pl.pallas_call(
        _mix_out_kernel,
        grid=(t // ROW_TILE,),
        in_specs=[row(D_MODEL), row(GLA_VW), row(GLA_VW), row(POOL_WIDTH), row(D_MODEL), row(D_MODEL),
                  _resident((GLA_VW, D_MODEL)), _resident((POOL_GROUPS, POOL_GC, POOL_GC)),
                  _resident((1, POOL_WIDTH)), _resident((POOL_WIDTH, D_MODEL)), _resident((D_MODEL, D_MODEL))],
        out_specs=row(D_MODEL),
        out_shape=jax.ShapeDtypeStruct((t, D_MODEL), F32),
        compiler_params=_params(),
        name="mix_out",
    )(x, o, sr, z, sga, sgb, wua, pm, ps, wub, wo)


def _mem_kv_kernel(m_ref, g_ref, wk_ref, wv_ref, k_ref, v_ref):
    m = _rms(m_ref[...], g_ref[...]).astype(BF16)
    k_ref[...] = _dot(m, wk_ref[...]).astype(BF16)
    v_ref[...] = _dot(m, wv_ref[...]).astype(BF16)


def _mem_kv(mem, g, wk, wv, mem_len):
    t = mem.shape[0]
    row = pl.BlockSpec((mem_len, D_MODEL), lambda i: (i, 0))
    out = jax.ShapeDtypeStruct((t, D_MODEL), BF16)
    return pl.pallas_call(
        _mem_kv_kernel,
        grid=(t // mem_len,),
        in_specs=[row, _resident((1, D_MODEL)), _resident((D_MODEL, D_MODEL)), _resident((D_MODEL, D_MODEL))],
        out_specs=[row, row],
        out_shape=[out, out],
        compiler_params=_params(),
        name="mem_kv",
    )(mem, g, wk, wv)


def _xattn_kernel(x_ref, g_ref, wq_ref, k_ref, v_ref, wo_ref, y_ref):
    x = x_ref[...]
    h = _rms(x, g_ref[...]).astype(BF16)
    q = (_dot(h, wq_ref[...]) * (XA_HD ** -0.5)).astype(BF16)
    outs = []
    for hd in range(XA_HEADS):
        cols = slice(hd * XA_HD, (hd + 1) * XA_HD)
        s = lax.dot_general(q[:, cols], k_ref[:, cols], (((1,), (1,)), ((), ())), preferred_element_type=F32)
        e = jnp.exp(s - jnp.max(s, axis=-1, keepdims=True))
        p = (e / jnp.sum(e, axis=-1, keepdims=True)).astype(BF16)
        outs.append(_dot(p, v_ref[:, cols]).astype(BF16))
    y_ref[...] = x + _dot(jnp.concatenate(outs, axis=1), wo_ref[...])


def _xattn(x, g, wq, km, vm, wo, seq, mem_len):
    t = x.shape[0]
    tiles_per_seq = seq // ROW_TILE
    row = pl.BlockSpec((ROW_TILE, D_MODEL), lambda i: (i, 0))
    mem = pl.BlockSpec((mem_len, D_MODEL), lambda i: (i // tiles_per_seq, 0))
    return pl.pallas_call(
        _xattn_kernel,
        grid=(t // ROW_TILE,),
        in_specs=[row, _resident((1, D_MODEL)), _resident((D_MODEL, D_MODEL)), mem, mem,
                  _resident((D_MODEL, D_MODEL))],
        out_specs=row,
        out_shape=jax.ShapeDtypeStruct((t, D_MODEL), F32),
        compiler_params=_params(),
        name="xattn",
    )(x, g, wq, km, vm, wo)


def _layer(x, mem, ffn1_norm, ffn1_w1, ffn1_w3, ffn1_w2, mix_norm, w_in, w_alpha, b_alpha, gla_head_norm,
           w_up_a, pool_mix, pool_scale, w_up_b, w_mix_out, xa_norm, mem_norm, xa_wq, xa_wk, xa_wv, xa_wo,
           ffn2_norm, ffn2_w1, ffn2_w3, ffn2_w2, final_norm, last):
    batch, seq, _ = x.shape
    mem_len = mem.shape[1]
    assert seq % ROW_TILE == 0 and seq % GLA_ROWS == 0 and GLA_ROWS % GLA_CHUNK == 0 and GLA_ROWS % GLA_WIDE == 0

    def vec(a):
        return a.reshape(1, -1).astype(F32)

    def bf(a):
        return a.astype(BF16)

    xt = x.reshape(batch * seq, D_MODEL)
    x1 = _ffn(xt, vec(ffn1_norm), bf(ffn1_w1), bf(ffn1_w3), bf(ffn1_w2), vec(final_norm), False)

    sizes = (GLA_QK, GLA_QK, GLA_VW, GLA_VW, GLA_RANK, POOL_WIDTH, D_MODEL, D_MODEL)
    offs = [0]
    for s in sizes:
        offs.append(offs[-1] + s)
    wq, wk, wv, wr, wa, wu, wga, wgb = (bf(w_in[:, offs[i]:offs[i + 1]]) for i in range(len(sizes)))
    wa = jnp.pad(wa, ((0, 0), (0, LANE - GLA_RANK)))
    wal = jnp.pad(bf(w_alpha), ((0, LANE - GLA_RANK), (0, 0)))
    q, k, v, sr, la, z, sga, sgb = _mix_in(x1, vec(mix_norm), wq, wk, wv, wr, wa, wu, wga, wgb, wal,
                                           vec(b_alpha), seq)
    o = _gla(q, k, v, la, vec(gla_head_norm), batch, seq)
    x2 = _mix_out(x1, o, sr, z, sga, sgb, bf(w_up_a), bf(pool_mix), vec(pool_scale), bf(w_up_b), bf(w_mix_out))

    km, vm = _mem_kv(mem.reshape(batch * mem_len, D_MODEL), vec(mem_norm), bf(xa_wk), bf(xa_wv), mem_len)
    x3 = _xattn(x2, vec(xa_norm), bf(xa_wq), km, vm, bf(xa_wo), seq, mem_len)

    x4 = _ffn(x3, vec(ffn2_norm), bf(ffn2_w1), bf(ffn2_w3), bf(ffn2_w2), vec(final_norm), last)
    return x4.reshape(batch, seq, D_MODEL)


def kernel(x, mem, ffn1_norm, ffn1_w1, ffn1_w3, ffn1_w2, mix_norm, w_in, w_alpha, b_alpha, gla_head_norm, w_up_a,
           pool_mix, pool_scale, w_up_b, w_mix_out, xa_norm, mem_norm, xa_wq, xa_wk, xa_wv, xa_wo, ffn2_norm,
           ffn2_w1, ffn2_w3, ffn2_w2, final_norm):
    depth = ffn1_norm.shape[0]
    for l in range(depth):
        last = l == depth - 1
        x = _layer(x, mem, ffn1_norm[l], ffn1_w1[l], ffn1_w3[l], ffn1_w2[l], mix_norm[l], w_in[l], w_alpha[l],
                   b_alpha[l], gla_head_norm[l], w_up_a[l], pool_mix[l], pool_scale[l], w_up_b[l], w_mix_out[l],
                   xa_norm[l], mem_norm[l], xa_wq[l], xa_wk[l], xa_wv[l], xa_wo[l], ffn2_norm[l], ffn2_w1[l],
                   ffn2_w3[l], ffn2_w2[l], final_norm, last)
    return x
```

```python
import functools

import jax
import jax.numpy as jnp
from jax import lax
from jax.experimental import pallas as pl
from jax.experimental.pallas import tpu as pltpu

F32 = jnp.float32
BF16 = jnp.bfloat16

D_MODEL = 1024
D_FF = 2816
EPS = 1e-6
GLA_HEADS = 4
GLA_DK = 128
GLA_DV = 256
GLA_QK = GLA_HEADS * GLA_DK
GLA_VW = GLA_HEADS * GLA_DV
GLA_RANK = 16
GLA_GATE_TEMP = 16.0
GLA_CHUNK = 64
GLA_SUB = 16
GLA_WIDE = 256
GLA_WIDE_MAX_SPAN = 60.0
POOL_GROUPS = 4
POOL_GC = 128
POOL_WIDTH = POOL_GROUPS * POOL_GC
POOL_WINDOWS = (2, 4, 8, 16)
POOL_HALO = 16
XA_HEADS = 4
XA_HD = 256

LANE = 128
ROW_TILE = 512
GLA_ROWS = 512
FF_SPLITS = (0, 1536, D_FF)
VMEM_LIMIT = 56 * 1024 * 1024


def _rms(x, g):
    return x * lax.rsqrt(jnp.mean(x * x, axis=-1, keepdims=True) + EPS) * g


def _dot(a, b):
    return jnp.dot(a, b, preferred_element_type=F32)


def _resident(shape):
    nd = len(shape)
    return pl.BlockSpec(shape, lambda *_: (0,) * nd, pipeline_mode=pl.Buffered(1))


def _params():
    return pltpu.CompilerParams(dimension_semantics=("arbitrary",), vmem_limit_bytes=VMEM_LIMIT)


def _ffn_kernel(x_ref, g_ref, w1_ref, w3_ref, w2_ref, fg_ref, o_ref, *, final_norm):
    x = x_ref[...]
    h = _rms(x, g_ref[...]).astype(BF16)
    acc = jnp.zeros_like(x)
    for lo, hi in zip(FF_SPLITS[:-1], FF_SPLITS[1:]):
        a = _dot(h, w1_ref[:, lo:hi])
        b = _dot(h, w3_ref[:, lo:hi])
        act = (a * jax.nn.sigmoid(a) * b).astype(BF16)
        acc = acc + _dot(act, w2_ref[lo:hi, :])
    y = x + 0.5 * acc
    if final_norm:
        y = _rms(y, fg_ref[...])
    o_ref[...] = y


def _ffn(x, g, w1, w3, w2, fg, final_norm):
    t = x.shape[0]
    row = pl.BlockSpec((ROW_TILE, D_MODEL), lambda i: (i, 0))
    return pl.pallas_call(
        functools.partial(_ffn_kernel, final_norm=final_norm),
        grid=(t // ROW_TILE,),
        in_specs=[row, _resident((1, D_MODEL)), _resident((D_MODEL, D_FF)), _resident((D_MODEL, D_FF)),
                  _resident((D_FF, D_MODEL)), _resident((1, D_MODEL))],
        out_specs=row,
        out_shape=jax.ShapeDtypeStruct((t, D_MODEL), F32),
        compiler_params=_params(),
        name="ffn_final" if final_norm else "ffn",
    )(x, g, w1, w3, w2, fg)


def _mix_in_kernel(x_ref, g_ref, wq_ref, wk_ref, wv_ref, wr_ref, wa_ref, wu_ref, wga_ref, wgb_ref,
                   wal_ref, bal_ref,
                   q_ref, k_ref, v_ref, sr_ref, la_ref, z_ref, sga_ref, sgb_ref, span_ref,
                   ubuf, *, tiles_per_seq):
    tile_in_seq = pl.program_id(0) % tiles_per_seq

    @pl.when(tile_in_seq == 0)
    def _():
        ubuf[0:POOL_HALO, :] = jnp.zeros((POOL_HALO, POOL_WIDTH), F32)

    h = _rms(x_ref[...], g_ref[...]).astype(BF16)
    q_ref[...] = (_dot(h, wq_ref[...]) * (GLA_DK ** -0.5)).astype(BF16)
    k_ref[...] = _dot(h, wk_ref[...]).astype(BF16)
    v_ref[...] = _dot(h, wv_ref[...]).astype(BF16)
    r = _dot(h, wr_ref[...])
    sr_ref[...] = (r * jax.nn.sigmoid(r)).astype(BF16)
    sga_ref[...] = jax.nn.sigmoid(_dot(h, wga_ref[...])).astype(BF16)
    sgb_ref[...] = jax.nn.sigmoid(_dot(h, wgb_ref[...])).astype(BF16)

    a_code = _dot(h, wa_ref[...]).astype(BF16)
    zg = _dot(a_code, wal_ref[...]) + bal_ref[...]
    la = (jnp.minimum(zg, 0.0) - jnp.log1p(jnp.exp(-jnp.abs(zg)))) * (1.0 / GLA_GATE_TEMP)
    la_ref[...] = la
    spans = [jnp.max(-jnp.sum(la[r:r + GLA_WIDE, :], axis=0, keepdims=True), axis=1, keepdims=True)
             for r in range(0, ROW_TILE, GLA_WIDE)]
    sub = lax.broadcasted_iota(jnp.int32, (8, LANE), 0)
    blk = jnp.zeros((8, LANE), F32)
    for j, sp in enumerate(spans):
        blk = jnp.where(sub == j, sp, blk)
    span_ref[0] = blk

    u = _dot(h, wu_ref[...])
    ubuf[POOL_HALO:POOL_HALO + ROW_TILE, :] = u
    pos = lax.broadcasted_iota(jnp.int32, (ROW_TILE, 1), 0) + tile_in_seq * ROW_TILE
    for g, w in enumerate(POOL_WINDOWS):
        cols = slice(g * POOL_GC, (g + 1) * POOL_GC)
        ug = u[:, cols]
        s = ug
        for sh in range(1, w):
            s = s + ubuf[POOL_HALO - sh:POOL_HALO - sh + ROW_TILE, cols]
        cnt = jnp.minimum(pos + 1, w).astype(F32)
        z_ref[:, cols] = (s / cnt - ug).astype(BF16)
    ubuf[0:POOL_HALO, :] = ubuf[ROW_TILE:ROW_TILE + POOL_HALO, :]


def _mix_in(x, g, wq, wk, wv, wr, wa, wu, wga, wgb, wal, bal, seq):
    t = x.shape[0]

    def row(n):
        return pl.BlockSpec((ROW_TILE, n), lambda i: (i, 0))

    def out(n, dt):
        return jax.ShapeDtypeStruct((t, n), dt)

    return pl.pallas_call(
        functools.partial(_mix_in_kernel, tiles_per_seq=seq // ROW_TILE),
        grid=(t // ROW_TILE,),
        in_specs=[row(D_MODEL), _resident((1, D_MODEL)),
                  _resident((D_MODEL, GLA_QK)), _resident((D_MODEL, GLA_QK)), _resident((D_MODEL, GLA_VW)),
                  _resident((D_MODEL, GLA_VW)), _resident((D_MODEL, LANE)), _resident((D_MODEL, POOL_WIDTH)),
                  _resident((D_MODEL, D_MODEL)), _resident((D_MODEL, D_MODEL)),
                  _resident((LANE, GLA_QK)), _resident((1, GLA_QK))],
        out_specs=[row(GLA_QK), row(GLA_QK), row(GLA_VW), row(GLA_VW), row(GLA_QK), row(POOL_WIDTH),
                   row(D_MODEL), row(D_MODEL), pl.BlockSpec((1, 8, LANE), lambda i: (i, 0, 0))],
        out_shape=[out(GLA_QK, BF16), out(GLA_QK, BF16), out(GLA_VW, BF16), out(GLA_VW, BF16),
                   out(GLA_QK, F32), out(POOL_WIDTH, BF16), out(D_MODEL, BF16), out(D_MODEL, BF16),
                   jax.ShapeDtypeStruct((t // ROW_TILE, 8, LANE), F32)],
        scratch_shapes=[pltpu.VMEM((POOL_HALO + ROW_TILE, POOL_WIDTH), F32)],
        compiler_params=_params(),
        name="mix_in",
    )(x, g, wq, wk, wv, wr, wa, wu, wga, wgb, wal, bal)


def _cumsum_rows(mask_bf16, la):
    hi = la.astype(BF16)
    lo = (la - hi.astype(F32)).astype(BF16)
    return _dot(mask_bf16, hi) + _dot(mask_bf16, lo)


def _decay_column(b_last_row):
    return jnp.transpose(jnp.broadcast_to(jnp.exp(b_last_row), (8, GLA_DK)))[:, 0:1]


def _gla_wide_chunks(q_ref, k_ref, v_ref, la_ref, gn, o_ref, s_ref):
    c = GLA_WIDE
    ri = lax.broadcasted_iota(jnp.int32, (c, c), 0)
    ci = lax.broadcasted_iota(jnp.int32, (c, c), 1)
    causal = ci <= ri
    cum_incl = causal.astype(BF16)
    row_sl = [slice(ic * c, (ic + 1) * c) for ic in range(GLA_ROWS // c)]
    b_all = [_cumsum_rows(cum_incl, la_ref[rows, :]) for rows in row_sl]
    units = [(ic, hd) for ic in range(len(row_sl)) for hd in range(GLA_HEADS)]

    def factors(ic, hd):
        rows, kc = row_sl[ic], slice(hd * GLA_DK, (hd + 1) * GLA_DK)
        b = b_all[ic][:, kc]
        qg = (q_ref[rows, kc].astype(F32) * jnp.exp(b)).astype(BF16)
        kg = (k_ref[rows, kc].astype(F32) * jnp.exp(-b)).astype(BF16)
        s = lax.dot_general(qg, kg, (((1,), (1,)), ((), ())), preferred_element_type=F32)
        return qg, kg, s, b[c - 1:c, :]

    nxt = factors(*units[0])
    for n, (ic, hd) in enumerate(units):
        qg, kg, s, b_last = nxt
        if n + 1 < len(units):
            nxt = factors(*units[n + 1])
        rows, vc = row_sl[ic], slice(hd * GLA_DV, (hd + 1) * GLA_DV)
        v = v_ref[rows, vc]
        s_old = s_ref[hd]
        inter = _dot(qg, s_old.astype(BF16))
        upd = lax.dot_general(kg, v, (((0,), (0,)), ((), ())), preferred_element_type=F32)
        p = jnp.where(causal, s, 0.0).astype(BF16)
        o = _dot(p, v) + inter
        s_ref[hd] = (s_old + upd) * _decay_column(b_last)
        o_ref[rows, vc] = _rms(o, gn).astype(BF16)


def _gla_anchored_chunks(q_ref, k_ref, v_ref, la_ref, gn, o_ref, s_ref):
    c = GLA_CHUNK
    ri = lax.broadcasted_iota(jnp.int32, (c, c), 0)
    ci = lax.broadcasted_iota(jnp.int32, (c, c), 1)
    causal = ci <= ri
    cum_incl = causal.astype(BF16)
    cum_anchor = (ci < (ri // GLA_SUB) * GLA_SUB).astype(BF16)
    krow = lax.broadcasted_iota(jnp.int32, (c, 1), 0)

    def chunk(ic, carry):
        r0 = pl.multiple_of(ic * c, c)
        la = la_ref[pl.ds(r0, c), :]
        b_all = _cumsum_rows(cum_incl, la)
        beta_all = _cumsum_rows(cum_anchor, la)
        for hd in range(GLA_HEADS):
            kc = slice(hd * GLA_DK, (hd + 1) * GLA_DK)
            vc = slice(hd * GLA_DV, (hd + 1) * GLA_DV)
            q = q_ref[pl.ds(r0, c), kc].astype(F32)
            k = k_ref[pl.ds(r0, c), kc].astype(F32)
            v = v_ref[pl.ds(r0, c), vc]
            b = b_all[:, kc]
            beta = beta_all[:, kc]
            qa = (q * jnp.exp(b - beta)).astype(BF16)
            rows = []
            for blk in range(c // GLA_SUB):
                r_lo, r_hi = blk * GLA_SUB, (blk + 1) * GLA_SUB
                e = jnp.where(krow < r_hi, beta[r_lo:r_lo + 1, :] - b, 0.0)
                ka = (k * jnp.exp(e)).astype(BF16)
                rows.append(lax.dot_general(qa[r_lo:r_hi, :], ka, (((1,), (1,)), ((), ())),
                                            preferred_element_type=F32))
            p = jnp.where(causal, jnp.concatenate(rows, axis=0), 0.0).astype(BF16)
            s_old = s_ref[hd]
            o = _dot(p, v) + _dot((q * jnp.exp(b)).astype(BF16), s_old.astype(BF16))
            b_last = b[c - 1:c, :]
            k_dec = (k * jnp.exp(b_last - b)).astype(BF16)
            upd = lax.dot_general(k_dec, v, (((0,), (0,)), ((), ())), preferred_element_type=F32)
            s_ref[hd] = s_old * _decay_column(b_last) + upd
            o_ref[pl.ds(r0, c), vc] = _rms(o, gn).astype(BF16)
        return carry

    lax.fori_loop(0, GLA_ROWS // c, chunk, 0)


def _gla_kernel(span_ref, q_ref, k_ref, v_ref, la_ref, gn_ref, o_ref, s_ref):
    @pl.when(pl.program_id(1) == 0)
    def _():
        s_ref[...] = jnp.zeros_like(s_ref)

    n_wide = GLA_ROWS // GLA_WIDE
    first = (pl.program_id(0) * pl.num_programs(1) + pl.program_id(1)) * n_wide
    span = span_ref[first]
    for j in range(1, n_wide):
        span = jnp.maximum(span, span_ref[first + j])
    wide_ok = span < GLA_WIDE_MAX_SPAN
    args = (q_ref, k_ref, v_ref, la_ref, gn_ref[...], o_ref, s_ref)

    @pl.when(wide_ok)
    def _():
        _gla_wide_chunks(*args)

    @pl.when(jnp.logical_not(wide_ok))
    def _():
        _gla_anchored_chunks(*args)


def _gla(spans, q, k, v, la, gn, batch, seq):
    t = q.shape[0]
    steps = seq // GLA_ROWS

    def row(n):
        return pl.BlockSpec((GLA_ROWS, n), lambda b, i, sp: (b * steps + i, 0))

    return pl.pallas_call(
        _gla_kernel,
        grid_spec=pltpu.PrefetchScalarGridSpec(
            num_scalar_prefetch=1,
            grid=(batch, steps),
            in_specs=[row(GLA_QK), row(GLA_QK), row(GLA_VW), row(GLA_QK), _resident((1, GLA_DV))],
            out_specs=row(GLA_VW),
            scratch_shapes=[pltpu.VMEM((GLA_HEADS, GLA_DK, GLA_DV), F32)]),
        out_shape=jax.ShapeDtypeStruct((t, GLA_VW), BF16),
        compiler_params=pltpu.CompilerParams(dimension_semantics=("arbitrary", "arbitrary"),
                                             vmem_limit_bytes=VMEM_LIMIT),
        name="gla",
    )(spans, q, k, v, la, gn)


def _mix_out_kernel(x_ref, o_ref, sr_ref, z_ref, sga_ref, sgb_ref, wua_ref, pm_ref, ps_ref, wub_ref, wo_ref,
                    y_ref):
    ya = _dot(o_ref[...] * sr_ref[...], wua_ref[...])
    zs = []
    for g in range(POOL_GROUPS):
        cols = slice(g * POOL_GC, (g + 1) * POOL_GC)
        zs.append((_dot(z_ref[:, cols], pm_ref[g]) * ps_ref[:, cols]).astype(BF16))
    yb = _dot(jnp.concatenate(zs, axis=1), wub_ref[...])
    merged = (sga_ref[...].astype(F32) * ya + sgb_ref[...].astype(F32) * yb).astype(BF16)
    y_ref[...] = x_ref[...] + _dot(merged, wo_ref[...])


def _mix_out(x, o, sr, z, sga, sgb, wua, pm, ps, wub, wo):
    t = x.shape[0]

    def row(n):
        return pl.BlockSpec((ROW_TILE, n), lambda i: (i, 0))

    return pl.pallas_call(
        _mix_out_kernel,
        grid=(t // ROW_TILE,),
        in_specs=[row(D_MODEL), row(GLA_VW), row(GLA_VW), row(POOL_WIDTH), row(D_MODEL), row(D_MODEL),
                  _resident((GLA_VW, D_MODEL)), _resident((POOL_GROUPS, POOL_GC, POOL_GC)),
                  _resident((1, POOL_WIDTH)), _resident((POOL_WIDTH, D_MODEL)), _resident((D_MODEL, D_MODEL))],
        out_specs=row(D_MODEL),
        out_shape=jax.ShapeDtypeStruct((t, D_MODEL), F32),
        compiler_params=_params(),
        name="mix_out",
    )(x, o, sr, z, sga, sgb, wua, pm, ps, wub, wo)


def _mem_kv_kernel(m_ref, g_ref, wk_ref, wv_ref, k_ref, v_ref):
    m = _rms(m_ref[...], g_ref[...]).astype(BF16)
    k_ref[...] = _dot(m, wk_ref[...]).astype(BF16)
    v_ref[...] = _dot(m, wv_ref[...]).astype(BF16)


def _mem_kv(mem, g, wk, wv, mem_len):
    t = mem.shape[0]
    row = pl.BlockSpec((mem_len, D_MODEL), lambda i: (i, 0))
    out = jax.ShapeDtypeStruct((t, D_MODEL), BF16)
    return pl.pallas_call(
        _mem_kv_kernel,
        grid=(t // mem_len,),
        in_specs=[row, _resident((1, D_MODEL)), _resident((D_MODEL, D_MODEL)), _resident((D_MODEL, D_MODEL))],
        out_specs=[row, row],
        out_shape=[out, out],
        compiler_params=_params(),
        name="mem_kv",
    )(mem, g, wk, wv)


def _xattn_kernel(x_ref, g_ref, wq_ref, k_ref, v_ref, wo_ref, y_ref):
    x = x_ref[...]
    h = _rms(x, g_ref[...]).astype(BF16)
    q = (_dot(h, wq_ref[...]) * (XA_HD ** -0.5)).astype(BF16)

    def scores(hd):
        cols = slice(hd * XA_HD, (hd + 1) * XA_HD)
        return lax.dot_general(q[:, cols], k_ref[:, cols], (((1,), (1,)), ((), ())), preferred_element_type=F32)

    outs = []
    s = scores(0)
    for hd in range(XA_HEADS):
        s_next = scores(hd + 1) if hd + 1 < XA_HEADS else None
        e = jnp.exp(s - jnp.max(s, axis=-1, keepdims=True))
        p = (e / jnp.sum(e, axis=-1, keepdims=True)).astype(BF16)
        outs.append(_dot(p, v_ref[:, hd * XA_HD:(hd + 1) * XA_HD]).astype(BF16))
        s = s_next
    y_ref[...] = x + _dot(jnp.concatenate(outs, axis=1), wo_ref[...])


def _xattn(x, g, wq, km, vm, wo, seq, mem_len):
    t = x.shape[0]
    tiles_per_seq = seq // ROW_TILE
    row = pl.BlockSpec((ROW_TILE, D_MODEL), lambda i: (i, 0))
    mem = pl.BlockSpec((mem_len, D_MODEL), lambda i: (i // tiles_per_seq, 0))
    return pl.pallas_call(
        _xattn_kernel,
        grid=(t // ROW_TILE,),
        in_specs=[row, _resident((1, D_MODEL)), _resident((D_MODEL, D_MODEL)), mem, mem,
                  _resident((D_MODEL, D_MODEL))],
        out_specs=row,
        out_shape=jax.ShapeDtypeStruct((t, D_MODEL), F32),
        compiler_params=_params(),
        name="xattn",
    )(x, g, wq, km, vm, wo)


def _layer(x, mem, ffn1_norm, ffn1_w1, ffn1_w3, ffn1_w2, mix_norm, w_in, w_alpha, b_alpha, gla_head_norm,
           w_up_a, pool_mix, pool_scale, w_up_b, w_mix_out, xa_norm, mem_norm, xa_wq, xa_wk, xa_wv, xa_wo,
           ffn2_norm, ffn2_w1, ffn2_w3, ffn2_w2, final_norm, last):
    batch, seq, _ = x.shape
    mem_len = mem.shape[1]
    assert seq % ROW_TILE == 0 and seq % GLA_ROWS == 0 and GLA_ROWS % GLA_CHUNK == 0 and GLA_ROWS % GLA_WIDE == 0
    assert ROW_TILE % GLA_WIDE == 0 and ROW_TILE // GLA_WIDE <= 8

    def vec(a):
        return a.reshape(1, -1).astype(F32)

    def bf(a):
        return a.astype(BF16)

    xt = x.reshape(batch * seq, D_MODEL)
    x1 = _ffn(xt, vec(ffn1_norm), bf(ffn1_w1), bf(ffn1_w3), bf(ffn1_w2), vec(final_norm), False)

    sizes = (GLA_QK, GLA_QK, GLA_VW, GLA_VW, GLA_RANK, POOL_WIDTH, D_MODEL, D_MODEL)
    offs = [0]
    for s in sizes:
        offs.append(offs[-1] + s)
    wq, wk, wv, wr, wa, wu, wga, wgb = (bf(w_in[:, offs[i]:offs[i + 1]]) for i in range(len(sizes)))
    wa = jnp.pad(wa, ((0, 0), (0, LANE - GLA_RANK)))
    wal = jnp.pad(bf(w_alpha), ((0, LANE - GLA_RANK), (0, 0)))
    q, k, v, sr, la, z, sga, sgb, span_blk = _mix_in(x1, vec(mix_norm), wq, wk, wv, wr, wa, wu, wga, wgb, wal,
                                                     vec(b_alpha), seq)
    spans = span_blk[:, :ROW_TILE // GLA_WIDE, 0].reshape(-1)
    o = _gla(spans, q, k, v, la, vec(gla_head_norm), batch, seq)
    x2 = _mix_out(x1, o, sr, z, sga, sgb, bf(w_up_a), bf(pool_mix), vec(pool_scale), bf(w_up_b), bf(w_mix_out))

    km, vm = _mem_kv(mem.reshape(batch * mem_len, D_MODEL), vec(mem_norm), bf(xa_wk), bf(xa_wv), mem_len)
    x3 = _xattn(x2, vec(xa_norm), bf(xa_wq), km, vm, bf(xa_wo), seq, mem_len)

    x4 = _ffn(x3, vec(ffn2_norm), bf(ffn2_w1), bf(ffn2_w3), bf(ffn2_w2), vec(final_norm), last)
    return x4.reshape(batch, seq, D_MODEL)


def kernel(x, mem, ffn1_norm, ffn1_w1, ffn1_w3, ffn1_w2, mix_norm, w_in, w_alpha, b_alpha, gla_head_norm, w_up_a,
           pool_mix, pool_scale, w_up_b, w_mix_out, xa_norm, mem_norm, xa_wq, xa_wk, xa_wv, xa_wo, ffn2_norm,
           ffn2_w1, ffn2_w3, ffn2_w2, final_norm):
    depth = ffn1_norm.shape[0]
    for l in range(depth):
        last = l == depth - 1
        x = _layer(x, mem, ffn1_norm[l], ffn1_w1[l], ffn1_w3[l], ffn1_w2[l], mix_norm[l], w_in[l], w_alpha[l],
                   b_alpha[l], gla_head_norm[l], w_up_a[l], pool_mix[l], pool_scale[l], w_up_b[l], w_mix_out[l],
                   xa_norm[l], mem_norm[l], xa_wq[l], xa_wk[l], xa_wv[l], xa_wo[l], ffn2_norm[l], ffn2_w1[l],
                   ffn2_w3[l], ffn2_w2[l], final_norm, last)
    return x
```

```python
import functools

import jax
import jax.numpy as jnp
from jax import lax
from jax.experimental import pallas as pl
from jax.experimental.pallas import tpu as pltpu

F32 = jnp.float32
BF16 = jnp.bfloat16

D_MODEL = 1024
D_FF = 2816
EPS = 1e-6
GLA_HEADS = 4
GLA_DK = 128
GLA_DV = 256
GLA_QK = GLA_HEADS * GLA_DK
GLA_VW = GLA_HEADS * GLA_DV
GLA_RANK = 16
GLA_GATE_TEMP = 16.0
GLA_CHUNK = 64
GLA_SUB = 16
GLA_WIDE = 256
GLA_WIDE_MAX_SPAN = 60.0
POOL_GROUPS = 4
POOL_GC = 128
POOL_WIDTH = POOL_GROUPS * POOL_GC
POOL_WINDOWS = (2, 4, 8, 16)
POOL_HALO = 16
XA_HEADS = 4
XA_HD = 256

LANE = 128
ROW_TILE = 512
GLA_ROWS = 512
FFN_ROWS = 1024
FF_SPLITS = (0, 1024, 2048, D_FF)
VMEM_LIMIT = 56 * 1024 * 1024


def _rms(x, g):
    return x * lax.rsqrt(jnp.mean(x * x, axis=-1, keepdims=True) + EPS) * g


def _dot(a, b):
    return jnp.dot(a, b, preferred_element_type=F32)


def _resident(shape):
    nd = len(shape)
    return pl.BlockSpec(shape, lambda *_: (0,) * nd, pipeline_mode=pl.Buffered(1))


def _params():
    return pltpu.CompilerParams(dimension_semantics=("arbitrary",), vmem_limit_bytes=VMEM_LIMIT)


def _ffn_kernel(x_ref, g_ref, w1_ref, w3_ref, w2_ref, fg_ref, o_ref, *, final_norm):
    x = x_ref[...]
    h = _rms(x, g_ref[...]).astype(BF16)
    acc = jnp.zeros_like(x)
    for lo, hi in zip(FF_SPLITS[:-1], FF_SPLITS[1:]):
        a = _dot(h, w1_ref[:, lo:hi])
        b = _dot(h, w3_ref[:, lo:hi])
        act = (a * jax.nn.sigmoid(a) * b).astype(BF16)
        acc = acc + _dot(act, w2_ref[lo:hi, :])
    y = x + 0.5 * acc
    if final_norm:
        y = _rms(y, fg_ref[...])
    o_ref[...] = y


def _ffn(x, g, w1, w3, w2, fg, final_norm):
    t = x.shape[0]
    row = pl.BlockSpec((FFN_ROWS, D_MODEL), lambda i: (i, 0))
    return pl.pallas_call(
        functools.partial(_ffn_kernel, final_norm=final_norm),
        grid=(t // FFN_ROWS,),
        in_specs=[row, _resident((1, D_MODEL)), _resident((D_MODEL, D_FF)), _resident((D_MODEL, D_FF)),
                  _resident((D_FF, D_MODEL)), _resident((1, D_MODEL))],
        out_specs=row,
        out_shape=jax.ShapeDtypeStruct((t, D_MODEL), F32),
        compiler_params=_params(),
        name="ffn_final" if final_norm else "ffn",
    )(x, g, w1, w3, w2, fg)


def _mix_in_kernel(x_ref, g_ref, wq_ref, wk_ref, wv_ref, wr_ref, wa_ref, wu_ref, wga_ref, wgb_ref,
                   wal_ref, bal_ref,
                   q_ref, k_ref, v_ref, sr_ref, la_ref, z_ref, sga_ref, sgb_ref, span_ref,
                   ubuf, *, tiles_per_seq):
    tile_in_seq = pl.program_id(0) % tiles_per_seq

    @pl.when(tile_in_seq == 0)
    def _():
        ubuf[0:POOL_HALO, :] = jnp.zeros((POOL_HALO, POOL_WIDTH), F32)

    h = _rms(x_ref[...], g_ref[...]).astype(BF16)
    a_code = _dot(h, wa_ref[...]).astype(BF16)
    u = _dot(h, wu_ref[...])
    r = _dot(h, wr_ref[...])
    sr_ref[...] = (r * jax.nn.sigmoid(r)).astype(BF16)

    zg = _dot(a_code, wal_ref[...]) + bal_ref[...]
    q_ref[...] = (_dot(h, wq_ref[...]) * (GLA_DK ** -0.5)).astype(BF16)
    sga_ref[...] = jax.nn.sigmoid(_dot(h, wga_ref[...])).astype(BF16)
    v_ref[...] = _dot(h, wv_ref[...]).astype(BF16)
    sgb_ref[...] = jax.nn.sigmoid(_dot(h, wgb_ref[...])).astype(BF16)
    k_ref[...] = _dot(h, wk_ref[...]).astype(BF16)

    la = (jnp.minimum(zg, 0.0) - jnp.log(1.0 + jnp.exp(-jnp.abs(zg)))) * (1.0 / GLA_GATE_TEMP)
    la_ref[...] = la
    spans = [jnp.max(-jnp.sum(la[r:r + GLA_WIDE, :], axis=0, keepdims=True), axis=1, keepdims=True)
             for r in range(0, ROW_TILE, GLA_WIDE)]
    sub = lax.broadcasted_iota(jnp.int32, (8, LANE), 0)
    blk = jnp.zeros((8, LANE), F32)
    for j, sp in enumerate(spans):
        blk = jnp.where(sub == j, sp, blk)
    span_ref[0] = blk

    ubuf[POOL_HALO:POOL_HALO + ROW_TILE, :] = u
    pos = lax.broadcasted_iota(jnp.int32, (ROW_TILE, 1), 0) + tile_in_seq * ROW_TILE
    for g, w in enumerate(POOL_WINDOWS):
        cols = slice(g * POOL_GC, (g + 1) * POOL_GC)
        ug = u[:, cols]
        s = ug
        for sh in range(1, w):
            s = s + ubuf[POOL_HALO - sh:POOL_HALO - sh + ROW_TILE, cols]
        cnt = jnp.minimum(pos + 1, w).astype(F32)
        z_ref[:, cols] = (s / cnt - ug).astype(BF16)
    ubuf[0:POOL_HALO, :] = ubuf[ROW_TILE:ROW_TILE + POOL_HALO, :]


def _mix_in(x, g, wq, wk, wv, wr, wa, wu, wga, wgb, wal, bal, seq):
    t = x.shape[0]

    def row(n):
        return pl.BlockSpec((ROW_TILE, n), lambda i: (i, 0))

    def out(n, dt):
        return jax.ShapeDtypeStruct((t, n), dt)

    return pl.pallas_call(
        functools.partial(_mix_in_kernel, tiles_per_seq=seq // ROW_TILE),
        grid=(t // ROW_TILE,),
        in_specs=[row(D_MODEL), _resident((1, D_MODEL)),
                  _resident((D_MODEL, GLA_QK)), _resident((D_MODEL, GLA_QK)), _resident((D_MODEL, GLA_VW)),
                  _resident((D_MODEL, GLA_VW)), _resident((D_MODEL, LANE)), _resident((D_MODEL, POOL_WIDTH)),
                  _resident((D_MODEL, D_MODEL)), _resident((D_MODEL, D_MODEL)),
                  _resident((LANE, GLA_QK)), _resident((1, GLA_QK))],
        out_specs=[row(GLA_QK), row(GLA_QK), row(GLA_VW), row(GLA_VW), row(GLA_QK), row(POOL_WIDTH),
                   row(D_MODEL), row(D_MODEL), pl.BlockSpec((1, 8, LANE), lambda i: (i, 0, 0))],
        out_shape=[out(GLA_QK, BF16), out(GLA_QK, BF16), out(GLA_VW, BF16), out(GLA_VW, BF16),
                   out(GLA_QK, F32), out(POOL_WIDTH, BF16), out(D_MODEL, BF16), out(D_MODEL, BF16),
                   jax.ShapeDtypeStruct((t // ROW_TILE, 8, LANE), F32)],
        scratch_shapes=[pltpu.VMEM((POOL_HALO + ROW_TILE, POOL_WIDTH), F32)],
        compiler_params=_params(),
        name="mix_in",
    )(x, g, wq, wk, wv, wr, wa, wu, wga, wgb, wal, bal)


def _cumsum_rows(mask_bf16, la):
    hi = la.astype(BF16)
    lo = (la - hi.astype(F32)).astype(BF16)
    return _dot(mask_bf16, hi) + _dot(mask_bf16, lo)


def _decay_column(b_last_row):
    return jnp.transpose(jnp.broadcast_to(jnp.exp(b_last_row), (8, GLA_DK)))[:, 0:1]


def _gla_wide_chunks(q_ref, k_ref, v_ref, la_ref, gn, o_ref, s_ref):
    c = GLA_WIDE
    ri = lax.broadcasted_iota(jnp.int32, (c, c), 0)
    ci = lax.broadcasted_iota(jnp.int32, (c, c), 1)
    causal = ci <= ri
    cum_incl = causal.astype(BF16)
    row_sl = [slice(ic * c, (ic + 1) * c) for ic in range(GLA_ROWS // c)]
    b_all = [_cumsum_rows(cum_incl, la_ref[rows, :]) for rows in row_sl]
    units = [(ic, hd) for ic in range(len(row_sl)) for hd in range(GLA_HEADS)]

    def factors(ic, hd):
        rows, kc = row_sl[ic], slice(hd * GLA_DK, (hd + 1) * GLA_DK)
        b = b_all[ic][:, kc]
        qg = (q_ref[rows, kc].astype(F32) * jnp.exp(b)).astype(BF16)
        kg = (k_ref[rows, kc].astype(F32) * jnp.exp(-b)).astype(BF16)
        s = lax.dot_general(qg, kg, (((1,), (1,)), ((), ())), preferred_element_type=F32)
        return qg, kg, s, b[c - 1:c, :]

    nxt = factors(*units[0])
    for n, (ic, hd) in enumerate(units):
        qg, kg, s, b_last = nxt
        if n + 1 < len(units):
            nxt = factors(*units[n + 1])
        rows, vc = row_sl[ic], slice(hd * GLA_DV, (hd + 1) * GLA_DV)
        v = v_ref[rows, vc]
        s_old = s_ref[hd]
        inter = _dot(qg, s_old.astype(BF16))
        upd = lax.dot_general(kg, v, (((0,), (0,)), ((), ())), preferred_element_type=F32)
        p = jnp.where(causal, s, 0.0).astype(BF16)
        o = _dot(p, v) + inter
        s_ref[hd] = (s_old + upd) * _decay_column(b_last)
        o_ref[rows, vc] = _rms(o, gn).astype(BF16)


def _gla_anchored_chunks(q_ref, k_ref, v_ref, la_ref, gn, o_ref, s_ref):
    c = GLA_CHUNK
    ri = lax.broadcasted_iota(jnp.int32, (c, c), 0)
    ci = lax.broadcasted_iota(jnp.int32, (c, c), 1)
    causal = ci <= ri
    cum_incl = causal.astype(BF16)
    cum_anchor = (ci < (ri // GLA_SUB) * GLA_SUB).astype(BF16)
    krow = lax.broadcasted_iota(jnp.int32, (c, 1), 0)

    def chunk(ic, carry):
        r0 = pl.multiple_of(ic * c, c)
        la = la_ref[pl.ds(r0, c), :]
        b_all = _cumsum_rows(cum_incl, la)
        beta_all = _cumsum_rows(cum_anchor, la)
        for hd in range(GLA_HEADS):
            kc = slice(hd * GLA_DK, (hd + 1) * GLA_DK)
            vc = slice(hd * GLA_DV, (hd + 1) * GLA_DV)
            q = q_ref[pl.ds(r0, c), kc].astype(F32)
            k = k_ref[pl.ds(r0, c), kc].astype(F32)
            v = v_ref[pl.ds(r0, c), vc]
            b = b_all[:, kc]
            beta = beta_all[:, kc]
            qa = (q * jnp.exp(b - beta)).astype(BF16)
            rows = []
            for blk in range(c // GLA_SUB):
                r_lo, r_hi = blk * GLA_SUB, (blk + 1) * GLA_SUB
                e = jnp.where(krow < r_hi, beta[r_lo:r_lo + 1, :] - b, 0.0)
                ka = (k * jnp.exp(e)).astype(BF16)
                rows.append(lax.dot_general(qa[r_lo:r_hi, :], ka, (((1,), (1,)), ((), ())),
                                            preferred_element_type=F32))
            p = jnp.where(causal, jnp.concatenate(rows, axis=0), 0.0).astype(BF16)
            s_old = s_ref[hd]
            o = _dot(p, v) + _dot((q * jnp.exp(b)).astype(BF16), s_old.astype(BF16))
            b_last = b[c - 1:c, :]
            k_dec = (k * jnp.exp(b_last - b)).astype(BF16)
            upd = lax.dot_general(k_dec, v, (((0,), (0,)), ((), ())), preferred_element_type=F32)
            s_ref[hd] = s_old * _decay_column(b_last) + upd
            o_ref[pl.ds(r0, c), vc] = _rms(o, gn).astype(BF16)
        return carry

    lax.fori_loop(0, GLA_ROWS // c, chunk, 0)


def _gla_kernel(span_ref, q_ref, k_ref, v_ref, la_ref, gn_ref, o_ref, s_ref):
    @pl.when(pl.program_id(1) == 0)
    def _():
        s_ref[...] = jnp.zeros_like(s_ref)

    n_wide = GLA_ROWS // GLA_WIDE
    first = (pl.program_id(0) * pl.num_programs(1) + pl.program_id(1)) * n_wide
    span = span_ref[first]
    for j in range(1, n_wide):
        span = jnp.maximum(span, span_ref[first + j])
    wide_ok = span < GLA_WIDE_MAX_SPAN
    args = (q_ref, k_ref, v_ref, la_ref, gn_ref[...], o_ref, s_ref)

    @pl.when(wide_ok)
    def _():
        _gla_wide_chunks(*args)

    @pl.when(jnp.logical_not(wide_ok))
    def _():
        _gla_anchored_chunks(*args)


def _gla(spans, q, k, v, la, gn, batch, seq):
    t = q.shape[0]
    steps = seq // GLA_ROWS

    def row(n):
        return pl.BlockSpec((GLA_ROWS, n), lambda b, i, sp: (b * steps + i, 0))

    return pl.pallas_call(
        _gla_kernel,
        grid_spec=pltpu.PrefetchScalarGridSpec(
            num_scalar_prefetch=1,
            grid=(batch, steps),
            in_specs=[row(GLA_QK), row(GLA_QK), row(GLA_VW), row(GLA_QK), _resident((1, GLA_DV))],
            out_specs=row(GLA_VW),
            scratch_shapes=[pltpu.VMEM((GLA_HEADS, GLA_DK, GLA_DV), F32)]),
        out_shape=jax.ShapeDtypeStruct((t, GLA_VW), BF16),
        compiler_params=pltpu.CompilerParams(dimension_semantics=("arbitrary", "arbitrary"),
                                             vmem_limit_bytes=VMEM_LIMIT),
        name="gla",
    )(spans, q, k, v, la, gn)


def _mem_kv_kernel(m_ref, g_ref, wk_ref, wv_ref, k_ref, v_ref):
    m = _rms(m_ref[...], g_ref[...]).astype(BF16)
    k_ref[...] = _dot(m, wk_ref[...]).astype(BF16)
    v_ref[...] = _dot(m, wv_ref[...]).astype(BF16)


def _mem_kv(mem, g, wk, wv, mem_len):
    t = mem.shape[0]
    row = pl.BlockSpec((mem_len, D_MODEL), lambda i: (i, 0))
    out = jax.ShapeDtypeStruct((t, D_MODEL), BF16)
    return pl.pallas_call(
        _mem_kv_kernel,
        grid=(t // mem_len,),
        in_specs=[row, _resident((1, D_MODEL)), _resident((D_MODEL, D_MODEL)), _resident((D_MODEL, D_MODEL))],
        out_specs=[row, row],
        out_shape=[out, out],
        compiler_params=_params(),
        name="mem_kv",
    )(mem, g, wk, wv)


def _mix_xattn_kernel(x_ref, o_ref, sr_ref, z_ref, sga_ref, sgb_ref, wua_ref, pm_ref, ps_ref, wub_ref, wmo_ref,
                      xg_ref, wq_ref, k_ref, v_ref, wo_ref, y_ref):
    zs = []
    for g in range(POOL_GROUPS):
        cols = slice(g * POOL_GC, (g + 1) * POOL_GC)
        zs.append((_dot(z_ref[:, cols], pm_ref[g]) * ps_ref[:, cols]).astype(BF16))
    ya = _dot(o_ref[...] * sr_ref[...], wua_ref[...])
    yb = _dot(jnp.concatenate(zs, axis=1), wub_ref[...])
    merged = (sga_ref[...].astype(F32) * ya + sgb_ref[...].astype(F32) * yb).astype(BF16)
    x = x_ref[...] + _dot(merged, wmo_ref[...])

    h = _rms(x, xg_ref[...]).astype(BF16)
    q = (_dot(h, wq_ref[...]) * (XA_HD ** -0.5)).astype(BF16)

    def scores(hd):
        cols = slice(hd * XA_HD, (hd + 1) * XA_HD)
        return lax.dot_general(q[:, cols], k_ref[:, cols], (((1,), (1,)), ((), ())), preferred_element_type=F32)

    outs = []
    s = scores(0)
    for hd in range(XA_HEADS):
        s_next = scores(hd + 1) if hd + 1 < XA_HEADS else None
        e = jnp.exp(s - jnp.max(s, axis=-1, keepdims=True))
        p = (e / jnp.sum(e, axis=-1, keepdims=True)).astype(BF16)
        outs.append(_dot(p, v_ref[:, hd * XA_HD:(hd + 1) * XA_HD]).astype(BF16))
        s = s_next
    y_ref[...] = x + _dot(jnp.concatenate(outs, axis=1), wo_ref[...])


def _mix_xattn(x, o, sr, z, sga, sgb, wua, pm, ps, wub, wmo, xg, wq, km, vm, wo, seq, mem_len):
    t = x.shape[0]
    tiles_per_seq = seq // ROW_TILE

    def row(n):
        return pl.BlockSpec((ROW_TILE, n), lambda i: (i, 0))

    mem = pl.BlockSpec((mem_len, D_MODEL), lambda i: (i // tiles_per_seq, 0))
    return pl.pallas_call(
        _mix_xattn_kernel,
        grid=(t // ROW_TILE,),
        in_specs=[row(D_MODEL), row(GLA_VW), row(GLA_VW), row(POOL_WIDTH), row(D_MODEL), row(D_MODEL),
                  _resident((GLA_VW, D_MODEL)), _resident((POOL_GROUPS, POOL_GC, POOL_GC)),
                  _resident((1, POOL_WIDTH)), _resident((POOL_WIDTH, D_MODEL)), _resident((D_MODEL, D_MODEL)),
                  _resident((1, D_MODEL)), _resident((D_MODEL, D_MODEL)), mem, mem, _resident((D_MODEL, D_MODEL))],
        out_specs=row(D_MODEL),
        out_shape=jax.ShapeDtypeStruct((t, D_MODEL), F32),
        compiler_params=_params(),
        name="mix_xattn",
    )(x, o, sr, z, sga, sgb, wua, pm, ps, wub, wmo, xg, wq, km, vm, wo)


def _layer(x, mem, ffn1_norm, ffn1_w1, ffn1_w3, ffn1_w2, mix_norm, w_in, w_alpha, b_alpha, gla_head_norm,
           w_up_a, pool_mix, pool_scale, w_up_b, w_mix_out, xa_norm, mem_norm, xa_wq, xa_wk, xa_wv, xa_wo,
           ffn2_norm, ffn2_w1, ffn2_w3, ffn2_w2, final_norm, last):
    batch, seq, _ = x.shape
    mem_len = mem.shape[1]
    assert seq % ROW_TILE == 0 and seq % GLA_ROWS == 0 and GLA_ROWS % GLA_CHUNK == 0 and GLA_ROWS % GLA_WIDE == 0
    assert ROW_TILE % GLA_WIDE == 0 and ROW_TILE // GLA_WIDE <= 8

    def vec(a):
        return a.reshape(1, -1).astype(F32)

    def bf(a):
        return a.astype(BF16)

    xt = x.reshape(batch * seq, D_MODEL)
    x1 = _ffn(xt, vec(ffn1_norm), bf(ffn1_w1), bf(ffn1_w3), bf(ffn1_w2), vec(final_norm), False)

    sizes = (GLA_QK, GLA_QK, GLA_VW, GLA_VW, GLA_RANK, POOL_WIDTH, D_MODEL, D_MODEL)
    offs = [0]
    for s in sizes:
        offs.append(offs[-1] + s)
    wq, wk, wv, wr, wa, wu, wga, wgb = (bf(w_in[:, offs[i]:offs[i + 1]]) for i in range(len(sizes)))
    wa = jnp.pad(wa, ((0, 0), (0, LANE - GLA_RANK)))
    wal = jnp.pad(bf(w_alpha), ((0, LANE - GLA_RANK), (0, 0)))
    q, k, v, sr, la, z, sga, sgb, span_blk = _mix_in(x1, vec(mix_norm), wq, wk, wv, wr, wa, wu, wga, wgb, wal,
                                                     vec(b_alpha), seq)
    spans = span_blk[:, :ROW_TILE // GLA_WIDE, 0].reshape(-1)
    o = _gla(spans, q, k, v, la, vec(gla_head_norm), batch, seq)
    km, vm = _mem_kv(mem.reshape(batch * mem_len, D_MODEL), vec(mem_norm), bf(xa_wk), bf(xa_wv), mem_len)
    x3 = _mix_xattn(x1, o, sr, z, sga, sgb, bf(w_up_a), bf(pool_mix), vec(pool_scale), bf(w_up_b), bf(w_mix_out),
                    vec(xa_norm), bf(xa_wq), km, vm, bf(xa_wo), seq, mem_len)

    x4 = _ffn(x3, vec(ffn2_norm), bf(ffn2_w1), bf(ffn2_w3), bf(ffn2_w2), vec(final_norm), last)
    return x4.reshape(batch, seq, D_MODEL)


def kernel(x, mem, ffn1_norm, ffn1_w1, ffn1_w3, ffn1_w2, mix_norm, w_in, w_alpha, b_alpha, gla_head_norm, w_up_a,
           pool_mix, pool_scale, w_up_b, w_mix_out, xa_norm, mem_norm, xa_wq, xa_wk, xa_wv, xa_wo, ffn2_norm,
           ffn2_w1, ffn2_w3, ffn2_w2, final_norm):
    depth = ffn1_norm.shape[0]
    for l in range(depth):
        last = l == depth - 1
        x = _layer(x, mem, ffn1_norm[l], ffn1_w1[l], ffn1_w3[l], ffn1_w2[l], mix_norm[l], w_in[l], w_alpha[l],
                   b_alpha[l], gla_head_norm[l], w_up_a[l], pool_mix[l], pool_scale[l], w_up_b[l], w_mix_out[l],
                   xa_norm[l], mem_norm[l], xa_wq[l], xa_wk[l], xa_wv[l], xa_wo[l], ffn2_norm[l], ffn2_w1[l],
                   ffn2_w3[l], ffn2_w2[l], final_norm, last)
    return x
```

```python
import functools

import jax
import jax.numpy as jnp
from jax import lax
from jax.experimental import pallas as pl
from jax.experimental.pallas import tpu as pltpu

F32 = jnp.float32
BF16 = jnp.bfloat16

D_MODEL = 1024
D_FF = 2816
EPS = 1e-6
GLA_HEADS = 4
GLA_DK = 128
GLA_DV = 256
GLA_QK = GLA_HEADS * GLA_DK
GLA_VW = GLA_HEADS * GLA_DV
GLA_RANK = 16
GLA_GATE_TEMP = 16.0
GLA_CHUNK = 64
GLA_SUB = 16
GLA_WIDE = 256
GLA_WIDE_MAX_SPAN = 60.0
POOL_GROUPS = 4
POOL_GC = 128
POOL_WIDTH = POOL_GROUPS * POOL_GC
POOL_WINDOWS = (2, 4, 8, 16)
POOL_HALO = 16
XA_HEADS = 4
XA_HD = 256

LANE = 128
ROW_TILE = 512
FFN_ROWS = 1024
FF_SPLITS = (0, 1024, 2048, D_FF)
VMEM_LIMIT = 56 * 1024 * 1024


def _rms(x, g):
    return x * lax.rsqrt(jnp.mean(x * x, axis=-1, keepdims=True) + EPS) * g


def _sigmoid(x):
    return 0.5 * jnp.tanh(0.5 * x) + 0.5


def _dot(a, b):
    return jnp.dot(a, b, preferred_element_type=F32)


def _resident(shape):
    nd = len(shape)
    return pl.BlockSpec(shape, lambda *_: (0,) * nd, pipeline_mode=pl.Buffered(1))


def _params():
    return pltpu.CompilerParams(dimension_semantics=("arbitrary",), vmem_limit_bytes=VMEM_LIMIT)


def _ffn_kernel(x_ref, g_ref, w1_ref, w3_ref, w2_ref, fg_ref, o_ref, *, final_norm):
    x = x_ref[...]
    h = _rms(x, g_ref[...]).astype(BF16)
    acc = jnp.zeros_like(x)
    for lo, hi in zip(FF_SPLITS[:-1], FF_SPLITS[1:]):
        a = _dot(h, w1_ref[:, lo:hi])
        b = _dot(h, w3_ref[:, lo:hi])
        act = (a * jax.nn.sigmoid(a) * b).astype(BF16)
        acc = acc + _dot(act, w2_ref[lo:hi, :])
    y = x + 0.5 * acc
    if final_norm:
        y = _rms(y, fg_ref[...])
    o_ref[...] = y


def _ffn(x, g, w1, w3, w2, fg, final_norm):
    t = x.shape[0]
    row = pl.BlockSpec((FFN_ROWS, D_MODEL), lambda i: (i, 0))
    return pl.pallas_call(
        functools.partial(_ffn_kernel, final_norm=final_norm),
        grid=(t // FFN_ROWS,),
        in_specs=[row, _resident((1, D_MODEL)), _resident((D_MODEL, D_FF)), _resident((D_MODEL, D_FF)),
                  _resident((D_FF, D_MODEL)), _resident((1, D_MODEL))],
        out_specs=row,
        out_shape=jax.ShapeDtypeStruct((t, D_MODEL), F32),
        compiler_params=_params(),
        name="ffn_final" if final_norm else "ffn",
    )(x, g, w1, w3, w2, fg)


def _cumsum_rows(mask_bf16, la):
    hi = la.astype(BF16)
    lo = (la - hi.astype(F32)).astype(BF16)
    return _dot(mask_bf16, hi) + _dot(mask_bf16, lo)


def _decay_column(b_last_row):
    return jnp.transpose(jnp.broadcast_to(jnp.exp(b_last_row), (8, GLA_DK)))[:, 0:1]


def _gla_wide_task(q_ref, k_ref, v_ref, la_ref, gn, o_ref, s_ref):
    c = GLA_WIDE
    ri = lax.broadcasted_iota(jnp.int32, (c, c), 0)
    ci = lax.broadcasted_iota(jnp.int32, (c, c), 1)
    causal = ci <= ri
    cum_incl = causal.astype(BF16)
    row_sl = [slice(ic * c, (ic + 1) * c) for ic in range(ROW_TILE // c)]
    b_all = [_cumsum_rows(cum_incl, la_ref[rows, :]) for rows in row_sl]
    units = [(ic, hd) for ic in range(len(row_sl)) for hd in range(GLA_HEADS)]

    def factors(ic, hd):
        rows, kc = row_sl[ic], slice(hd * GLA_DK, (hd + 1) * GLA_DK)
        b = b_all[ic][:, kc]
        qg = (q_ref[rows, kc].astype(F32) * jnp.exp(b)).astype(BF16)
        kg = (k_ref[rows, kc].astype(F32) * jnp.exp(-b)).astype(BF16)
        s = lax.dot_general(qg, kg, (((1,), (1,)), ((), ())), preferred_element_type=F32)
        return qg, kg, s, b[c - 1:c, :]

    nxt = factors(*units[0])
    yield
    for n, (ic, hd) in enumerate(units):
        qg, kg, s, b_last = nxt
        if n + 1 < len(units):
            nxt = factors(*units[n + 1])
        rows, vc = row_sl[ic], slice(hd * GLA_DV, (hd + 1) * GLA_DV)
        v = v_ref[rows, vc]
        s_old = s_ref[hd]
        inter = _dot(qg, s_old.astype(BF16))
        upd = lax.dot_general(kg, v, (((0,), (0,)), ((), ())), preferred_element_type=F32)
        p = jnp.where(causal, s, 0.0).astype(BF16)
        o = _dot(p, v) + inter
        s_ref[hd] = (s_old + upd) * _decay_column(b_last)
        o_ref[rows, vc] = _rms(o, gn).astype(BF16)
        yield


def _gla_anchored_chunks(q_ref, k_ref, v_ref, la_ref, gn, o_ref, s_ref):
    c = GLA_CHUNK
    ri = lax.broadcasted_iota(jnp.int32, (c, c), 0)
    ci = lax.broadcasted_iota(jnp.int32, (c, c), 1)
    causal = ci <= ri
    cum_incl = causal.astype(BF16)
    cum_anchor = (ci < (ri // GLA_SUB) * GLA_SUB).astype(BF16)
    krow = lax.broadcasted_iota(jnp.int32, (c, 1), 0)

    def chunk(ic, carry):
        r0 = pl.multiple_of(ic * c, c)
        la = la_ref[pl.ds(r0, c), :]
        b_all = _cumsum_rows(cum_incl, la)
        beta_all = _cumsum_rows(cum_anchor, la)
        for hd in range(GLA_HEADS):
            kc = slice(hd * GLA_DK, (hd + 1) * GLA_DK)
            vc = slice(hd * GLA_DV, (hd + 1) * GLA_DV)
            q = q_ref[pl.ds(r0, c), kc].astype(F32)
            k = k_ref[pl.ds(r0, c), kc].astype(F32)
            v = v_ref[pl.ds(r0, c), vc]
            b = b_all[:, kc]
            beta = beta_all[:, kc]
            qa = (q * jnp.exp(b - beta)).astype(BF16)
            rows = []
            for blk in range(c // GLA_SUB):
                r_lo, r_hi = blk * GLA_SUB, (blk + 1) * GLA_SUB
                e = jnp.where(krow < r_hi, beta[r_lo:r_lo + 1, :] - b, 0.0)
                ka = (k * jnp.exp(e)).astype(BF16)
                rows.append(lax.dot_general(qa[r_lo:r_hi, :], ka, (((1,), (1,)), ((), ())),
                                            preferred_element_type=F32))
            p = jnp.where(causal, jnp.concatenate(rows, axis=0), 0.0).astype(BF16)
            s_old = s_ref[hd]
            o = _dot(p, v) + _dot((q * jnp.exp(b)).astype(BF16), s_old.astype(BF16))
            b_last = b[c - 1:c, :]
            k_dec = (k * jnp.exp(b_last - b)).astype(BF16)
            upd = lax.dot_general(k_dec, v, (((0,), (0,)), ((), ())), preferred_element_type=F32)
            s_ref[hd] = s_old * _decay_column(b_last) + upd
            o_ref[pl.ds(r0, c), vc] = _rms(o, gn).astype(BF16)
        return carry

    lax.fori_loop(0, ROW_TILE // c, chunk, 0)


def _interleave(tasks):
    tasks = list(tasks)
    while tasks:
        for t in list(tasks):
            try:
                next(t)
            except StopIteration:
                tasks.remove(t)


def _mix_in_task(x_ref, g_ref, wq_ref, wk_ref, wv_ref, wr_ref, wa_ref, wu_ref, wga_ref, wgb_ref, wal_ref, bal_ref,
                 sr_ref, z_ref, sga_ref, sgb_ref, ubuf, tile_in_seq, keep_halo, gla_inputs):
    h = _rms(x_ref[...], g_ref[...]).astype(BF16)
    a_code = _dot(h, wa_ref[...]).astype(BF16)
    yield
    u = _dot(h, wu_ref[...])
    yield
    r = _dot(h, wr_ref[...])
    sr_ref[...] = (r * _sigmoid(r)).astype(BF16)
    yield
    zg = _dot(a_code, wal_ref[...]) + bal_ref[...]
    yield
    sga_ref[...] = _sigmoid(_dot(h, wga_ref[...])).astype(BF16)
    yield
    sgb_ref[...] = _sigmoid(_dot(h, wgb_ref[...])).astype(BF16)
    yield

    ubuf[POOL_HALO:POOL_HALO + ROW_TILE, :] = u
    pos = lax.broadcasted_iota(jnp.int32, (ROW_TILE, 1), 0) + tile_in_seq * ROW_TILE
    for g, w in enumerate(POOL_WINDOWS):
        cols = slice(g * POOL_GC, (g + 1) * POOL_GC)
        s = ubuf[:, cols]
        sh = 1
        while sh < w:
            s = s + pltpu.roll(s, sh, 0)
            sh *= 2
        cnt = jnp.minimum(pos + 1, w).astype(F32)
        z_ref[:, cols] = (s[POOL_HALO:, :] / cnt - u[:, cols]).astype(BF16)
    ubuf[0:POOL_HALO, :] = jnp.where(keep_halo, ubuf[0:POOL_HALO, :], ubuf[ROW_TILE:ROW_TILE + POOL_HALO, :])

    q = (_dot(h, wq_ref[...]) * (GLA_DK ** -0.5)).astype(BF16)
    yield
    v = _dot(h, wv_ref[...]).astype(BF16)
    yield
    k = _dot(h, wk_ref[...]).astype(BF16)
    la = (jnp.minimum(zg, 0.0) - jnp.log(1.0 + jnp.exp(-jnp.abs(zg)))) * (1.0 / GLA_GATE_TEMP)
    span = None
    for r0 in range(0, ROW_TILE, GLA_WIDE):
        sp = jnp.max(-jnp.sum(la[r0:r0 + GLA_WIDE, :], axis=0, keepdims=True))
        span = sp if span is None else jnp.maximum(span, sp)
    gla_inputs.extend((q, k, v, la, span))


def _mix_gla_kernel(x_ref, g_ref, wq_ref, wk_ref, wv_ref, wr_ref, wa_ref, wu_ref, wga_ref, wgb_ref,
                    wal_ref, bal_ref, gn_ref,
                    sr_ref, z_ref, sga_ref, sgb_ref, o_ref,
                    q_s, k_s, v_s, la_s, s_ref, ubuf, span_s, *, tiles_per_seq, n_tiles):
    step = pl.program_id(0)
    mix_tile = jnp.minimum(step, n_tiles - 1)
    gla_tile = step - 1

    @pl.when(step == 0)
    def _():
        q_s[...] = jnp.zeros_like(q_s)
        k_s[...] = jnp.zeros_like(k_s)
        v_s[...] = jnp.zeros_like(v_s)
        la_s[...] = jnp.zeros_like(la_s)
        span_s[0] = 0.0

    @pl.when(mix_tile % tiles_per_seq == 0)
    def _():
        ubuf[0:POOL_HALO, :] = jnp.zeros((POOL_HALO, POOL_WIDTH), F32)

    @pl.when(jnp.logical_or(step == 0, gla_tile % tiles_per_seq == 0))
    def _():
        s_ref[...] = jnp.zeros_like(s_ref)

    keep_halo = step >= n_tiles - 1
    mix_args = (x_ref, g_ref, wq_ref, wk_ref, wv_ref, wr_ref, wa_ref, wu_ref, wga_ref, wgb_ref, wal_ref, bal_ref,
                sr_ref, z_ref, sga_ref, sgb_ref, ubuf, mix_tile % tiles_per_seq, keep_halo)
    gla_args = (q_s, k_s, v_s, la_s, gn_ref[...], o_ref, s_ref)

    def run(gla_tasks):
        nxt = []
        _interleave(gla_tasks + [_mix_in_task(*mix_args, nxt)])
        q_s[...], k_s[...], v_s[...], la_s[...], span_s[0] = nxt

    wide_ok = span_s[0] < GLA_WIDE_MAX_SPAN

    @pl.when(wide_ok)
    def _():
        run([_gla_wide_task(*gla_args)])

    @pl.when(jnp.logical_not(wide_ok))
    def _():
        _gla_anchored_chunks(*gla_args)
        run([])


def _mix_gla(x, g, wq, wk, wv, wr, wa, wu, wga, wgb, wal, bal, gn, seq):
    t = x.shape[0]
    n_tiles = t // ROW_TILE

    def cur(n):
        return pl.BlockSpec((ROW_TILE, n), lambda i: (jnp.minimum(i, n_tiles - 1), 0))

    prev = pl.BlockSpec((ROW_TILE, GLA_VW), lambda i: (jnp.maximum(i - 1, 0), 0))

    def out(n):
        return jax.ShapeDtypeStruct((t, n), BF16)

    return pl.pallas_call(
        functools.partial(_mix_gla_kernel, tiles_per_seq=seq // ROW_TILE, n_tiles=n_tiles),
        grid=(n_tiles + 1,),
        in_specs=[cur(D_MODEL), _resident((1, D_MODEL)),
                  _resident((D_MODEL, GLA_QK)), _resident((D_MODEL, GLA_QK)), _resident((D_MODEL, GLA_VW)),
                  _resident((D_MODEL, GLA_VW)), _resident((D_MODEL, LANE)), _resident((D_MODEL, POOL_WIDTH)),
                  _resident((D_MODEL, D_MODEL)), _resident((D_MODEL, D_MODEL)),
                  _resident((LANE, GLA_QK)), _resident((1, GLA_QK)), _resident((1, GLA_DV))],
        out_specs=[cur(GLA_VW), cur(POOL_WIDTH), cur(D_MODEL), cur(D_MODEL), prev],
        out_shape=[out(GLA_VW), out(POOL_WIDTH), out(D_MODEL), out(D_MODEL), out(GLA_VW)],
        scratch_shapes=[pltpu.VMEM((ROW_TILE, GLA_QK), BF16), pltpu.VMEM((ROW_TILE, GLA_QK), BF16),
                        pltpu.VMEM((ROW_TILE, GLA_VW), BF16), pltpu.VMEM((ROW_TILE, GLA_QK), F32),
                        pltpu.VMEM((GLA_HEADS, GLA_DK, GLA_DV), F32),
                        pltpu.VMEM((POOL_HALO + ROW_TILE, POOL_WIDTH), F32),
                        pltpu.SMEM((1,), F32)],
        compiler_params=_params(),
        name="mix_gla",
    )(x, g, wq, wk, wv, wr, wa, wu, wga, wgb, wal, bal, gn)


def _mem_kv_kernel(m_ref, g_ref, wk_ref, wv_ref, k_ref, v_ref):
    m = _rms(m_ref[...], g_ref[...]).astype(BF16)
    k_ref[...] = _dot(m, wk_ref[...]).astype(BF16)
    v_ref[...] = _dot(m, wv_ref[...]).astype(BF16)


def _mem_kv(mem, g, wk, wv, mem_len):
    t = mem.shape[0]
    row = pl.BlockSpec((mem_len, D_MODEL), lambda i: (i, 0))
    out = jax.ShapeDtypeStruct((t, D_MODEL), BF16)
    return pl.pallas_call(
        _mem_kv_kernel,
        grid=(t // mem_len,),
        in_specs=[row, _resident((1, D_MODEL)), _resident((D_MODEL, D_MODEL)), _resident((D_MODEL, D_MODEL))],
        out_specs=[row, row],
        out_shape=[out, out],
        compiler_params=_params(),
        name="mem_kv",
    )(mem, g, wk, wv)


def _mix_xattn_kernel(x_ref, o_ref, sr_ref, z_ref, sga_ref, sgb_ref, wua_ref, pm_ref, ps_ref, wub_ref, wmo_ref,
                      xg_ref, wq_ref, k_ref, v_ref, wo_ref, y_ref):
    zs = []
    for g in range(POOL_GROUPS):
        cols = slice(g * POOL_GC, (g + 1) * POOL_GC)
        zs.append((_dot(z_ref[:, cols], pm_ref[g]) * ps_ref[:, cols]).astype(BF16))
    ya = _dot(o_ref[...] * sr_ref[...], wua_ref[...])
    yb = _dot(jnp.concatenate(zs, axis=1), wub_ref[...])
    merged = (sga_ref[...].astype(F32) * ya + sgb_ref[...].astype(F32) * yb).astype(BF16)
    x = x_ref[...] + _dot(merged, wmo_ref[...])

    h = _rms(x, xg_ref[...]).astype(BF16)
    q = (_dot(h, wq_ref[...]) * (XA_HD ** -0.5)).astype(BF16)

    def scores(hd):
        cols = slice(hd * XA_HD, (hd + 1) * XA_HD)
        return lax.dot_general(q[:, cols], k_ref[:, cols], (((1,), (1,)), ((), ())), preferred_element_type=F32)

    outs = []
    s = scores(0)
    for hd in range(XA_HEADS):
        s_next = scores(hd + 1) if hd + 1 < XA_HEADS else None
        e = jnp.exp(s - jnp.max(s, axis=-1, keepdims=True))
        p = (e / jnp.sum(e, axis=-1, keepdims=True)).astype(BF16)
        outs.append(_dot(p, v_ref[:, hd * XA_HD:(hd + 1) * XA_HD]).astype(BF16))
        s = s_next
    y_ref[...] = x + _dot(jnp.concatenate(outs, axis=1), wo_ref[...])


def _mix_xattn(x, o, sr, z, sga, sgb, wua, pm, ps, wub, wmo, xg, wq, km, vm, wo, seq, mem_len):
    t = x.shape[0]
    tiles_per_seq = seq // ROW_TILE

    def row(n):
        return pl.BlockSpec((ROW_TILE, n), lambda i: (i, 0))

    mem = pl.BlockSpec((mem_len, D_MODEL), lambda i: (i // tiles_per_seq, 0))
    return pl.pallas_call(
        _mix_xattn_kernel,
        grid=(t // ROW_TILE,),
        in_specs=[row(D_MODEL), row(GLA_VW), row(GLA_VW), row(POOL_WIDTH), row(D_MODEL), row(D_MODEL),
                  _resident((GLA_VW, D_MODEL)), _resident((POOL_GROUPS, POOL_GC, POOL_GC)),
                  _resident((1, POOL_WIDTH)), _resident((POOL_WIDTH, D_MODEL)), _resident((D_MODEL, D_MODEL)),
                  _resident((1, D_MODEL)), _resident((D_MODEL, D_MODEL)), mem, mem, _resident((D_MODEL, D_MODEL))],
        out_specs=row(D_MODEL),
        out_shape=jax.ShapeDtypeStruct((t, D_MODEL), F32),
        compiler_params=_params(),
        name="mix_xattn",
    )(x, o, sr, z, sga, sgb, wua, pm, ps, wub, wmo, xg, wq, km, vm, wo)


def _layer(x, mem, ffn1_norm, ffn1_w1, ffn1_w3, ffn1_w2, mix_norm, w_in, w_alpha, b_alpha, gla_head_norm,
           w_up_a, pool_mix, pool_scale, w_up_b, w_mix_out, xa_norm, mem_norm, xa_wq, xa_wk, xa_wv, xa_wo,
           ffn2_norm, ffn2_w1, ffn2_w3, ffn2_w2, final_norm, last):
    batch, seq, _ = x.shape
    mem_len = mem.shape[1]
    assert seq % ROW_TILE == 0 and ROW_TILE % GLA_CHUNK == 0 and ROW_TILE % GLA_WIDE == 0

    def vec(a):
        return a.reshape(1, -1).astype(F32)

    def bf(a):
        return a.astype(BF16)

    xt = x.reshape(batch * seq, D_MODEL)
    x1 = _ffn(xt, vec(ffn1_norm), bf(ffn1_w1), bf(ffn1_w3), bf(ffn1_w2), vec(final_norm), False)

    sizes = (GLA_QK, GLA_QK, GLA_VW, GLA_VW, GLA_RANK, POOL_WIDTH, D_MODEL, D_MODEL)
    offs = [0]
    for s in sizes:
        offs.append(offs[-1] + s)
    wq, wk, wv, wr, wa, wu, wga, wgb = (bf(w_in[:, offs[i]:offs[i + 1]]) for i in range(len(sizes)))
    wa = jnp.pad(wa, ((0, 0), (0, LANE - GLA_RANK)))
    wal = jnp.pad(bf(w_alpha), ((0, LANE - GLA_RANK), (0, 0)))
    sr, z, sga, sgb, o = _mix_gla(x1, vec(mix_norm), wq, wk, wv, wr, wa, wu, wga, wgb, wal, vec(b_alpha),
                                  vec(gla_head_norm), seq)
    km, vm = _mem_kv(mem.reshape(batch * mem_len, D_MODEL), vec(mem_norm), bf(xa_wk), bf(xa_wv), mem_len)
    x3 = _mix_xattn(x1, o, sr, z, sga, sgb, bf(w_up_a), bf(pool_mix), vec(pool_scale), bf(w_up_b), bf(w_mix_out),
                    vec(xa_norm), bf(xa_wq), km, vm, bf(xa_wo), seq, mem_len)

    x4 = _ffn(x3, vec(ffn2_norm), bf(ffn2_w1), bf(ffn2_w3), bf(ffn2_w2), vec(final_norm), last)
    return x4.reshape(batch, seq, D_MODEL)


def kernel(x, mem, ffn1_norm, ffn1_w1, ffn1_w3, ffn1_w2, mix_norm, w_in, w_alpha, b_alpha, gla_head_norm, w_up_a,
           pool_mix, pool_scale, w_up_b, w_mix_out, xa_norm, mem_norm, xa_wq, xa_wk, xa_wv, xa_wo, ffn2_norm,
           ffn2_w1, ffn2_w3, ffn2_w2, final_norm):
    depth = ffn1_norm.shape[0]
    for l in range(depth):
        last = l == depth - 1
        x = _layer(x, mem, ffn1_norm[l], ffn1_w1[l], ffn1_w3[l], ffn1_w2[l], mix_norm[l], w_in[l], w_alpha[l],
                   b_alpha[l], gla_head_norm[l], w_up_a[l], pool_mix[l], pool_scale[l], w_up_b[l], w_mix_out[l],
                   xa_norm[l], mem_norm[l], xa_wq[l], xa_wk[l], xa_wv[l], xa_wo[l], ffn2_norm[l], ffn2_w1[l],
                   ffn2_w3[l], ffn2_w2[l], final_norm, last)
    return x
```

```python
import functools
from typing import Callable, NamedTuple

import jax
import jax.numpy as jnp
from jax import lax
from jax.experimental import pallas as pl
from jax.experimental.pallas import tpu as pltpu

F32 = jnp.float32
BF16 = jnp.bfloat16

D_MODEL = 1024
D_FF = 2816
EPS = 1e-6
GLA_HEADS = 4
GLA_DK = 128
GLA_DV = 256
GLA_QK = GLA_HEADS * GLA_DK
GLA_VW = GLA_HEADS * GLA_DV
GLA_RANK = 16
GLA_GATE_TEMP = 16.0
GLA_CHUNK = 64
GLA_SUB = 16
GLA_WIDE = 256
GLA_WIDE_MAX_SPAN = 60.0
POOL_GROUPS = 4
POOL_GC = 128
POOL_WIDTH = POOL_GROUPS * POOL_GC
POOL_WINDOWS = (2, 4, 8, 16)
POOL_HALO = 16
XA_HEADS = 4
XA_HD = 256

LANE = 128
BF16_ROWS = 16
W_IN_SIZES = (GLA_QK, GLA_QK, GLA_VW, GLA_VW, GLA_RANK, POOL_WIDTH, D_MODEL, D_MODEL)
ROW_TILE = 512
FFN_ROWS = 1024
FF_SPLITS = (0, 1024, 2048, D_FF)
VMEM_LIMIT = 56 * 1024 * 1024


def _rms(x, g):
    return x * lax.rsqrt(jnp.mean(x * x, axis=-1, keepdims=True) + EPS) * g


def _sigmoid(x):
    return 0.5 * jnp.tanh(0.5 * x) + 0.5


def _dot(a, b):
    return jnp.dot(a, b, preferred_element_type=F32)


def _resident(shape):
    nd = len(shape)
    return pl.BlockSpec(shape, lambda *_: (0,) * nd, pipeline_mode=pl.Buffered(1))


def _params():
    return pltpu.CompilerParams(dimension_semantics=("arbitrary",), vmem_limit_bytes=VMEM_LIMIT)


class _CastJob(NamedTuple):
    inputs: tuple
    in_specs: tuple
    out_specs: tuple
    out_shapes: tuple
    body: Callable


def _cast_rows_job(w, n_steps):
    rows, cols = w.shape
    per_step = -(-rows // n_steps)
    block = next(b for b in range(BF16_ROWS, rows + 1, BF16_ROWS) if rows % b == 0 and b >= per_step)
    n_blocks = rows // block
    spec = pl.BlockSpec((block, cols), lambda i: ((i * n_blocks) // n_steps, 0))

    def body(ins, outs):
        outs[0][...] = ins[0][...].astype(BF16)

    return _CastJob((w,), (spec,), (spec,), (jax.ShapeDtypeStruct((rows, cols), BF16),), body)


def _w_in_job(w_in, n_steps):
    rows = w_in.shape[0]
    block = rows // n_steps
    assert rows % n_steps == 0 and block % BF16_ROWS == 0
    offs = [0]
    for n in W_IN_SIZES:
        offs.append(offs[-1] + n)
    widths = tuple(LANE if n == GLA_RANK else n for n in W_IN_SIZES)

    def spec(n):
        return pl.BlockSpec((block, n), lambda i: (i, 0))

    def body(ins, outs):
        for o_ref, n, lo in zip(outs, W_IN_SIZES, offs):
            if n == GLA_RANK:
                lane = lax.broadcasted_iota(jnp.int32, (block, LANE), 1)
                o_ref[...] = jnp.where(lane < GLA_RANK, ins[0][:, lo:lo + LANE], 0.0).astype(BF16)
            else:
                o_ref[...] = ins[0][:, lo:lo + n].astype(BF16)

    return _CastJob((w_in,), (spec(w_in.shape[1]),), tuple(spec(n) for n in widths),
                    tuple(jax.ShapeDtypeStruct((rows, n), BF16) for n in widths), body)


def _call(kernel_fn, args, in_specs, out_specs, out_shape, jobs=(), *, grid, scratch_shapes=(), name):
    n_in, n_out = len(args), len(out_specs)
    job_args = [a for j in jobs for a in j.inputs]
    job_outs = [o for j in jobs for o in j.out_shapes]

    def body(*refs):
        j_in = refs[n_in:n_in + len(job_args)]
        first_out = n_in + len(job_args)
        j_out = refs[first_out + n_out:first_out + n_out + len(job_outs)]
        kernel_fn(*refs[:n_in], *refs[first_out:first_out + n_out], *refs[first_out + n_out + len(job_outs):])
        for j in jobs:
            j.body(j_in[:len(j.inputs)], j_out[:len(j.out_shapes)])
            j_in, j_out = j_in[len(j.inputs):], j_out[len(j.out_shapes):]

    outs = pl.pallas_call(
        body,
        grid=grid,
        in_specs=[*in_specs, *(sp for j in jobs for sp in j.in_specs)],
        out_specs=[*out_specs, *(sp for j in jobs for sp in j.out_specs)],
        out_shape=[*out_shape, *job_outs],
        scratch_shapes=list(scratch_shapes),
        compiler_params=_params(),
        name=name,
    )(*args, *job_args)
    return outs[:n_out], outs[n_out:]


def _ffn_kernel(x_ref, g_ref, w1_ref, w3_ref, w2_ref, fg_ref, o_ref, *, final_norm):
    x = x_ref[...]
    h = _rms(x, g_ref[...]).astype(BF16)
    acc = jnp.zeros_like(x)
    for lo, hi in zip(FF_SPLITS[:-1], FF_SPLITS[1:]):
        a = _dot(h, w1_ref[:, lo:hi])
        b = _dot(h, w3_ref[:, lo:hi])
        act = (a * jax.nn.sigmoid(a) * b).astype(BF16)
        acc = acc + _dot(act, w2_ref[lo:hi, :])
    y = x + 0.5 * acc
    if final_norm:
        y = _rms(y, fg_ref[...])
    o_ref[...] = y


def _ffn(x, g, w1, w3, w2, fg, final_norm, make_jobs=None):
    t = x.shape[0]
    steps = t // FFN_ROWS
    row = pl.BlockSpec((FFN_ROWS, D_MODEL), lambda i: (i, 0))
    (y,), cast = _call(
        functools.partial(_ffn_kernel, final_norm=final_norm),
        (x, g, w1, w3, w2, fg),
        [row, _resident((1, D_MODEL)), _resident((D_MODEL, D_FF)), _resident((D_MODEL, D_FF)),
         _resident((D_FF, D_MODEL)), _resident((1, D_MODEL))],
        [row], [jax.ShapeDtypeStruct((t, D_MODEL), F32)],
        make_jobs(steps) if make_jobs else (),
        grid=(steps,), name="ffn_final" if final_norm else "ffn")
    return y, cast


def _cumsum_rows(mask_bf16, la):
    hi = la.astype(BF16)
    lo = (la - hi.astype(F32)).astype(BF16)
    return _dot(mask_bf16, hi) + _dot(mask_bf16, lo)


def _decay_column(b_last_row):
    return jnp.transpose(jnp.broadcast_to(jnp.exp(b_last_row), (8, GLA_DK)))[:, 0:1]


def _gla_wide_task(q_ref, k_ref, v_ref, la_ref, gn, o_ref, s_ref):
    c = GLA_WIDE
    ri = lax.broadcasted_iota(jnp.int32, (c, c), 0)
    ci = lax.broadcasted_iota(jnp.int32, (c, c), 1)
    causal = ci <= ri
    cum_incl = causal.astype(BF16)
    row_sl = [slice(ic * c, (ic + 1) * c) for ic in range(ROW_TILE // c)]
    b_all = [_cumsum_rows(cum_incl, la_ref[rows, :]) for rows in row_sl]
    units = [(ic, hd) for ic in range(len(row_sl)) for hd in range(GLA_HEADS)]

    def factors(ic, hd):
        rows, kc = row_sl[ic], slice(hd * GLA_DK, (hd + 1) * GLA_DK)
        b = b_all[ic][:, kc]
        qg = (q_ref[rows, kc].astype(F32) * jnp.exp(b)).astype(BF16)
        kg = (k_ref[rows, kc].astype(F32) * jnp.exp(-b)).astype(BF16)
        s = lax.dot_general(qg, kg, (((1,), (1,)), ((), ())), preferred_element_type=F32)
        return qg, kg, s, b[c - 1:c, :]

    nxt = factors(*units[0])
    yield
    for n, (ic, hd) in enumerate(units):
        qg, kg, s, b_last = nxt
        if n + 1 < len(units):
            nxt = factors(*units[n + 1])
        rows, vc = row_sl[ic], slice(hd * GLA_DV, (hd + 1) * GLA_DV)
        v = v_ref[rows, vc]
        s_old = s_ref[hd]
        inter = _dot(qg, s_old.astype(BF16))
        upd = lax.dot_general(kg, v, (((0,), (0,)), ((), ())), preferred_element_type=F32)
        p = jnp.where(causal, s, 0.0).astype(BF16)
        o = _dot(p, v) + inter
        s_ref[hd] = (s_old + upd) * _decay_column(b_last)
        o_ref[rows, vc] = _rms(o, gn).astype(BF16)
        yield


def _gla_anchored_chunks(q_ref, k_ref, v_ref, la_ref, gn, o_ref, s_ref):
    c = GLA_CHUNK
    ri = lax.broadcasted_iota(jnp.int32, (c, c), 0)
    ci = lax.broadcasted_iota(jnp.int32, (c, c), 1)
    causal = ci <= ri
    cum_incl = causal.astype(BF16)
    cum_anchor = (ci < (ri // GLA_SUB) * GLA_SUB).astype(BF16)
    krow = lax.broadcasted_iota(jnp.int32, (c, 1), 0)

    def chunk(ic, carry):
        r0 = pl.multiple_of(ic * c, c)
        la = la_ref[pl.ds(r0, c), :]
        b_all = _cumsum_rows(cum_incl, la)
        beta_all = _cumsum_rows(cum_anchor, la)
        for hd in range(GLA_HEADS):
            kc = slice(hd * GLA_DK, (hd + 1) * GLA_DK)
            vc = slice(hd * GLA_DV, (hd + 1) * GLA_DV)
            q = q_ref[pl.ds(r0, c), kc].astype(F32)
            k = k_ref[pl.ds(r0, c), kc].astype(F32)
            v = v_ref[pl.ds(r0, c), vc]
            b = b_all[:, kc]
            beta = beta_all[:, kc]
            qa = (q * jnp.exp(b - beta)).astype(BF16)
            rows = []
            for blk in range(c // GLA_SUB):
                r_lo, r_hi = blk * GLA_SUB, (blk + 1) * GLA_SUB
                e = jnp.where(krow < r_hi, beta[r_lo:r_lo + 1, :] - b, 0.0)
                ka = (k * jnp.exp(e)).astype(BF16)
                rows.append(lax.dot_general(qa[r_lo:r_hi, :], ka, (((1,), (1,)), ((), ())),
                                            preferred_element_type=F32))
            p = jnp.where(causal, jnp.concatenate(rows, axis=0), 0.0).astype(BF16)
            s_old = s_ref[hd]
            o = _dot(p, v) + _dot((q * jnp.exp(b)).astype(BF16), s_old.astype(BF16))
            b_last = b[c - 1:c, :]
            k_dec = (k * jnp.exp(b_last - b)).astype(BF16)
            upd = lax.dot_general(k_dec, v, (((0,), (0,)), ((), ())), preferred_element_type=F32)
            s_ref[hd] = s_old * _decay_column(b_last) + upd
            o_ref[pl.ds(r0, c), vc] = _rms(o, gn).astype(BF16)
        return carry

    lax.fori_loop(0, ROW_TILE // c, chunk, 0)


def _interleave(tasks):
    tasks = list(tasks)
    while tasks:
        for t in list(tasks):
            try:
                next(t)
            except StopIteration:
                tasks.remove(t)


def _mix_in_task(x_ref, g_ref, wq_ref, wk_ref, wv_ref, wr_ref, wa_ref, wu_ref, wga_ref, wgb_ref, wal_ref, bal_ref,
                 sr_ref, z_ref, sga_ref, sgb_ref, ubuf, tile_in_seq, keep_halo, gla_inputs):
    h = _rms(x_ref[...], g_ref[...]).astype(BF16)
    a_code = _dot(h, wa_ref[...]).astype(BF16)
    yield
    u = _dot(h, wu_ref[...])
    yield

    ubuf[POOL_HALO:POOL_HALO + ROW_TILE, :] = u
    pos = lax.broadcasted_iota(jnp.int32, (ROW_TILE, 1), 0) + tile_in_seq * ROW_TILE
    for g, w in enumerate(POOL_WINDOWS):
        cols = slice(g * POOL_GC, (g + 1) * POOL_GC)
        s = ubuf[:, cols]
        sh = 1
        while sh < w:
            s = s + pltpu.roll(s, sh, 0)
            sh *= 2
        cnt = jnp.minimum(pos + 1, w).astype(F32)
        z_ref[:, cols] = (s[POOL_HALO:, :] / cnt - u[:, cols]).astype(BF16)
    ubuf[0:POOL_HALO, :] = jnp.where(keep_halo, ubuf[0:POOL_HALO, :], ubuf[ROW_TILE:ROW_TILE + POOL_HALO, :])

    r = _dot(h, wr_ref[...])
    sr_ref[...] = (r * _sigmoid(r)).astype(BF16)
    yield
    zg = _dot(a_code, wal_ref[...]) + bal_ref[...]
    yield
    la = (jnp.minimum(zg, 0.0) - jnp.log(1.0 + jnp.exp(-jnp.abs(zg)))) * (1.0 / GLA_GATE_TEMP)
    span = None
    for r0 in range(0, ROW_TILE, GLA_WIDE):
        sp = jnp.max(-jnp.sum(la[r0:r0 + GLA_WIDE, :], axis=0, keepdims=True))
        span = sp if span is None else jnp.maximum(span, sp)
    sga_ref[...] = _sigmoid(_dot(h, wga_ref[...])).astype(BF16)
    yield
    sgb_ref[...] = _sigmoid(_dot(h, wgb_ref[...])).astype(BF16)
    yield

    q = (_dot(h, wq_ref[...]) * (GLA_DK ** -0.5)).astype(BF16)
    yield
    v = _dot(h, wv_ref[...]).astype(BF16)
    yield
    k = _dot(h, wk_ref[...]).astype(BF16)
    gla_inputs.extend((q, k, v, la, span))


def _mix_gla_kernel(x_ref, g_ref, wq_ref, wk_ref, wv_ref, wr_ref, wa_ref, wu_ref, wga_ref, wgb_ref,
                    wal_ref, bal_ref, gn_ref,
                    sr_ref, z_ref, sga_ref, sgb_ref, o_ref,
                    q_s, k_s, v_s, la_s, s_ref, ubuf, span_s, *, tiles_per_seq, n_tiles):
    step = pl.program_id(0)
    mix_tile = jnp.minimum(step, n_tiles - 1)
    gla_tile = step - 1

    @pl.when(step == 0)
    def _():
        q_s[...] = jnp.zeros_like(q_s)
        k_s[...] = jnp.zeros_like(k_s)
        v_s[...] = jnp.zeros_like(v_s)
        la_s[...] = jnp.zeros_like(la_s)
        span_s[0] = 0.0

    @pl.when(mix_tile % tiles_per_seq == 0)
    def _():
        ubuf[0:POOL_HALO, :] = jnp.zeros((POOL_HALO, POOL_WIDTH), F32)

    @pl.when(jnp.logical_or(step == 0, gla_tile % tiles_per_seq == 0))
    def _():
        s_ref[...] = jnp.zeros_like(s_ref)

    keep_halo = step >= n_tiles - 1
    mix_args = (x_ref, g_ref, wq_ref, wk_ref, wv_ref, wr_ref, wa_ref, wu_ref, wga_ref, wgb_ref, wal_ref, bal_ref,
                sr_ref, z_ref, sga_ref, sgb_ref, ubuf, mix_tile % tiles_per_seq, keep_halo)
    gla_args = (q_s, k_s, v_s, la_s, gn_ref[...], o_ref, s_ref)

    def run(gla_tasks):
        nxt = []
        _interleave(gla_tasks + [_mix_in_task(*mix_args, nxt)])
        q_s[...], k_s[...], v_s[...], la_s[...], span_s[0] = nxt

    wide_ok = span_s[0] < GLA_WIDE_MAX_SPAN

    @pl.when(wide_ok)
    def _():
        run([_gla_wide_task(*gla_args)])

    @pl.when(jnp.logical_not(wide_ok))
    def _():
        _gla_anchored_chunks(*gla_args)
        run([])


def _mix_gla(x, g, wq, wk, wv, wr, wa, wu, wga, wgb, wal, bal, gn, seq, make_jobs):
    t = x.shape[0]
    n_tiles = t // ROW_TILE

    def cur(n):
        return pl.BlockSpec((ROW_TILE, n), lambda i: (jnp.minimum(i, n_tiles - 1), 0))

    prev = pl.BlockSpec((ROW_TILE, GLA_VW), lambda i: (jnp.maximum(i - 1, 0), 0))

    def out(n):
        return jax.ShapeDtypeStruct((t, n), BF16)

    return _call(
        functools.partial(_mix_gla_kernel, tiles_per_seq=seq // ROW_TILE, n_tiles=n_tiles),
        (x, g, wq, wk, wv, wr, wa, wu, wga, wgb, wal, bal, gn),
        [cur(D_MODEL), _resident((1, D_MODEL)),
         _resident((D_MODEL, GLA_QK)), _resident((D_MODEL, GLA_QK)), _resident((D_MODEL, GLA_VW)),
         _resident((D_MODEL, GLA_VW)), _resident((D_MODEL, LANE)), _resident((D_MODEL, POOL_WIDTH)),
         _resident((D_MODEL, D_MODEL)), _resident((D_MODEL, D_MODEL)),
         _resident((LANE, GLA_QK)), _resident((1, GLA_QK)), _resident((1, GLA_DV))],
        [cur(GLA_VW), cur(POOL_WIDTH), cur(D_MODEL), cur(D_MODEL), prev],
        [out(GLA_VW), out(POOL_WIDTH), out(D_MODEL), out(D_MODEL), out(GLA_VW)],
        make_jobs(n_tiles + 1),
        grid=(n_tiles + 1,),
        scratch_shapes=[pltpu.VMEM((ROW_TILE, GLA_QK), BF16), pltpu.VMEM((ROW_TILE, GLA_QK), BF16),
                        pltpu.VMEM((ROW_TILE, GLA_VW), BF16), pltpu.VMEM((ROW_TILE, GLA_QK), F32),
                        pltpu.VMEM((GLA_HEADS, GLA_DK, GLA_DV), F32),
                        pltpu.VMEM((POOL_HALO + ROW_TILE, POOL_WIDTH), F32),
                        pltpu.SMEM((1,), F32)],
        name="mix_gla")


def _mem_kv_kernel(m_ref, g_ref, wk_ref, wv_ref, k_ref, v_ref):
    m = _rms(m_ref[...], g_ref[...]).astype(BF16)
    k_ref[...] = _dot(m, wk_ref[...]).astype(BF16)
    v_ref[...] = _dot(m, wv_ref[...]).astype(BF16)


def _mem_kv(mem, g, wk, wv, mem_len, make_jobs):
    t = mem.shape[0]
    steps = t // mem_len
    row = pl.BlockSpec((mem_len, D_MODEL), lambda i: (i, 0))
    out = jax.ShapeDtypeStruct((t, D_MODEL), BF16)
    return _call(
        _mem_kv_kernel, (mem, g, wk, wv),
        [row, _resident((1, D_MODEL)), _resident((D_MODEL, D_MODEL)), _resident((D_MODEL, D_MODEL))],
        [row, row], [out, out], make_jobs(steps), grid=(steps,), name="mem_kv")


def _mix_xattn_kernel(x_ref, o_ref, sr_ref, z_ref, sga_ref, sgb_ref, wua_ref, pm_ref, ps_ref, wub_ref, wmo_ref,
                      xg_ref, wq_ref, k_ref, v_ref, wo_ref, y_ref):
    zs = []
    for g in range(POOL_GROUPS):
        cols = slice(g * POOL_GC, (g + 1) * POOL_GC)
        zs.append((_dot(z_ref[:, cols], pm_ref[g]) * ps_ref[:, cols]).astype(BF16))
    ya = _dot(o_ref[...] * sr_ref[...], wua_ref[...])
    yb = _dot(jnp.concatenate(zs, axis=1), wub_ref[...])
    merged = (sga_ref[...].astype(F32) * ya + sgb_ref[...].astype(F32) * yb).astype(BF16)
    x = x_ref[...] + _dot(merged, wmo_ref[...])

    h = _rms(x, xg_ref[...]).astype(BF16)
    q = (_dot(h, wq_ref[...]) * (XA_HD ** -0.5)).astype(BF16)

    def scores(hd):
        cols = slice(hd * XA_HD, (hd + 1) * XA_HD)
        return lax.dot_general(q[:, cols], k_ref[:, cols], (((1,), (1,)), ((), ())), preferred_element_type=F32)

    outs = []
    s = scores(0)
    for hd in range(XA_HEADS):
        s_next = scores(hd + 1) if hd + 1 < XA_HEADS else None
        e = jnp.exp(s - jnp.max(s, axis=-1, keepdims=True))
        p = (e / jnp.sum(e, axis=-1, keepdims=True)).astype(BF16)
        outs.append(_dot(p, v_ref[:, hd * XA_HD:(hd + 1) * XA_HD]).astype(BF16))
        s = s_next
    y_ref[...] = x + _dot(jnp.concatenate(outs, axis=1), wo_ref[...])


def _mix_xattn(x, o, sr, z, sga, sgb, wua, pm, ps, wub, wmo, xg, wq, km, vm, wo, seq, mem_len, make_jobs):
    t = x.shape[0]
    steps = t // ROW_TILE
    tiles_per_seq = seq // ROW_TILE

    def row(n):
        return pl.BlockSpec((ROW_TILE, n), lambda i: (i, 0))

    mem = pl.BlockSpec((mem_len, D_MODEL), lambda i: (i // tiles_per_seq, 0))
    (y,), cast = _call(
        _mix_xattn_kernel, (x, o, sr, z, sga, sgb, wua, pm, ps, wub, wmo, xg, wq, km, vm, wo),
        [row(D_MODEL), row(GLA_VW), row(GLA_VW), row(POOL_WIDTH), row(D_MODEL), row(D_MODEL),
         _resident((GLA_VW, D_MODEL)), _resident((POOL_GROUPS, POOL_GC, POOL_GC)),
         _resident((1, POOL_WIDTH)), _resident((POOL_WIDTH, D_MODEL)), _resident((D_MODEL, D_MODEL)),
         _resident((1, D_MODEL)), _resident((D_MODEL, D_MODEL)), mem, mem, _resident((D_MODEL, D_MODEL))],
        [row(D_MODEL)], [jax.ShapeDtypeStruct((t, D_MODEL), F32)],
        make_jobs(steps), grid=(steps,), name="mix_xattn")
    return y, cast


def _layer(x, mem, ffn1_norm, ffn1_w1, ffn1_w3, ffn1_w2, mix_norm, w_in, w_alpha, b_alpha, gla_head_norm,
           w_up_a, pool_mix, pool_scale, w_up_b, w_mix_out, xa_norm, mem_norm, xa_wq, xa_wk, xa_wv, xa_wo,
           ffn2_norm, ffn2_w1, ffn2_w3, ffn2_w2, final_norm, last):
    batch, seq, _ = x.shape
    mem_len = mem.shape[1]
    assert seq % ROW_TILE == 0 and ROW_TILE % GLA_CHUNK == 0 and ROW_TILE % GLA_WIDE == 0

    def vec(a):
        return a.reshape(1, -1).astype(F32)

    def bf(a):
        return a.astype(BF16)

    def casts(*ws):
        return lambda n_steps: tuple(_cast_rows_job(w, n_steps) for w in ws)

    (km, vm), ffn1_w = _mem_kv(mem.reshape(batch * mem_len, D_MODEL), vec(mem_norm), bf(xa_wk), bf(xa_wv), mem_len,
                               casts(ffn1_w1, ffn1_w3, ffn1_w2))
    x1, w_in_cols = _ffn(x.reshape(batch * seq, D_MODEL), vec(ffn1_norm), *ffn1_w, vec(final_norm), False,
                         lambda n_steps: (_w_in_job(w_in, n_steps),))
    wal = jnp.pad(bf(w_alpha), ((0, LANE - GLA_RANK), (0, 0)))
    wq, wk, wv, wr, wa, wu, wga, wgb = w_in_cols
    (sr, z, sga, sgb, o), mix_w = _mix_gla(x1, vec(mix_norm), wq, wk, wv, wr, wa, wu, wga, wgb, wal, vec(b_alpha),
                                           vec(gla_head_norm), seq,
                                           casts(w_up_a, w_up_b, w_mix_out, xa_wq, xa_wo))
    wua, wub, wmo, xwq, xwo = mix_w
    x3, ffn2_w = _mix_xattn(x1, o, sr, z, sga, sgb, wua, bf(pool_mix), vec(pool_scale), wub, wmo,
                            vec(xa_norm), xwq, km, vm, xwo, seq, mem_len, casts(ffn2_w1, ffn2_w3, ffn2_w2))
    x4, _ = _ffn(x3, vec(ffn2_norm), *ffn2_w, vec(final_norm), last)
    return x4.reshape(batch, seq, D_MODEL)


def kernel(x, mem, ffn1_norm, ffn1_w1, ffn1_w3, ffn1_w2, mix_norm, w_in, w_alpha, b_alpha, gla_head_norm, w_up_a,
           pool_mix, pool_scale, w_up_b, w_mix_out, xa_norm, mem_norm, xa_wq, xa_wk, xa_wv, xa_wo, ffn2_norm,
           ffn2_w1, ffn2_w3, ffn2_w2, final_norm):
    depth = ffn1_norm.shape[0]
    for l in range(depth):
        last = l == depth - 1
        x = _layer(x, mem, ffn1_norm[l], ffn1_w1[l], ffn1_w3[l], ffn1_w2[l], mix_norm[l], w_in[l], w_alpha[l],
                   b_alpha[l], gla_head_norm[l], w_up_a[l], pool_mix[l], pool_scale[l], w_up_b[l], w_mix_out[l],
                   xa_norm[l], mem_norm[l], xa_wq[l], xa_wk[l], xa_wv[l], xa_wo[l], ffn2_norm[l], ffn2_w1[l],
                   ffn2_w3[l], ffn2_w2[l], final_norm, last)
    return x
```

```python
import functools
from typing import Callable, NamedTuple

import jax
import jax.numpy as jnp
from jax import lax
from jax.experimental import pallas as pl
from jax.experimental.pallas import tpu as pltpu

F32 = jnp.float32
BF16 = jnp.bfloat16

D_MODEL = 1024
D_FF = 2816
EPS = 1e-6
GLA_HEADS = 4
GLA_DK = 128
GLA_DV = 256
GLA_QK = GLA_HEADS * GLA_DK
GLA_VW = GLA_HEADS * GLA_DV
GLA_RANK = 16
GLA_GATE_TEMP = 16.0
GLA_CHUNK = 64
GLA_SUB = 16
GLA_WIDE = 256
GLA_WIDE_MAX_SPAN = 60.0
EXP_TO_ZERO = -1e30
POOL_GROUPS = 4
POOL_GC = 128
POOL_WIDTH = POOL_GROUPS * POOL_GC
POOL_WINDOWS = (2, 4, 8, 16)
POOL_HALO = 16
XA_HEADS = 4
XA_HD = 256

LANE = 128
BF16_ROWS = 16
W_IN_SIZES = (GLA_QK, GLA_QK, GLA_VW, GLA_VW, GLA_RANK, POOL_WIDTH, D_MODEL, D_MODEL)
ROW_TILE = 512
FFN_ROWS = 1024
FF_SPLITS = (0, 1024, 2048, D_FF)
VMEM_LIMIT = 56 * 1024 * 1024


def _rms(x, g):
    return x * lax.rsqrt(jnp.mean(x * x, axis=-1, keepdims=True) + EPS) * g


def _sigmoid(x):
    return 0.5 * jnp.tanh(0.5 * x) + 0.5


def _dot(a, b):
    return jnp.dot(a, b, preferred_element_type=F32)


def _resident(shape):
    nd = len(shape)
    return pl.BlockSpec(shape, lambda *_: (0,) * nd, pipeline_mode=pl.Buffered(1))


def _params():
    return pltpu.CompilerParams(dimension_semantics=("arbitrary",), vmem_limit_bytes=VMEM_LIMIT)


class _CastJob(NamedTuple):
    inputs: tuple
    in_specs: tuple
    out_specs: tuple
    out_shapes: tuple
    body: Callable


def _cast_rows_job(w, n_steps):
    rows, cols = w.shape
    per_step = -(-rows // n_steps)
    block = next(b for b in range(BF16_ROWS, rows + 1, BF16_ROWS) if rows % b == 0 and b >= per_step)
    n_blocks = rows // block
    spec = pl.BlockSpec((block, cols), lambda i: ((i * n_blocks) // n_steps, 0))

    def body(ins, outs):
        outs[0][...] = ins[0][...].astype(BF16)

    return _CastJob((w,), (spec,), (spec,), (jax.ShapeDtypeStruct((rows, cols), BF16),), body)


def _w_in_job(w_in, n_steps):
    rows = w_in.shape[0]
    block = rows // n_steps
    assert rows % n_steps == 0 and block % BF16_ROWS == 0
    offs = [0]
    for n in W_IN_SIZES:
        offs.append(offs[-1] + n)
    widths = tuple(LANE if n == GLA_RANK else n for n in W_IN_SIZES)

    def spec(n):
        return pl.BlockSpec((block, n), lambda i: (i, 0))

    def body(ins, outs):
        for o_ref, n, lo in zip(outs, W_IN_SIZES, offs):
            if n == GLA_RANK:
                lane = lax.broadcasted_iota(jnp.int32, (block, LANE), 1)
                o_ref[...] = jnp.where(lane < GLA_RANK, ins[0][:, lo:lo + LANE], 0.0).astype(BF16)
            else:
                o_ref[...] = ins[0][:, lo:lo + n].astype(BF16)

    return _CastJob((w_in,), (spec(w_in.shape[1]),), tuple(spec(n) for n in widths),
                    tuple(jax.ShapeDtypeStruct((rows, n), BF16) for n in widths), body)


def _call(kernel_fn, args, in_specs, out_specs, out_shape, jobs=(), *, grid, scratch_shapes=(), name):
    n_in, n_out = len(args), len(out_specs)
    job_args = [a for j in jobs for a in j.inputs]
    job_outs = [o for j in jobs for o in j.out_shapes]

    def body(*refs):
        j_in = refs[n_in:n_in + len(job_args)]
        first_out = n_in + len(job_args)
        j_out = refs[first_out + n_out:first_out + n_out + len(job_outs)]
        kernel_fn(*refs[:n_in], *refs[first_out:first_out + n_out], *refs[first_out + n_out + len(job_outs):])
        for j in jobs:
            j.body(j_in[:len(j.inputs)], j_out[:len(j.out_shapes)])
            j_in, j_out = j_in[len(j.inputs):], j_out[len(j.out_shapes):]

    outs = pl.pallas_call(
        body,
        grid=grid,
        in_specs=[*in_specs, *(sp for j in jobs for sp in j.in_specs)],
        out_specs=[*out_specs, *(sp for j in jobs for sp in j.out_specs)],
        out_shape=[*out_shape, *job_outs],
        scratch_shapes=list(scratch_shapes),
        compiler_params=_params(),
        name=name,
    )(*args, *job_args)
    return outs[:n_out], outs[n_out:]


def _ffn_kernel(x_ref, g_ref, w1_ref, w3_ref, w2_ref, fg_ref, o_ref, *, final_norm):
    x = x_ref[...]
    h = _rms(x, g_ref[...]).astype(BF16)
    acc = jnp.zeros_like(x)
    for lo, hi in zip(FF_SPLITS[:-1], FF_SPLITS[1:]):
        a = _dot(h, w1_ref[:, lo:hi])
        b = _dot(h, w3_ref[:, lo:hi])
        act = (a * jax.nn.sigmoid(a) * b).astype(BF16)
        acc = acc + _dot(act, w2_ref[lo:hi, :])
    y = x + 0.5 * acc
    if final_norm:
        y = _rms(y, fg_ref[...])
    o_ref[...] = y


def _ffn(x, g, w1, w3, w2, fg, final_norm, make_jobs=None):
    t = x.shape[0]
    steps = t // FFN_ROWS
    row = pl.BlockSpec((FFN_ROWS, D_MODEL), lambda i: (i, 0))
    (y,), cast = _call(
        functools.partial(_ffn_kernel, final_norm=final_norm),
        (x, g, w1, w3, w2, fg),
        [row, _resident((1, D_MODEL)), _resident((D_MODEL, D_FF)), _resident((D_MODEL, D_FF)),
         _resident((D_FF, D_MODEL)), _resident((1, D_MODEL))],
        [row], [jax.ShapeDtypeStruct((t, D_MODEL), F32)],
        make_jobs(steps) if make_jobs else (),
        grid=(steps,), name="ffn_final" if final_norm else "ffn")
    return y, cast


def _cumsum_rows(mask_bf16, la):
    hi = la.astype(BF16)
    lo = (la - hi.astype(F32)).astype(BF16)
    return _dot(mask_bf16, hi) + _dot(mask_bf16, lo)


def _decay_column(b_last_row):
    return jnp.transpose(jnp.broadcast_to(jnp.exp(b_last_row), (8, GLA_DK)))[:, 0:1]


def _gla_wide_task(q_ref, k_ref, v_ref, la_ref, gn, o_ref, s_ref):
    c = GLA_WIDE
    ri = lax.broadcasted_iota(jnp.int32, (c, c), 0)
    ci = lax.broadcasted_iota(jnp.int32, (c, c), 1)
    causal = ci <= ri
    cum_incl = causal.astype(BF16)
    row_sl = [slice(ic * c, (ic + 1) * c) for ic in range(ROW_TILE // c)]
    b_all = [_cumsum_rows(cum_incl, la_ref[rows, :]) for rows in row_sl]
    units = [(ic, hd) for ic in range(len(row_sl)) for hd in range(GLA_HEADS)]

    def factors(ic, hd):
        rows, kc = row_sl[ic], slice(hd * GLA_DK, (hd + 1) * GLA_DK)
        b = b_all[ic][:, kc]
        qg = (q_ref[rows, kc].astype(F32) * jnp.exp(b)).astype(BF16)
        kg = (k_ref[rows, kc].astype(F32) * jnp.exp(-b)).astype(BF16)
        s = lax.dot_general(qg, kg, (((1,), (1,)), ((), ())), preferred_element_type=F32)
        return qg, kg, s, b[c - 1:c, :]

    nxt = factors(*units[0])
    yield
    for n, (ic, hd) in enumerate(units):
        qg, kg, s, b_last = nxt
        if n + 1 < len(units):
            nxt = factors(*units[n + 1])
        rows, vc = row_sl[ic], slice(hd * GLA_DV, (hd + 1) * GLA_DV)
        v = v_ref[rows, vc]
        s_old = s_ref[hd]
        inter = _dot(qg, s_old.astype(BF16))
        upd = lax.dot_general(kg, v, (((0,), (0,)), ((), ())), preferred_element_type=F32)
        p = jnp.where(causal, s, 0.0).astype(BF16)
        o = _dot(p, v) + inter
        s_ref[hd] = (s_old + upd) * _decay_column(b_last)
        o_ref[rows, vc] = _rms(o, gn).astype(BF16)
        yield


def _gla_anchored_chunks(q_ref, k_ref, v_ref, la_ref, gn, o_ref, s_ref):
    c = GLA_CHUNK
    ri = lax.broadcasted_iota(jnp.int32, (c, c), 0)
    ci = lax.broadcasted_iota(jnp.int32, (c, c), 1)
    causal = ci <= ri
    cum_incl = causal.astype(BF16)
    cum_anchor = (ci < (ri // GLA_SUB) * GLA_SUB).astype(BF16)
    krow = lax.broadcasted_iota(jnp.int32, (c, 1), 0)
    kcol = lax.broadcasted_iota(jnp.int32, (GLA_SUB, c), 1)

    def chunk(ic, carry):
        r0 = pl.multiple_of(ic * c, c)
        la = la_ref[pl.ds(r0, c), :]
        b_all = _cumsum_rows(cum_incl, la)
        beta_all = _cumsum_rows(cum_anchor, la)
        for hd in range(GLA_HEADS):
            kc = slice(hd * GLA_DK, (hd + 1) * GLA_DK)
            vc = slice(hd * GLA_DV, (hd + 1) * GLA_DV)
            q = q_ref[pl.ds(r0, c), kc].astype(F32)
            k = k_ref[pl.ds(r0, c), kc].astype(F32)
            v = v_ref[pl.ds(r0, c), vc]
            b = b_all[:, kc]
            beta = beta_all[:, kc]
            qa = (q * jnp.exp(b - beta)).astype(BF16)
            rows = []
            for blk in range(c // GLA_SUB):
                r_lo, r_hi = blk * GLA_SUB, (blk + 1) * GLA_SUB
                e = jnp.where(krow < r_lo, beta[r_lo:r_lo + 1, :] - b, EXP_TO_ZERO)
                ka = (k * jnp.exp(e)).astype(BF16)
                sc = lax.dot_general(qa[r_lo:r_hi, :], ka, (((1,), (1,)), ((), ())), preferred_element_type=F32)
                qb, bb = q[r_lo:r_hi, :], b[r_lo:r_hi, :]
                for j in range(r_lo, r_hi):
                    w = jnp.exp(jnp.minimum(bb - b[j:j + 1, :], 0.0))
                    sc = jnp.where(kcol == j, jnp.sum(qb * k[j:j + 1, :] * w, axis=1, keepdims=True), sc)
                rows.append(sc)
            p = jnp.where(causal, jnp.concatenate(rows, axis=0), 0.0).astype(BF16)
            s_old = s_ref[hd]
            o = _dot(p, v) + _dot((q * jnp.exp(b)).astype(BF16), s_old.astype(BF16))
            b_last = b[c - 1:c, :]
            k_dec = (k * jnp.exp(b_last - b)).astype(BF16)
            upd = lax.dot_general(k_dec, v, (((0,), (0,)), ((), ())), preferred_element_type=F32)
            s_ref[hd] = s_old * _decay_column(b_last) + upd
            o_ref[pl.ds(r0, c), vc] = _rms(o, gn).astype(BF16)
        return carry

    lax.fori_loop(0, ROW_TILE // c, chunk, 0)


def _interleave(tasks):
    tasks = list(tasks)
    while tasks:
        for t in list(tasks):
            try:
                next(t)
            except StopIteration:
                tasks.remove(t)


def _mix_in_task(x_ref, g_ref, wq_ref, wk_ref, wv_ref, wr_ref, wa_ref, wu_ref, wga_ref, wgb_ref, wal_ref, bal_ref,
                 sr_ref, z_ref, sga_ref, sgb_ref, ubuf, tile_in_seq, keep_halo, gla_inputs):
    h = _rms(x_ref[...], g_ref[...]).astype(BF16)
    a_code = _dot(h, wa_ref[...]).astype(BF16)
    yield
    u = _dot(h, wu_ref[...])
    yield

    ubuf[POOL_HALO:POOL_HALO + ROW_TILE, :] = u
    pos = lax.broadcasted_iota(jnp.int32, (ROW_TILE, 1), 0) + tile_in_seq * ROW_TILE
    for g, w in enumerate(POOL_WINDOWS):
        cols = slice(g * POOL_GC, (g + 1) * POOL_GC)
        s = ubuf[:, cols]
        sh = 1
        while sh < w:
            s = s + pltpu.roll(s, sh, 0)
            sh *= 2
        cnt = jnp.minimum(pos + 1, w).astype(F32)
        z_ref[:, cols] = (s[POOL_HALO:, :] / cnt - u[:, cols]).astype(BF16)
    ubuf[0:POOL_HALO, :] = jnp.where(keep_halo, ubuf[0:POOL_HALO, :], ubuf[ROW_TILE:ROW_TILE + POOL_HALO, :])

    r = _dot(h, wr_ref[...])
    sr_ref[...] = (r * _sigmoid(r)).astype(BF16)
    yield
    zg = _dot(a_code, wal_ref[...]) + bal_ref[...]
    yield
    la = (jnp.minimum(zg, 0.0) - jnp.log(1.0 + jnp.exp(-jnp.abs(zg)))) * (1.0 / GLA_GATE_TEMP)
    span = None
    for r0 in range(0, ROW_TILE, GLA_WIDE):
        sp = jnp.max(-jnp.sum(la[r0:r0 + GLA_WIDE, :], axis=0, keepdims=True))
        span = sp if span is None else jnp.maximum(span, sp)
    sga_ref[...] = _sigmoid(_dot(h, wga_ref[...])).astype(BF16)
    yield
    sgb_ref[...] = _sigmoid(_dot(h, wgb_ref[...])).astype(BF16)
    yield

    q = (_dot(h, wq_ref[...]) * (GLA_DK ** -0.5)).astype(BF16)
    yield
    v = _dot(h, wv_ref[...]).astype(BF16)
    yield
    k = _dot(h, wk_ref[...]).astype(BF16)
    gla_inputs.extend((q, k, v, la, span))


def _mix_gla_kernel(x_ref, g_ref, wq_ref, wk_ref, wv_ref, wr_ref, wa_ref, wu_ref, wga_ref, wgb_ref,
                    wal_ref, bal_ref, gn_ref,
                    sr_ref, z_ref, sga_ref, sgb_ref, o_ref,
                    q_s, k_s, v_s, la_s, s_ref, ubuf, span_s, *, tiles_per_seq, n_tiles):
    step = pl.program_id(0)
    mix_tile = jnp.minimum(step, n_tiles - 1)
    gla_tile = step - 1

    @pl.when(step == 0)
    def _():
        q_s[...] = jnp.zeros_like(q_s)
        k_s[...] = jnp.zeros_like(k_s)
        v_s[...] = jnp.zeros_like(v_s)
        la_s[...] = jnp.zeros_like(la_s)
        span_s[0] = 0.0

    @pl.when(mix_tile % tiles_per_seq == 0)
    def _():
        ubuf[0:POOL_HALO, :] = jnp.zeros((POOL_HALO, POOL_WIDTH), F32)

    @pl.when(jnp.logical_or(step == 0, gla_tile % tiles_per_seq == 0))
    def _():
        s_ref[...] = jnp.zeros_like(s_ref)

    keep_halo = step >= n_tiles - 1
    mix_args = (x_ref, g_ref, wq_ref, wk_ref, wv_ref, wr_ref, wa_ref, wu_ref, wga_ref, wgb_ref, wal_ref, bal_ref,
                sr_ref, z_ref, sga_ref, sgb_ref, ubuf, mix_tile % tiles_per_seq, keep_halo)
    gla_args = (q_s, k_s, v_s, la_s, gn_ref[...], o_ref, s_ref)

    def run(gla_tasks):
        nxt = []
        _interleave(gla_tasks + [_mix_in_task(*mix_args, nxt)])
        q_s[...], k_s[...], v_s[...], la_s[...], span_s[0] = nxt

    wide_ok = span_s[0] < GLA_WIDE_MAX_SPAN

    @pl.when(wide_ok)
    def _():
        run([_gla_wide_task(*gla_args)])

    @pl.when(jnp.logical_not(wide_ok))
    def _():
        _gla_anchored_chunks(*gla_args)
        run([])


def _mix_gla(x, g, wq, wk, wv, wr, wa, wu, wga, wgb, wal, bal, gn, seq):
    t = x.shape[0]
    n_tiles = t // ROW_TILE

    def cur(n):
        return pl.BlockSpec((ROW_TILE, n), lambda i: (jnp.minimum(i, n_tiles - 1), 0))

    prev = pl.BlockSpec((ROW_TILE, GLA_VW), lambda i: (jnp.maximum(i - 1, 0), 0))

    def out(n):
        return jax.ShapeDtypeStruct((t, n), BF16)

    outs, _ = _call(
        functools.partial(_mix_gla_kernel, tiles_per_seq=seq // ROW_TILE, n_tiles=n_tiles),
        (x, g, wq, wk, wv, wr, wa, wu, wga, wgb, wal, bal, gn),
        [cur(D_MODEL), _resident((1, D_MODEL)),
         _resident((D_MODEL, GLA_QK)), _resident((D_MODEL, GLA_QK)), _resident((D_MODEL, GLA_VW)),
         _resident((D_MODEL, GLA_VW)), _resident((D_MODEL, LANE)), _resident((D_MODEL, POOL_WIDTH)),
         _resident((D_MODEL, D_MODEL)), _resident((D_MODEL, D_MODEL)),
         _resident((LANE, GLA_QK)), _resident((1, GLA_QK)), _resident((1, GLA_DV))],
        [cur(GLA_VW), cur(POOL_WIDTH), cur(D_MODEL), cur(D_MODEL), prev],
        [out(GLA_VW), out(POOL_WIDTH), out(D_MODEL), out(D_MODEL), out(GLA_VW)],
        grid=(n_tiles + 1,),
        scratch_shapes=[pltpu.VMEM((ROW_TILE, GLA_QK), BF16), pltpu.VMEM((ROW_TILE, GLA_QK), BF16),
                        pltpu.VMEM((ROW_TILE, GLA_VW), BF16), pltpu.VMEM((ROW_TILE, GLA_QK), F32),
                        pltpu.VMEM((GLA_HEADS, GLA_DK, GLA_DV), F32),
                        pltpu.VMEM((POOL_HALO + ROW_TILE, POOL_WIDTH), F32),
                        pltpu.SMEM((1,), F32)],
        name="mix_gla")
    return outs


def _mem_kv_kernel(m_ref, g_ref, wk_ref, wv_ref, k_ref, v_ref):
    m = _rms(m_ref[...], g_ref[...]).astype(BF16)
    k_ref[...] = _dot(m, wk_ref[...]).astype(BF16)
    v_ref[...] = _dot(m, wv_ref[...]).astype(BF16)


def _mem_kv(mem, g, wk, wv, mem_len, make_jobs):
    t = mem.shape[0]
    steps = t // mem_len
    row = pl.BlockSpec((mem_len, D_MODEL), lambda i: (i, 0))
    out = jax.ShapeDtypeStruct((t, D_MODEL), BF16)
    return _call(
        _mem_kv_kernel, (mem, g, wk, wv),
        [row, _resident((1, D_MODEL)), _resident((D_MODEL, D_MODEL)), _resident((D_MODEL, D_MODEL))],
        [row, row], [out, out], make_jobs(steps), grid=(steps,), name="mem_kv")


def _mix_xattn_kernel(x_ref, o_ref, sr_ref, z_ref, sga_ref, sgb_ref, wua_ref, pm_ref, ps_ref, wub_ref, wmo_ref,
                      xg_ref, wq_ref, k_ref, v_ref, wo_ref, y_ref):
    zs = []
    for g in range(POOL_GROUPS):
        cols = slice(g * POOL_GC, (g + 1) * POOL_GC)
        zs.append((_dot(z_ref[:, cols], pm_ref[g]) * ps_ref[:, cols]).astype(BF16))
    ya = _dot(o_ref[...] * sr_ref[...], wua_ref[...])
    yb = _dot(jnp.concatenate(zs, axis=1), wub_ref[...])
    merged = (sga_ref[...].astype(F32) * ya + sgb_ref[...].astype(F32) * yb).astype(BF16)
    x = x_ref[...] + _dot(merged, wmo_ref[...])

    h = _rms(x, xg_ref[...]).astype(BF16)
    q = (_dot(h, wq_ref[...]) * (XA_HD ** -0.5)).astype(BF16)

    def scores(hd):
        cols = slice(hd * XA_HD, (hd + 1) * XA_HD)
        return lax.dot_general(q[:, cols], k_ref[:, cols], (((1,), (1,)), ((), ())), preferred_element_type=F32)

    outs = []
    s = scores(0)
    for hd in range(XA_HEADS):
        s_next = scores(hd + 1) if hd + 1 < XA_HEADS else None
        e = jnp.exp(s - jnp.max(s, axis=-1, keepdims=True))
        p = (e / jnp.sum(e, axis=-1, keepdims=True)).astype(BF16)
        outs.append(_dot(p, v_ref[:, hd * XA_HD:(hd + 1) * XA_HD]).astype(BF16))
        s = s_next
    y_ref[...] = x + _dot(jnp.concatenate(outs, axis=1), wo_ref[...])


def _mix_xattn(x, o, sr, z, sga, sgb, wua, pm, ps, wub, wmo, xg, wq, km, vm, wo, seq, mem_len):
    t = x.shape[0]
    steps = t // ROW_TILE
    tiles_per_seq = seq // ROW_TILE

    def row(n):
        return pl.BlockSpec((ROW_TILE, n), lambda i: (i, 0))

    mem = pl.BlockSpec((mem_len, D_MODEL), lambda i: (i // tiles_per_seq, 0))
    (y,), _ = _call(
        _mix_xattn_kernel, (x, o, sr, z, sga, sgb, wua, pm, ps, wub, wmo, xg, wq, km, vm, wo),
        [row(D_MODEL), row(GLA_VW), row(GLA_VW), row(POOL_WIDTH), row(D_MODEL), row(D_MODEL),
         _resident((GLA_VW, D_MODEL)), _resident((POOL_GROUPS, POOL_GC, POOL_GC)),
         _resident((1, POOL_WIDTH)), _resident((POOL_WIDTH, D_MODEL)), _resident((D_MODEL, D_MODEL)),
         _resident((1, D_MODEL)), _resident((D_MODEL, D_MODEL)), mem, mem, _resident((D_MODEL, D_MODEL))],
        [row(D_MODEL)], [jax.ShapeDtypeStruct((t, D_MODEL), F32)],
        grid=(steps,), name="mix_xattn")
    return y


def _layer(x, mem, ffn1_norm, ffn1_w1, ffn1_w3, ffn1_w2, mix_norm, w_in, w_alpha, b_alpha, gla_head_norm,
           w_up_a, pool_mix, pool_scale, w_up_b, w_mix_out, xa_norm, mem_norm, xa_wq, xa_wk, xa_wv, xa_wo,
           ffn2_norm, ffn2_w1, ffn2_w3, ffn2_w2, final_norm, last):
    batch, seq, _ = x.shape
    mem_len = mem.shape[1]
    assert seq % ROW_TILE == 0 and ROW_TILE % GLA_CHUNK == 0 and ROW_TILE % GLA_WIDE == 0

    def vec(a):
        return a.reshape(1, -1).astype(F32)

    def bf(a):
        return a.astype(BF16)

    def casts(*ws):
        return lambda n_steps: tuple(_cast_rows_job(w, n_steps) for w in ws)

    (km, vm), ffn1_w = _mem_kv(mem.reshape(batch * mem_len, D_MODEL), vec(mem_norm), bf(xa_wk), bf(xa_wv), mem_len,
                               casts(ffn1_w1, ffn1_w3, ffn1_w2))
    later = (w_up_a, w_up_b, w_mix_out, xa_wq, xa_wo, ffn2_w1, ffn2_w3, ffn2_w2)
    x1, cast = _ffn(x.reshape(batch * seq, D_MODEL), vec(ffn1_norm), *ffn1_w, vec(final_norm), False,
                    lambda n_steps: (_w_in_job(w_in, n_steps), *casts(*later)(n_steps)))
    wq, wk, wv, wr, wa, wu, wga, wgb = cast[:len(W_IN_SIZES)]
    wua, wub, wmo, xwq, xwo, *ffn2_w = cast[len(W_IN_SIZES):]
    wal = jnp.pad(bf(w_alpha), ((0, LANE - GLA_RANK), (0, 0)))
    sr, z, sga, sgb, o = _mix_gla(x1, vec(mix_norm), wq, wk, wv, wr, wa, wu, wga, wgb, wal, vec(b_alpha),
                                  vec(gla_head_norm), seq)
    x3 = _mix_xattn(x1, o, sr, z, sga, sgb, wua, bf(pool_mix), vec(pool_scale), wub, wmo,
                    vec(xa_norm), xwq, km, vm, xwo, seq, mem_len)
    x4, _ = _ffn(x3, vec(ffn2_norm), *ffn2_w, vec(final_norm), last)
    return x4.reshape(batch, seq, D_MODEL)


def kernel(x, mem, ffn1_norm, ffn1_w1, ffn1_w3, ffn1_w2, mix_norm, w_in, w_alpha, b_alpha, gla_head_norm, w_up_a,
           pool_mix, pool_scale, w_up_b, w_mix_out, xa_norm, mem_norm, xa_wq, xa_wk, xa_wv, xa_wo, ffn2_norm,
           ffn2_w1, ffn2_w3, ffn2_w2, final_norm):
    depth = ffn1_norm.shape[0]
    for l in range(depth):
        last = l == depth - 1
        x = _layer(x, mem, ffn1_norm[l], ffn1_w1[l], ffn1_w3[l], ffn1_w2[l], mix_norm[l], w_in[l], w_alpha[l],
                   b_alpha[l], gla_head_norm[l], w_up_a[l], pool_mix[l], pool_scale[l], w_up_b[l], w_mix_out[l],
                   xa_norm[l], mem_norm[l], xa_wq[l], xa_wk[l], xa_wv[l], xa_wo[l], ffn2_norm[l], ffn2_w1[l],
                   ffn2_w3[l], ffn2_w2[l], final_norm, last)
    return x
```

```python
import functools
from typing import Callable, NamedTuple

import jax
import jax.numpy as jnp
from jax import lax
from jax.experimental import pallas as pl
from jax.experimental.pallas import tpu as pltpu

F32 = jnp.float32
BF16 = jnp.bfloat16

D_MODEL = 1024
D_FF = 2816
EPS = 1e-6
GLA_HEADS = 4
GLA_DK = 128
GLA_DV = 256
GLA_QK = GLA_HEADS * GLA_DK
GLA_VW = GLA_HEADS * GLA_DV
GLA_RANK = 16
GLA_GATE_TEMP = 16.0
GLA_CHUNK = 64
GLA_SUB = 16
GLA_WIDE = 256
GLA_WIDE_MAX_SPAN = 60.0
EXP_TO_ZERO = -1e30
POOL_GROUPS = 4
POOL_GC = 128
POOL_WIDTH = POOL_GROUPS * POOL_GC
POOL_WINDOWS = (2, 4, 8, 16)
POOL_HALO = 16
XA_HEADS = 4
XA_HD = 256

LANE = 128
BF16_ROWS = 16
W_IN_SIZES = (GLA_QK, GLA_QK, GLA_VW, GLA_VW, GLA_RANK, POOL_WIDTH, D_MODEL, D_MODEL)
ROW_TILE = 512
FFN_ROWS = 1024
FF_SPLITS = (0, 1024, 2048, D_FF)
VMEM_LIMIT = 56 * 1024 * 1024


def _rms(x, g):
    return x * lax.rsqrt(jnp.mean(x * x, axis=-1, keepdims=True) + EPS) * g


def _sigmoid(x):
    return 0.5 * jnp.tanh(0.5 * x) + 0.5


def _dot(a, b):
    return jnp.dot(a, b, preferred_element_type=F32)


def _resident(shape):
    nd = len(shape)
    return pl.BlockSpec(shape, lambda *_: (0,) * nd, pipeline_mode=pl.Buffered(1))


def _params():
    return pltpu.CompilerParams(dimension_semantics=("arbitrary",), vmem_limit_bytes=VMEM_LIMIT)


class _CastJob(NamedTuple):
    inputs: tuple
    in_specs: tuple
    out_specs: tuple
    out_shapes: tuple
    body: Callable


def _cast_rows_job(w, n_steps):
    rows, cols = w.shape
    per_step = -(-rows // n_steps)
    block = next(b for b in range(BF16_ROWS, rows + 1, BF16_ROWS) if rows % b == 0 and b >= per_step)
    n_blocks = rows // block
    spec = pl.BlockSpec((block, cols), lambda i: ((i * n_blocks) // n_steps, 0))

    def body(ins, outs):
        outs[0][...] = ins[0][...].astype(BF16)

    return _CastJob((w,), (spec,), (spec,), (jax.ShapeDtypeStruct((rows, cols), BF16),), body)


def _w_in_job(w_in, n_steps):
    rows = w_in.shape[0]
    block = rows // n_steps
    assert rows % n_steps == 0 and block % BF16_ROWS == 0
    offs = [0]
    for n in W_IN_SIZES:
        offs.append(offs[-1] + n)
    widths = tuple(LANE if n == GLA_RANK else n for n in W_IN_SIZES)

    def spec(n):
        return pl.BlockSpec((block, n), lambda i: (i, 0))

    def body(ins, outs):
        for o_ref, n, lo in zip(outs, W_IN_SIZES, offs):
            if n == GLA_RANK:
                lane = lax.broadcasted_iota(jnp.int32, (block, LANE), 1)
                o_ref[...] = jnp.where(lane < GLA_RANK, ins[0][:, lo:lo + LANE], 0.0).astype(BF16)
            else:
                o_ref[...] = ins[0][:, lo:lo + n].astype(BF16)

    return _CastJob((w_in,), (spec(w_in.shape[1]),), tuple(spec(n) for n in widths),
                    tuple(jax.ShapeDtypeStruct((rows, n), BF16) for n in widths), body)


def _call(kernel_fn, args, in_specs, out_specs, out_shape, jobs=(), *, grid, scratch_shapes=(), name):
    n_in, n_out = len(args), len(out_specs)
    job_args = [a for j in jobs for a in j.inputs]
    job_outs = [o for j in jobs for o in j.out_shapes]

    def body(*refs):
        j_in = refs[n_in:n_in + len(job_args)]
        first_out = n_in + len(job_args)
        j_out = refs[first_out + n_out:first_out + n_out + len(job_outs)]
        kernel_fn(*refs[:n_in], *refs[first_out:first_out + n_out], *refs[first_out + n_out + len(job_outs):])
        for j in jobs:
            j.body(j_in[:len(j.inputs)], j_out[:len(j.out_shapes)])
            j_in, j_out = j_in[len(j.inputs):], j_out[len(j.out_shapes):]

    outs = pl.pallas_call(
        body,
        grid=grid,
        in_specs=[*in_specs, *(sp for j in jobs for sp in j.in_specs)],
        out_specs=[*out_specs, *(sp for j in jobs for sp in j.out_specs)],
        out_shape=[*out_shape, *job_outs],
        scratch_shapes=list(scratch_shapes),
        compiler_params=_params(),
        name=name,
    )(*args, *job_args)
    return outs[:n_out], outs[n_out:]


def _ffn_kernel(x_ref, g_ref, w1_ref, w3_ref, w2_ref, fg_ref, o_ref, *, final_norm):
    x = x_ref[...]
    h = _rms(x, g_ref[...]).astype(BF16)
    acc = jnp.zeros_like(x)
    for lo, hi in zip(FF_SPLITS[:-1], FF_SPLITS[1:]):
        a = _dot(h, w1_ref[:, lo:hi])
        b = _dot(h, w3_ref[:, lo:hi])
        act = (a * jax.nn.sigmoid(a) * b).astype(BF16)
        acc = acc + _dot(act, w2_ref[lo:hi, :])
    y = x + 0.5 * acc
    if final_norm:
        y = _rms(y, fg_ref[...])
    o_ref[...] = y


def _ffn(x, g, w1, w3, w2, fg, final_norm, make_jobs=None):
    t = x.shape[0]
    steps = t // FFN_ROWS
    row = pl.BlockSpec((FFN_ROWS, D_MODEL), lambda i: (i, 0))
    (y,), cast = _call(
        functools.partial(_ffn_kernel, final_norm=final_norm),
        (x, g, w1, w3, w2, fg),
        [row, _resident((1, D_MODEL)), _resident((D_MODEL, D_FF)), _resident((D_MODEL, D_FF)),
         _resident((D_FF, D_MODEL)), _resident((1, D_MODEL))],
        [row], [jax.ShapeDtypeStruct((t, D_MODEL), F32)],
        make_jobs(steps) if make_jobs else (),
        grid=(steps,), name="ffn_final" if final_norm else "ffn")
    return y, cast


def _cumsum_rows(mask_bf16, la):
    hi = la.astype(BF16)
    lo = (la - hi.astype(F32)).astype(BF16)
    return _dot(mask_bf16, hi) + _dot(mask_bf16, lo)


def _decay_column(b_last_row):
    return jnp.transpose(jnp.broadcast_to(jnp.exp(b_last_row), (8, GLA_DK)))[:, 0:1]


def _gla_wide_task(q_ref, k_ref, v_ref, la_ref, gn, o_ref, s_ref):
    c = GLA_WIDE
    ri = lax.broadcasted_iota(jnp.int32, (c, c), 0)
    ci = lax.broadcasted_iota(jnp.int32, (c, c), 1)
    causal = ci <= ri
    cum_incl = causal.astype(BF16)
    row_sl = [slice(ic * c, (ic + 1) * c) for ic in range(ROW_TILE // c)]
    b_all = [_cumsum_rows(cum_incl, la_ref[rows, :]) for rows in row_sl]
    units = [(ic, hd) for ic in range(len(row_sl)) for hd in range(GLA_HEADS)]

    def factors(ic, hd):
        rows, kc = row_sl[ic], slice(hd * GLA_DK, (hd + 1) * GLA_DK)
        b = b_all[ic][:, kc]
        qg = (q_ref[rows, kc].astype(F32) * jnp.exp(b)).astype(BF16)
        kg = (k_ref[rows, kc].astype(F32) * jnp.exp(-b)).astype(BF16)
        s = lax.dot_general(qg, kg, (((1,), (1,)), ((), ())), preferred_element_type=F32)
        return qg, kg, s, b[c - 1:c, :]

    nxt = factors(*units[0])
    yield
    for n, (ic, hd) in enumerate(units):
        qg, kg, s, b_last = nxt
        if n + 1 < len(units):
            nxt = factors(*units[n + 1])
        rows, vc = row_sl[ic], slice(hd * GLA_DV, (hd + 1) * GLA_DV)
        v = v_ref[rows, vc]
        s_old = s_ref[hd]
        inter = _dot(qg, s_old.astype(BF16))
        upd = lax.dot_general(kg, v, (((0,), (0,)), ((), ())), preferred_element_type=F32)
        p = jnp.where(causal, s, 0.0).astype(BF16)
        o = _dot(p, v) + inter
        s_ref[hd] = (s_old + upd) * _decay_column(b_last)
        o_ref[rows, vc] = _rms(o, gn).astype(BF16)
        yield


def _gla_anchored_chunks(q_ref, k_ref, v_ref, la_ref, gn, o_ref, s_ref):
    c = GLA_CHUNK
    ri = lax.broadcasted_iota(jnp.int32, (c, c), 0)
    ci = lax.broadcasted_iota(jnp.int32, (c, c), 1)
    causal = ci <= ri
    cum_incl = causal.astype(BF16)
    cum_anchor = (ci < (ri // GLA_SUB) * GLA_SUB).astype(BF16)
    krow = lax.broadcasted_iota(jnp.int32, (c, 1), 0)
    kcol = lax.broadcasted_iota(jnp.int32, (GLA_SUB, c), 1)

    def chunk(ic, carry):
        r0 = pl.multiple_of(ic * c, c)
        la = la_ref[pl.ds(r0, c), :]
        b_all = _cumsum_rows(cum_incl, la)
        beta_all = _cumsum_rows(cum_anchor, la)
        for hd in range(GLA_HEADS):
            kc = slice(hd * GLA_DK, (hd + 1) * GLA_DK)
            vc = slice(hd * GLA_DV, (hd + 1) * GLA_DV)
            q = q_ref[pl.ds(r0, c), kc].astype(F32)
            k = k_ref[pl.ds(r0, c), kc].astype(F32)
            v = v_ref[pl.ds(r0, c), vc]
            b = b_all[:, kc]
            beta = beta_all[:, kc]
            qa = (q * jnp.exp(b - beta)).astype(BF16)
            rows = []
            for blk in range(c // GLA_SUB):
                r_lo, r_hi = blk * GLA_SUB, (blk + 1) * GLA_SUB
                e = jnp.where(krow < r_lo, beta[r_lo:r_lo + 1, :] - b, EXP_TO_ZERO)
                ka = (k * jnp.exp(e)).astype(BF16)
                sc = lax.dot_general(qa[r_lo:r_hi, :], ka, (((1,), (1,)), ((), ())), preferred_element_type=F32)
                qb, bb = q[r_lo:r_hi, :], b[r_lo:r_hi, :]
                for j in range(r_lo, r_hi):
                    w = jnp.exp(jnp.minimum(bb - b[j:j + 1, :], 0.0))
                    sc = jnp.where(kcol == j, jnp.sum(qb * k[j:j + 1, :] * w, axis=1, keepdims=True), sc)
                rows.append(sc)
            p = jnp.where(causal, jnp.concatenate(rows, axis=0), 0.0).astype(BF16)
            s_old = s_ref[hd]
            o = _dot(p, v) + _dot((q * jnp.exp(b)).astype(BF16), s_old.astype(BF16))
            b_last = b[c - 1:c, :]
            k_dec = (k * jnp.exp(b_last - b)).astype(BF16)
            upd = lax.dot_general(k_dec, v, (((0,), (0,)), ((), ())), preferred_element_type=F32)
            s_ref[hd] = s_old * _decay_column(b_last) + upd
            o_ref[pl.ds(r0, c), vc] = _rms(o, gn).astype(BF16)
        return carry

    lax.fori_loop(0, ROW_TILE // c, chunk, 0)


def _interleave(tasks):
    tasks = list(tasks)
    while tasks:
        for t in list(tasks):
            try:
                next(t)
            except StopIteration:
                tasks.remove(t)


def _mix_in_task(x_ref, g_ref, wq_ref, wk_ref, wv_ref, wr_ref, wa_ref, wu_ref, wga_ref, wgb_ref, wal_ref, bal_ref,
                 sr_ref, z_ref, sga_ref, sgb_ref, ubuf, tile_in_seq, keep_halo, gla_inputs):
    h = _rms(x_ref[...], g_ref[...]).astype(BF16)
    a_code = _dot(h, wa_ref[...]).astype(BF16)
    yield
    u = _dot(h, wu_ref[...])
    yield
    r = _dot(h, wr_ref[...])
    sr_ref[...] = (r * _sigmoid(r)).astype(BF16)
    yield
    zg = _dot(a_code, wal_ref[...]) + bal_ref[...]
    yield
    sga_ref[...] = _sigmoid(_dot(h, wga_ref[...])).astype(BF16)
    yield
    sgb_ref[...] = _sigmoid(_dot(h, wgb_ref[...])).astype(BF16)
    yield

    ubuf[POOL_HALO:POOL_HALO + ROW_TILE, :] = u
    pos = lax.broadcasted_iota(jnp.int32, (ROW_TILE, 1), 0) + tile_in_seq * ROW_TILE
    for g, w in enumerate(POOL_WINDOWS):
        cols = slice(g * POOL_GC, (g + 1) * POOL_GC)
        s = ubuf[:, cols]
        sh = 1
        while sh < w:
            s = s + pltpu.roll(s, sh, 0)
            sh *= 2
        cnt = jnp.minimum(pos + 1, w).astype(F32)
        z_ref[:, cols] = (s[POOL_HALO:, :] / cnt - u[:, cols]).astype(BF16)
    ubuf[0:POOL_HALO, :] = jnp.where(keep_halo, ubuf[0:POOL_HALO, :], ubuf[ROW_TILE:ROW_TILE + POOL_HALO, :])

    q = (_dot(h, wq_ref[...]) * (GLA_DK ** -0.5)).astype(BF16)
    yield
    v = _dot(h, wv_ref[...]).astype(BF16)
    yield
    k = _dot(h, wk_ref[...]).astype(BF16)
    la = (jnp.minimum(zg, 0.0) - jnp.log(1.0 + jnp.exp(-jnp.abs(zg)))) * (1.0 / GLA_GATE_TEMP)
    span = None
    for r0 in range(0, ROW_TILE, GLA_WIDE):
        sp = jnp.max(-jnp.sum(la[r0:r0 + GLA_WIDE, :], axis=0, keepdims=True))
        span = sp if span is None else jnp.maximum(span, sp)
    gla_inputs.extend((q, k, v, la, span))


def _mix_gla_kernel(x_ref, g_ref, wq_ref, wk_ref, wv_ref, wr_ref, wa_ref, wu_ref, wga_ref, wgb_ref,
                    wal_ref, bal_ref, gn_ref,
                    sr_ref, z_ref, sga_ref, sgb_ref, o_ref,
                    q_s, k_s, v_s, la_s, s_ref, ubuf, span_s, *, tiles_per_seq, n_tiles):
    step = pl.program_id(0)
    mix_tile = jnp.minimum(step, n_tiles - 1)
    gla_tile = step - 1

    @pl.when(step == 0)
    def _():
        q_s[...] = jnp.zeros_like(q_s)
        k_s[...] = jnp.zeros_like(k_s)
        v_s[...] = jnp.zeros_like(v_s)
        la_s[...] = jnp.zeros_like(la_s)
        span_s[0] = 0.0

    @pl.when(mix_tile % tiles_per_seq == 0)
    def _():
        ubuf[0:POOL_HALO, :] = jnp.zeros((POOL_HALO, POOL_WIDTH), F32)

    @pl.when(jnp.logical_or(step == 0, gla_tile % tiles_per_seq == 0))
    def _():
        s_ref[...] = jnp.zeros_like(s_ref)

    keep_halo = step >= n_tiles - 1
    mix_args = (x_ref, g_ref, wq_ref, wk_ref, wv_ref, wr_ref, wa_ref, wu_ref, wga_ref, wgb_ref, wal_ref, bal_ref,
                sr_ref, z_ref, sga_ref, sgb_ref, ubuf, mix_tile % tiles_per_seq, keep_halo)
    gla_args = (q_s, k_s, v_s, la_s, gn_ref[...], o_ref, s_ref)

    def run(gla_tasks):
        nxt = []
        _interleave(gla_tasks + [_mix_in_task(*mix_args, nxt)])
        q_s[...], k_s[...], v_s[...], la_s[...], span_s[0] = nxt

    wide_ok = span_s[0] < GLA_WIDE_MAX_SPAN

    @pl.when(wide_ok)
    def _():
        run([_gla_wide_task(*gla_args)])

    @pl.when(jnp.logical_not(wide_ok))
    def _():
        _gla_anchored_chunks(*gla_args)
        run([])


def _mix_gla(x, g, wq, wk, wv, wr, wa, wu, wga, wgb, wal, bal, gn, seq):
    t = x.shape[0]
    n_tiles = t // ROW_TILE

    def cur(n):
        return pl.BlockSpec((ROW_TILE, n), lambda i: (jnp.minimum(i, n_tiles - 1), 0))

    prev = pl.BlockSpec((ROW_TILE, GLA_VW), lambda i: (jnp.maximum(i - 1, 0), 0))

    def out(n):
        return jax.ShapeDtypeStruct((t, n), BF16)

    outs, _ = _call(
        functools.partial(_mix_gla_kernel, tiles_per_seq=seq // ROW_TILE, n_tiles=n_tiles),
        (x, g, wq, wk, wv, wr, wa, wu, wga, wgb, wal, bal, gn),
        [cur(D_MODEL), _resident((1, D_MODEL)),
         _resident((D_MODEL, GLA_QK)), _resident((D_MODEL, GLA_QK)), _resident((D_MODEL, GLA_VW)),
         _resident((D_MODEL, GLA_VW)), _resident((D_MODEL, LANE)), _resident((D_MODEL, POOL_WIDTH)),
         _resident((D_MODEL, D_MODEL)), _resident((D_MODEL, D_MODEL)),
         _resident((LANE, GLA_QK)), _resident((1, GLA_QK)), _resident((1, GLA_DV))],
        [cur(GLA_VW), cur(POOL_WIDTH), cur(D_MODEL), cur(D_MODEL), prev],
        [out(GLA_VW), out(POOL_WIDTH), out(D_MODEL), out(D_MODEL), out(GLA_VW)],
        grid=(n_tiles + 1,),
        scratch_shapes=[pltpu.VMEM((ROW_TILE, GLA_QK), BF16), pltpu.VMEM((ROW_TILE, GLA_QK), BF16),
                        pltpu.VMEM((ROW_TILE, GLA_VW), BF16), pltpu.VMEM((ROW_TILE, GLA_QK), F32),
                        pltpu.VMEM((GLA_HEADS, GLA_DK, GLA_DV), F32),
                        pltpu.VMEM((POOL_HALO + ROW_TILE, POOL_WIDTH), F32),
                        pltpu.SMEM((1,), F32)],
        name="mix_gla")
    return outs


def _mem_kv_kernel(m_ref, g_ref, wk_ref, wv_ref, k_ref, v_ref):
    m = _rms(m_ref[...], g_ref[...]).astype(BF16)
    k_ref[...] = _dot(m, wk_ref[...]).astype(BF16)
    v_ref[...] = _dot(m, wv_ref[...]).astype(BF16)


def _mem_kv(mem, g, wk, wv, mem_len, make_jobs):
    t = mem.shape[0]
    steps = t // mem_len
    row = pl.BlockSpec((mem_len, D_MODEL), lambda i: (i, 0))
    out = jax.ShapeDtypeStruct((t, D_MODEL), BF16)
    return _call(
        _mem_kv_kernel, (mem, g, wk, wv),
        [row, _resident((1, D_MODEL)), _resident((D_MODEL, D_MODEL)), _resident((D_MODEL, D_MODEL))],
        [row, row], [out, out], make_jobs(steps), grid=(steps,), name="mem_kv")


def _mix_xattn_kernel(x_ref, o_ref, sr_ref, z_ref, sga_ref, sgb_ref, wua_ref, pm_ref, ps_ref, wub_ref, wmo_ref,
                      xg_ref, wq_ref, k_ref, v_ref, wo_ref, y_ref):
    zs = []
    for g in range(POOL_GROUPS):
        cols = slice(g * POOL_GC, (g + 1) * POOL_GC)
        zs.append((_dot(z_ref[:, cols], pm_ref[g]) * ps_ref[:, cols]).astype(BF16))
    ya = _dot(o_ref[...] * sr_ref[...], wua_ref[...])
    yb = _dot(jnp.concatenate(zs, axis=1), wub_ref[...])
    merged = (sga_ref[...].astype(F32) * ya + sgb_ref[...].astype(F32) * yb).astype(BF16)
    x = x_ref[...] + _dot(merged, wmo_ref[...])

    inv_rms = lax.rsqrt(jnp.mean(x * x, axis=-1, keepdims=True) + EPS)
    q = (_dot((x * xg_ref[...]).astype(BF16), wq_ref[...]) * (inv_rms * (XA_HD ** -0.5))).astype(BF16)

    def scores(hd):
        cols = slice(hd * XA_HD, (hd + 1) * XA_HD)
        return lax.dot_general(q[:, cols], k_ref[:, cols], (((1,), (1,)), ((), ())), preferred_element_type=F32)

    outs = []
    s = scores(0)
    for hd in range(XA_HEADS):
        s_next = scores(hd + 1) if hd + 1 < XA_HEADS else None
        e = jnp.exp(s - jnp.max(s, axis=-1, keepdims=True))
        o = _dot(e.astype(BF16), v_ref[:, hd * XA_HD:(hd + 1) * XA_HD])
        outs.append((o / jnp.sum(e, axis=-1, keepdims=True)).astype(BF16))
        s = s_next
    y_ref[...] = x + _dot(jnp.concatenate(outs, axis=1), wo_ref[...])


def _mix_xattn(x, o, sr, z, sga, sgb, wua, pm, ps, wub, wmo, xg, wq, km, vm, wo, seq, mem_len):
    t = x.shape[0]
    steps = t // ROW_TILE
    tiles_per_seq = seq // ROW_TILE

    def row(n):
        return pl.BlockSpec((ROW_TILE, n), lambda i: (i, 0))

    mem = pl.BlockSpec((mem_len, D_MODEL), lambda i: (i // tiles_per_seq, 0))
    (y,), _ = _call(
        _mix_xattn_kernel, (x, o, sr, z, sga, sgb, wua, pm, ps, wub, wmo, xg, wq, km, vm, wo),
        [row(D_MODEL), row(GLA_VW), row(GLA_VW), row(POOL_WIDTH), row(D_MODEL), row(D_MODEL),
         _resident((GLA_VW, D_MODEL)), _resident((POOL_GROUPS, POOL_GC, POOL_GC)),
         _resident((1, POOL_WIDTH)), _resident((POOL_WIDTH, D_MODEL)), _resident((D_MODEL, D_MODEL)),
         _resident((1, D_MODEL)), _resident((D_MODEL, D_MODEL)), mem, mem, _resident((D_MODEL, D_MODEL))],
        [row(D_MODEL)], [jax.ShapeDtypeStruct((t, D_MODEL), F32)],
        grid=(steps,), name="mix_xattn")
    return y


def _layer(x, mem, ffn1_norm, ffn1_w1, ffn1_w3, ffn1_w2, mix_norm, w_in, w_alpha, b_alpha, gla_head_norm,
           w_up_a, pool_mix, pool_scale, w_up_b, w_mix_out, xa_norm, mem_norm, xa_wq, xa_wk, xa_wv, xa_wo,
           ffn2_norm, ffn2_w1, ffn2_w3, ffn2_w2, final_norm, last):
    batch, seq, _ = x.shape
    mem_len = mem.shape[1]
    assert seq % ROW_TILE == 0 and ROW_TILE % GLA_CHUNK == 0 and ROW_TILE % GLA_WIDE == 0

    def vec(a):
        return a.reshape(1, -1).astype(F32)

    def bf(a):
        return a.astype(BF16)

    def casts(*ws):
        return lambda n_steps: tuple(_cast_rows_job(w, n_steps) for w in ws)

    (km, vm), ffn1_w = _mem_kv(mem.reshape(batch * mem_len, D_MODEL), vec(mem_norm), bf(xa_wk), bf(xa_wv), mem_len,
                               casts(ffn1_w1, ffn1_w3, ffn1_w2))
    later = (w_up_a, w_up_b, w_mix_out, xa_wq, xa_wo, ffn2_w1, ffn2_w3, ffn2_w2)
    x1, cast = _ffn(x.reshape(batch * seq, D_MODEL), vec(ffn1_norm), *ffn1_w, vec(final_norm), False,
                    lambda n_steps: (_w_in_job(w_in, n_steps), *casts(*later)(n_steps)))
    wq, wk, wv, wr, wa, wu, wga, wgb = cast[:len(W_IN_SIZES)]
    wua, wub, wmo, xwq, xwo, *ffn2_w = cast[len(W_IN_SIZES):]
    wal = jnp.pad(bf(w_alpha), ((0, LANE - GLA_RANK), (0, 0)))
    sr, z, sga, sgb, o = _mix_gla(x1, vec(mix_norm), wq, wk, wv, wr, wa, wu, wga, wgb, wal, vec(b_alpha),
                                  vec(gla_head_norm), seq)
    x3 = _mix_xattn(x1, o, sr, z, sga, sgb, wua, bf(pool_mix), vec(pool_scale), wub, wmo,
                    vec(xa_norm), xwq, km, vm, xwo, seq, mem_len)
    x4, _ = _ffn(x3, vec(ffn2_norm), *ffn2_w, vec(final_norm), last)
    return x4.reshape(batch, seq, D_MODEL)


def kernel(x, mem, ffn1_norm, ffn1_w1, ffn1_w3, ffn1_w2, mix_norm, w_in, w_alpha, b_alpha, gla_head_norm, w_up_a,
           pool_mix, pool_scale, w_up_b, w_mix_out, xa_norm, mem_norm, xa_wq, xa_wk, xa_wv, xa_wo, ffn2_norm,
           ffn2_w1, ffn2_w3, ffn2_w2, final_norm):
    depth = ffn1_norm.shape[0]
    for l in range(depth):
        last = l == depth - 1
        x = _layer(x, mem, ffn1_norm[l], ffn1_w1[l], ffn1_w3[l], ffn1_w2[l], mix_norm[l], w_in[l], w_alpha[l],
                   b_alpha[l], gla_head_norm[l], w_up_a[l], pool_mix[l], pool_scale[l], w_up_b[l], w_mix_out[l],
                   xa_norm[l], mem_norm[l], xa_wq[l], xa_wk[l], xa_wv[l], xa_wo[l], ffn2_norm[l], ffn2_w1[l],
                   ffn2_w3[l], ffn2_w2[l], final_norm, last)
    return x
```

```python
import functools
from typing import Callable, NamedTuple

import jax
import jax.numpy as jnp
from jax import lax
from jax.experimental import pallas as pl
from jax.experimental.pallas import tpu as pltpu

F32 = jnp.float32
BF16 = jnp.bfloat16

D_MODEL = 1024
D_FF = 2816
EPS = 1e-6
GLA_HEADS = 4
GLA_DK = 128
GLA_DV = 256
GLA_QK = GLA_HEADS * GLA_DK
GLA_VW = GLA_HEADS * GLA_DV
GLA_RANK = 16
GLA_GATE_TEMP = 16.0
GLA_CHUNK = 64
GLA_SUB = 16
GLA_WIDE = 256
GLA_WIDE_MAX_SPAN = 60.0
EXP_TO_ZERO = -1e30
POOL_GROUPS = 4
POOL_GC = 128
POOL_WIDTH = POOL_GROUPS * POOL_GC
POOL_WINDOWS = (2, 4, 8, 16)
POOL_HALO = 16
XA_HEADS = 4
XA_HD = 256

LANE = 128
BF16_ROWS = 16
W_IN_SIZES = (GLA_QK, GLA_QK, GLA_VW, GLA_VW, GLA_RANK, POOL_WIDTH, D_MODEL, D_MODEL)
W_IN_SPLIT = 4
ROW_TILE = 512
FFN_ROWS = 1024
FF_SPLITS = (0, 1024, 2048, D_FF)
VMEM_LIMIT = 56 * 1024 * 1024


def _rms(x, g):
    return x * lax.rsqrt(jnp.mean(x * x, axis=-1, keepdims=True) + EPS) * g


def _sigmoid(x):
    return 0.5 * jnp.tanh(0.5 * x) + 0.5


def _dot(a, b):
    return jnp.dot(a, b, preferred_element_type=F32)


def _resident(shape):
    nd = len(shape)
    return pl.BlockSpec(shape, lambda *_: (0,) * nd, pipeline_mode=pl.Buffered(1))


def _params():
    return pltpu.CompilerParams(dimension_semantics=("arbitrary",), vmem_limit_bytes=VMEM_LIMIT)


class _CastJob(NamedTuple):
    inputs: tuple
    in_specs: tuple
    out_specs: tuple
    out_shapes: tuple
    body: Callable


def _cast_rows_job(w, n_steps):
    rows, cols = w.shape
    per_step = -(-rows // n_steps)
    block = next(b for b in range(BF16_ROWS, rows + 1, BF16_ROWS) if rows % b == 0 and b >= per_step)
    n_blocks = rows // block
    spec = pl.BlockSpec((block, cols), lambda i: ((i * n_blocks) // n_steps, 0))

    def body(ins, outs):
        outs[0][...] = ins[0][...].astype(BF16)

    return _CastJob((w,), (spec,), (spec,), (jax.ShapeDtypeStruct((rows, cols), BF16),), body)


def _w_in_job(w_in_t, n_steps):
    cols = w_in_t.shape[1]
    rows_a = sum(W_IN_SIZES[:W_IN_SPLIT])
    rows_b = sum(W_IN_SIZES[W_IN_SPLIT:])
    assert w_in_t.shape[0] == rows_a + rows_b and rows_a % n_steps == 0
    blk_a = rows_a // n_steps
    per_b = -(-rows_b // n_steps)
    blk_b = next(r for r in range(BF16_ROWS, rows_b + 1, BF16_ROWS) if rows_b % r == 0 and r >= per_b)
    last_b = rows_b // blk_b - 1
    assert blk_a % BF16_ROWS == 0
    in_a = pl.BlockSpec((blk_a, cols), lambda i: (i, 0))
    in_b = pl.BlockSpec((pl.Element(blk_b), pl.Element(cols)),
                        lambda i: (pl.multiple_of(rows_a + blk_b * jnp.minimum(i, last_b), BF16_ROWS), 0))
    out_b = pl.BlockSpec((blk_b, cols), lambda i: (jnp.minimum(i, last_b), 0))

    def body(ins, outs):
        for i_ref, o_ref in zip(ins, outs):
            o_ref[...] = i_ref[...].astype(BF16)

    return _CastJob((w_in_t, w_in_t), (in_a, in_b), (in_a, out_b),
                    (jax.ShapeDtypeStruct((rows_a, cols), BF16), jax.ShapeDtypeStruct((rows_b, cols), BF16)), body)


def _call(kernel_fn, args, in_specs, out_specs, out_shape, jobs=(), *, grid, scratch_shapes=(), name):
    n_in, n_out = len(args), len(out_specs)
    job_args = [a for j in jobs for a in j.inputs]
    job_outs = [o for j in jobs for o in j.out_shapes]

    def body(*refs):
        j_in = refs[n_in:n_in + len(job_args)]
        first_out = n_in + len(job_args)
        j_out = refs[first_out + n_out:first_out + n_out + len(job_outs)]
        kernel_fn(*refs[:n_in], *refs[first_out:first_out + n_out], *refs[first_out + n_out + len(job_outs):])
        for j in jobs:
            j.body(j_in[:len(j.inputs)], j_out[:len(j.out_shapes)])
            j_in, j_out = j_in[len(j.inputs):], j_out[len(j.out_shapes):]

    outs = pl.pallas_call(
        body,
        grid=grid,
        in_specs=[*in_specs, *(sp for j in jobs for sp in j.in_specs)],
        out_specs=[*out_specs, *(sp for j in jobs for sp in j.out_specs)],
        out_shape=[*out_shape, *job_outs],
        scratch_shapes=list(scratch_shapes),
        compiler_params=_params(),
        name=name,
    )(*args, *job_args)
    return outs[:n_out], outs[n_out:]


def _ffn_kernel(x_ref, g_ref, w1_ref, w3_ref, w2_ref, fg_ref, o_ref, *, final_norm):
    x = x_ref[...]
    h = _rms(x, g_ref[...]).astype(BF16)
    acc = jnp.zeros_like(x)
    for lo, hi in zip(FF_SPLITS[:-1], FF_SPLITS[1:]):
        a = _dot(h, w1_ref[:, lo:hi])
        b = _dot(h, w3_ref[:, lo:hi])
        act = (a * jax.nn.sigmoid(a) * b).astype(BF16)
        acc = acc + _dot(act, w2_ref[lo:hi, :])
    y = x + 0.5 * acc
    if final_norm:
        y = _rms(y, fg_ref[...])
    o_ref[...] = y


def _ffn(x, g, w1, w3, w2, fg, final_norm, make_jobs=None):
    t = x.shape[0]
    steps = t // FFN_ROWS
    row = pl.BlockSpec((FFN_ROWS, D_MODEL), lambda i: (i, 0))
    (y,), cast = _call(
        functools.partial(_ffn_kernel, final_norm=final_norm),
        (x, g, w1, w3, w2, fg),
        [row, _resident((1, D_MODEL)), _resident((D_MODEL, D_FF)), _resident((D_MODEL, D_FF)),
         _resident((D_FF, D_MODEL)), _resident((1, D_MODEL))],
        [row], [jax.ShapeDtypeStruct((t, D_MODEL), F32)],
        make_jobs(steps) if make_jobs else (),
        grid=(steps,), name="ffn_final" if final_norm else "ffn")
    return y, cast


def _cumsum_rows(mask_bf16, la):
    hi = la.astype(BF16)
    lo = (la - hi.astype(F32)).astype(BF16)
    return _dot(mask_bf16, hi) + _dot(mask_bf16, lo)


def _decay_column(b_last_row):
    return jnp.transpose(jnp.broadcast_to(jnp.exp(b_last_row), (8, GLA_DK)))[:, 0:1]


def _gla_wide_task(q_ref, k_ref, v_ref, la_ref, gn, o_ref, s_ref):
    c = GLA_WIDE
    ri = lax.broadcasted_iota(jnp.int32, (c, c), 0)
    ci = lax.broadcasted_iota(jnp.int32, (c, c), 1)
    causal = ci <= ri
    cum_incl = causal.astype(BF16)
    row_sl = [slice(ic * c, (ic + 1) * c) for ic in range(ROW_TILE // c)]
    b_all = [_dot(cum_incl, la_ref[rows, :].astype(BF16)) for rows in row_sl]
    units = [(ic, hd) for ic in range(len(row_sl)) for hd in range(GLA_HEADS)]

    def factors(ic, hd):
        rows, kc = row_sl[ic], slice(hd * GLA_DK, (hd + 1) * GLA_DK)
        b = b_all[ic][:, kc]
        qg = (q_ref[rows, kc].astype(F32) * jnp.exp(b)).astype(BF16)
        kg = (k_ref[rows, kc].astype(F32) * jnp.exp(-b)).astype(BF16)
        s = lax.dot_general(qg, kg, (((1,), (1,)), ((), ())), preferred_element_type=F32)
        return qg, kg, s, b[c - 1:c, :]

    nxt = factors(*units[0])
    yield
    for n, (ic, hd) in enumerate(units):
        qg, kg, s, b_last = nxt
        if n + 1 < len(units):
            nxt = factors(*units[n + 1])
        rows, vc = row_sl[ic], slice(hd * GLA_DV, (hd + 1) * GLA_DV)
        v = v_ref[rows, vc]
        s_old = s_ref[hd]
        inter = _dot(qg, s_old.astype(BF16))
        upd = lax.dot_general(kg, v, (((0,), (0,)), ((), ())), preferred_element_type=F32)
        p = jnp.where(causal, s, 0.0).astype(BF16)
        o = _dot(p, v) + inter
        s_ref[hd] = (s_old + upd) * _decay_column(b_last)
        o_ref[rows, vc] = _rms(o, gn).astype(BF16)
        yield


def _gla_anchored_chunks(q_ref, k_ref, v_ref, la_ref, gn, o_ref, s_ref):
    c = GLA_CHUNK
    ri = lax.broadcasted_iota(jnp.int32, (c, c), 0)
    ci = lax.broadcasted_iota(jnp.int32, (c, c), 1)
    causal = ci <= ri
    cum_incl = causal.astype(BF16)
    cum_anchor = (ci < (ri // GLA_SUB) * GLA_SUB).astype(BF16)
    krow = lax.broadcasted_iota(jnp.int32, (c, 1), 0)
    kcol = lax.broadcasted_iota(jnp.int32, (GLA_SUB, c), 1)

    def chunk(ic, carry):
        r0 = pl.multiple_of(ic * c, c)
        la = la_ref[pl.ds(r0, c), :]
        b_all = _cumsum_rows(cum_incl, la)
        beta_all = _cumsum_rows(cum_anchor, la)
        for hd in range(GLA_HEADS):
            kc = slice(hd * GLA_DK, (hd + 1) * GLA_DK)
            vc = slice(hd * GLA_DV, (hd + 1) * GLA_DV)
            q = q_ref[pl.ds(r0, c), kc].astype(F32)
            k = k_ref[pl.ds(r0, c), kc].astype(F32)
            v = v_ref[pl.ds(r0, c), vc]
            b = b_all[:, kc]
            beta = beta_all[:, kc]
            qa = (q * jnp.exp(b - beta)).astype(BF16)
            rows = []
            for blk in range(c // GLA_SUB):
                r_lo, r_hi = blk * GLA_SUB, (blk + 1) * GLA_SUB
                e = jnp.where(krow < r_lo, beta[r_lo:r_lo + 1, :] - b, EXP_TO_ZERO)
                ka = (k * jnp.exp(e)).astype(BF16)
                sc = lax.dot_general(qa[r_lo:r_hi, :], ka, (((1,), (1,)), ((), ())), preferred_element_type=F32)
                qb, bb = q[r_lo:r_hi, :], b[r_lo:r_hi, :]
                for j in range(r_lo, r_hi):
                    w = jnp.exp(jnp.minimum(bb - b[j:j + 1, :], 0.0))
                    sc = jnp.where(kcol == j, jnp.sum(qb * k[j:j + 1, :] * w, axis=1, keepdims=True), sc)
                rows.append(sc)
            p = jnp.where(causal, jnp.concatenate(rows, axis=0), 0.0).astype(BF16)
            s_old = s_ref[hd]
            o = _dot(p, v) + _dot((q * jnp.exp(b)).astype(BF16), s_old.astype(BF16))
            b_last = b[c - 1:c, :]
            k_dec = (k * jnp.exp(b_last - b)).astype(BF16)
            upd = lax.dot_general(k_dec, v, (((0,), (0,)), ((), ())), preferred_element_type=F32)
            s_ref[hd] = s_old * _decay_column(b_last) + upd
            o_ref[pl.ds(r0, c), vc] = _rms(o, gn).astype(BF16)
        return carry

    lax.fori_loop(0, ROW_TILE // c, chunk, 0)


def _interleave(tasks):
    tasks = list(tasks)
    while tasks:
        for t in list(tasks):
            try:
                next(t)
            except StopIteration:
                tasks.remove(t)


def _mix_in_task(x_ref, g_ref, wq_ref, wk_ref, wv_ref, wr_ref, wa_ref, wu_ref, wga_ref, wgb_ref, wal_ref, bal_ref,
                 sr_ref, z_ref, sga_ref, sgb_ref, ubuf, tile_in_seq, keep_halo, gla_inputs):
    h = _rms(x_ref[...], g_ref[...]).astype(BF16)
    a_code = _dot(h, wa_ref[...]).astype(BF16)
    yield
    u = _dot(h, wu_ref[...])
    yield
    r = _dot(h, wr_ref[...])
    sr_ref[...] = (r * _sigmoid(r)).astype(BF16)
    yield
    zg = _dot(a_code, wal_ref[...]) + bal_ref[...]
    yield
    sga_ref[...] = _sigmoid(_dot(h, wga_ref[...])).astype(BF16)
    yield
    sgb_ref[...] = _sigmoid(_dot(h, wgb_ref[...])).astype(BF16)
    yield

    ubuf[POOL_HALO:POOL_HALO + ROW_TILE, :] = u
    pos = lax.broadcasted_iota(jnp.int32, (ROW_TILE, 1), 0) + tile_in_seq * ROW_TILE
    for g, w in enumerate(POOL_WINDOWS):
        cols = slice(g * POOL_GC, (g + 1) * POOL_GC)
        s = ubuf[:, cols]
        sh = 1
        while sh < w:
            s = s + pltpu.roll(s, sh, 0)
            sh *= 2
        cnt = jnp.minimum(pos + 1, w).astype(F32)
        z_ref[:, cols] = (s[POOL_HALO:, :] / cnt - u[:, cols]).astype(BF16)
    ubuf[0:POOL_HALO, :] = jnp.where(keep_halo, ubuf[0:POOL_HALO, :], ubuf[ROW_TILE:ROW_TILE + POOL_HALO, :])

    q = (_dot(h, wq_ref[...]) * (GLA_DK ** -0.5)).astype(BF16)
    yield
    v = _dot(h, wv_ref[...]).astype(BF16)
    yield
    k = _dot(h, wk_ref[...]).astype(BF16)
    la = (jnp.minimum(zg, 0.0) - jnp.log(1.0 + jnp.exp(-jnp.abs(zg)))) * (1.0 / GLA_GATE_TEMP)
    span = None
    for r0 in range(0, ROW_TILE, GLA_WIDE):
        sp = jnp.max(-jnp.sum(la[r0:r0 + GLA_WIDE, :], axis=0, keepdims=True))
        span = sp if span is None else jnp.maximum(span, sp)
    gla_inputs.extend((q, k, v, la, span))


def _mix_gla_kernel(x_ref, g_ref, wt_a_ref, wt_b_ref, wal_ref, bal_ref, gn_ref,
                    sr_ref, z_ref, sga_ref, sgb_ref, o_ref,
                    q_s, k_s, v_s, la_s, s_ref, ubuf, span_s,
                    wq_ref, wk_ref, wv_ref, wr_ref, wa_ref, wu_ref, wga_ref, wgb_ref, *, tiles_per_seq, n_tiles):
    step = pl.program_id(0)
    mix_tile = jnp.minimum(step, n_tiles - 1)
    gla_tile = step - 1

    @pl.when(step == 0)
    def _():
        q_s[...] = jnp.zeros_like(q_s)
        k_s[...] = jnp.zeros_like(k_s)
        v_s[...] = jnp.zeros_like(v_s)
        la_s[...] = jnp.zeros_like(la_s)
        span_s[0] = 0.0
        lo = 0
        for w_ref, n in zip((wq_ref, wk_ref, wv_ref, wr_ref), W_IN_SIZES[:W_IN_SPLIT]):
            w_ref[...] = wt_a_ref[lo:lo + n, :].T
            lo += n
        lane = lax.broadcasted_iota(jnp.int32, (D_MODEL, LANE), 1)
        wa_ref[...] = jnp.where(lane < GLA_RANK, wt_b_ref[0:LANE, :].T, jnp.zeros((), BF16))
        lo = GLA_RANK
        for w_ref, n in zip((wu_ref, wga_ref, wgb_ref), W_IN_SIZES[W_IN_SPLIT + 1:]):
            w_ref[...] = wt_b_ref[lo:lo + n, :].T
            lo += n

    @pl.when(mix_tile % tiles_per_seq == 0)
    def _():
        ubuf[0:POOL_HALO, :] = jnp.zeros((POOL_HALO, POOL_WIDTH), F32)

    @pl.when(jnp.logical_or(step == 0, gla_tile % tiles_per_seq == 0))
    def _():
        s_ref[...] = jnp.zeros_like(s_ref)

    keep_halo = step >= n_tiles - 1
    mix_args = (x_ref, g_ref, wq_ref, wk_ref, wv_ref, wr_ref, wa_ref, wu_ref, wga_ref, wgb_ref, wal_ref, bal_ref,
                sr_ref, z_ref, sga_ref, sgb_ref, ubuf, mix_tile % tiles_per_seq, keep_halo)
    gla_args = (q_s, k_s, v_s, la_s, gn_ref[...], o_ref, s_ref)

    def run(gla_tasks):
        nxt = []
        _interleave(gla_tasks + [_mix_in_task(*mix_args, nxt)])
        q_s[...], k_s[...], v_s[...], la_s[...], span_s[0] = nxt

    wide_ok = span_s[0] < GLA_WIDE_MAX_SPAN

    @pl.when(wide_ok)
    def _():
        run([_gla_wide_task(*gla_args)])

    @pl.when(jnp.logical_not(wide_ok))
    def _():
        _gla_anchored_chunks(*gla_args)
        run([])


def _mix_gla(x, g, wt_a, wt_b, wal, bal, gn, seq):
    t = x.shape[0]
    n_tiles = t // ROW_TILE

    def cur(n):
        return pl.BlockSpec((ROW_TILE, n), lambda i: (jnp.minimum(i, n_tiles - 1), 0))

    prev = pl.BlockSpec((ROW_TILE, GLA_VW), lambda i: (jnp.maximum(i - 1, 0), 0))

    def out(n):
        return jax.ShapeDtypeStruct((t, n), BF16)

    outs, _ = _call(
        functools.partial(_mix_gla_kernel, tiles_per_seq=seq // ROW_TILE, n_tiles=n_tiles),
        (x, g, wt_a, wt_b, wal, bal, gn),
        [cur(D_MODEL), _resident((1, D_MODEL)), _resident(wt_a.shape), _resident(wt_b.shape),
         _resident((LANE, GLA_QK)), _resident((1, GLA_QK)), _resident((1, GLA_DV))],
        [cur(GLA_VW), cur(POOL_WIDTH), cur(D_MODEL), cur(D_MODEL), prev],
        [out(GLA_VW), out(POOL_WIDTH), out(D_MODEL), out(D_MODEL), out(GLA_VW)],
        grid=(n_tiles + 1,),
        scratch_shapes=[pltpu.VMEM((ROW_TILE, GLA_QK), BF16), pltpu.VMEM((ROW_TILE, GLA_QK), BF16),
                        pltpu.VMEM((ROW_TILE, GLA_VW), BF16), pltpu.VMEM((ROW_TILE, GLA_QK), F32),
                        pltpu.VMEM((GLA_HEADS, GLA_DK, GLA_DV), F32),
                        pltpu.VMEM((POOL_HALO + ROW_TILE, POOL_WIDTH), F32),
                        pltpu.SMEM((1,), F32),
                        *(pltpu.VMEM((D_MODEL, LANE if n == GLA_RANK else n), BF16) for n in W_IN_SIZES)],
        name="mix_gla")
    return outs


def _mem_kv_kernel(m_ref, g_ref, wk_ref, wv_ref, k_ref, v_ref):
    m = _rms(m_ref[...], g_ref[...]).astype(BF16)
    k_ref[...] = _dot(m, wk_ref[...]).astype(BF16)
    v_ref[...] = _dot(m, wv_ref[...]).astype(BF16)


def _mem_kv(mem, g, wk, wv, mem_len, make_jobs):
    t = mem.shape[0]
    steps = t // mem_len
    row = pl.BlockSpec((mem_len, D_MODEL), lambda i: (i, 0))
    out = jax.ShapeDtypeStruct((t, D_MODEL), BF16)
    return _call(
        _mem_kv_kernel, (mem, g, wk, wv),
        [row, _resident((1, D_MODEL)), _resident((D_MODEL, D_MODEL)), _resident((D_MODEL, D_MODEL))],
        [row, row], [out, out], make_jobs(steps), grid=(steps,), name="mem_kv")


def _mix_xattn_kernel(x_ref, o_ref, sr_ref, z_ref, sga_ref, sgb_ref, wua_ref, pm_ref, ps_ref, wub_ref, wmo_ref,
                      xg_ref, wq_ref, k_ref, v_ref, wo_ref, y_ref):
    zs = []
    for g in range(POOL_GROUPS):
        cols = slice(g * POOL_GC, (g + 1) * POOL_GC)
        zs.append((_dot(z_ref[:, cols], pm_ref[g]) * ps_ref[:, cols]).astype(BF16))
    ya = _dot(o_ref[...] * sr_ref[...], wua_ref[...])
    yb = _dot(jnp.concatenate(zs, axis=1), wub_ref[...])
    merged = (sga_ref[...].astype(F32) * ya + sgb_ref[...].astype(F32) * yb).astype(BF16)
    x = x_ref[...] + _dot(merged, wmo_ref[...])

    inv_rms = lax.rsqrt(jnp.mean(x * x, axis=-1, keepdims=True) + EPS)
    q = (_dot((x * xg_ref[...]).astype(BF16), wq_ref[...]) * (inv_rms * (XA_HD ** -0.5))).astype(BF16)

    def scores(hd):
        cols = slice(hd * XA_HD, (hd + 1) * XA_HD)
        return lax.dot_general(q[:, cols], k_ref[:, cols], (((1,), (1,)), ((), ())), preferred_element_type=F32)

    outs = []
    s = scores(0)
    for hd in range(XA_HEADS):
        s_next = scores(hd + 1) if hd + 1 < XA_HEADS else None
        e = jnp.exp(s - jnp.max(s, axis=-1, keepdims=True))
        o = _dot(e.astype(BF16), v_ref[:, hd * XA_HD:(hd + 1) * XA_HD])
        outs.append((o / jnp.sum(e, axis=-1, keepdims=True)).astype(BF16))
        s = s_next
    y_ref[...] = x + _dot(jnp.concatenate(outs, axis=1), wo_ref[...])


def _mix_xattn(x, o, sr, z, sga, sgb, wua, pm, ps, wub, wmo, xg, wq, km, vm, wo, seq, mem_len):
    t = x.shape[0]
    steps = t // ROW_TILE
    tiles_per_seq = seq // ROW_TILE

    def row(n):
        return pl.BlockSpec((ROW_TILE, n), lambda i: (i, 0))

    mem = pl.BlockSpec((mem_len, D_MODEL), lambda i: (i // tiles_per_seq, 0))
    (y,), _ = _call(
        _mix_xattn_kernel, (x, o, sr, z, sga, sgb, wua, pm, ps, wub, wmo, xg, wq, km, vm, wo),
        [row(D_MODEL), row(GLA_VW), row(GLA_VW), row(POOL_WIDTH), row(D_MODEL), row(D_MODEL),
         _resident((GLA_VW, D_MODEL)), _resident((POOL_GROUPS, POOL_GC, POOL_GC)),
         _resident((1, POOL_WIDTH)), _resident((POOL_WIDTH, D_MODEL)), _resident((D_MODEL, D_MODEL)),
         _resident((1, D_MODEL)), _resident((D_MODEL, D_MODEL)), mem, mem, _resident((D_MODEL, D_MODEL))],
        [row(D_MODEL)], [jax.ShapeDtypeStruct((t, D_MODEL), F32)],
        grid=(steps,), name="mix_xattn")
    return y


def _layer(x, mem, ffn1_norm, ffn1_w1, ffn1_w3, ffn1_w2, mix_norm, w_in, w_alpha, b_alpha, gla_head_norm,
           w_up_a, pool_mix, pool_scale, w_up_b, w_mix_out, xa_norm, mem_norm, xa_wq, xa_wk, xa_wv, xa_wo,
           ffn2_norm, ffn2_w1, ffn2_w3, ffn2_w2, final_norm, last):
    batch, seq, _ = x.shape
    mem_len = mem.shape[1]
    assert seq % ROW_TILE == 0 and ROW_TILE % GLA_CHUNK == 0 and ROW_TILE % GLA_WIDE == 0

    def vec(a):
        return a.reshape(1, -1).astype(F32)

    def bf(a):
        return a.astype(BF16)

    def casts(*ws):
        return lambda n_steps: tuple(_cast_rows_job(w, n_steps) for w in ws)

    (km, vm), ffn1_w = _mem_kv(mem.reshape(batch * mem_len, D_MODEL), vec(mem_norm), bf(xa_wk), bf(xa_wv), mem_len,
                               casts(ffn1_w1, ffn1_w3, ffn1_w2))
    later = (w_up_a, w_up_b, w_mix_out, xa_wq, xa_wo, ffn2_w1, ffn2_w3, ffn2_w2)
    x1, cast = _ffn(x.reshape(batch * seq, D_MODEL), vec(ffn1_norm), *ffn1_w, vec(final_norm), False,
                    lambda n_steps: (_w_in_job(jnp.swapaxes(w_in, 0, 1), n_steps), *casts(*later)(n_steps)))
    wt_a, wt_b, wua, wub, wmo, xwq, xwo, *ffn2_w = cast
    wal = jnp.pad(bf(w_alpha), ((0, LANE - GLA_RANK), (0, 0)))
    sr, z, sga, sgb, o = _mix_gla(x1, vec(mix_norm), wt_a, wt_b, wal, vec(b_alpha), vec(gla_head_norm), seq)
    x3 = _mix_xattn(x1, o, sr, z, sga, sgb, wua, bf(pool_mix), vec(pool_scale), wub, wmo,
                    vec(xa_norm), xwq, km, vm, xwo, seq, mem_len)
    x4, _ = _ffn(x3, vec(ffn2_norm), *ffn2_w, vec(final_norm), last)
    return x4.reshape(batch, seq, D_MODEL)


def kernel(x, mem, ffn1_norm, ffn1_w1, ffn1_w3, ffn1_w2, mix_norm, w_in, w_alpha, b_alpha, gla_head_norm, w_up_a,
           pool_mix, pool_scale, w_up_b, w_mix_out, xa_norm, mem_norm, xa_wq, xa_wk, xa_wv, xa_wo, ffn2_norm,
           ffn2_w1, ffn2_w3, ffn2_w2, final_norm):
    depth = ffn1_norm.shape[0]
    for l in range(depth):
        last = l == depth - 1
        x = _layer(x, mem, ffn1_norm[l], ffn1_w1[l], ffn1_w3[l], ffn1_w2[l], mix_norm[l], w_in[l], w_alpha[l],
                   b_alpha[l], gla_head_norm[l], w_up_a[l], pool_mix[l], pool_scale[l], w_up_b[l], w_mix_out[l],
                   xa_norm[l], mem_norm[l], xa_wq[l], xa_wk[l], xa_wv[l], xa_wo[l], ffn2_norm[l], ffn2_w1[l],
                   ffn2_w3[l], ffn2_w2[l], final_norm, last)
    return x
```

```python
import functools
from typing import Callable, NamedTuple

import jax
import jax.numpy as jnp
from jax import lax
from jax.experimental import pallas as pl
from jax.experimental.pallas import tpu as pltpu

F32 = jnp.float32
BF16 = jnp.bfloat16

D_MODEL = 1024
D_FF = 2816
EPS = 1e-6
GLA_HEADS = 4
GLA_DK = 128
GLA_DV = 256
GLA_QK = GLA_HEADS * GLA_DK
GLA_VW = GLA_HEADS * GLA_DV
GLA_RANK = 16
GLA_GATE_TEMP = 16.0
GLA_CHUNK = 64
GLA_SUB = 16
GLA_WIDE = 256
GLA_WIDE_MAX_SPAN = 60.0
EXP_TO_ZERO = -1e30
POOL_GROUPS = 4
POOL_GC = 128
POOL_WIDTH = POOL_GROUPS * POOL_GC
POOL_WINDOWS = (2, 4, 8, 16)
POOL_HALO = 16
XA_HEADS = 4
XA_HD = 256

LANE = 128
BF16_ROWS = 16
W_IN_SIZES = (GLA_QK, GLA_QK, GLA_VW, GLA_VW, GLA_RANK, POOL_WIDTH, D_MODEL, D_MODEL)
GATE_COLS = (GLA_VW, D_MODEL, D_MODEL, POOL_WIDTH)
W_IN_SPLIT = 4
ROW_TILE = 512
FFN_ROWS = 1024
FF_SPLITS = (0, 1024, 2048, D_FF)
VMEM_LIMIT = 56 * 1024 * 1024


def _rms(x, g):
    return x * lax.rsqrt(jnp.mean(x * x, axis=-1, keepdims=True) + EPS) * g


def _sigmoid(x):
    return 0.5 * jnp.tanh(0.5 * x) + 0.5


def _dot(a, b):
    return jnp.dot(a, b, preferred_element_type=F32)


def _gate_views(gz_ref):
    views, lo = [], 0
    for n in GATE_COLS:
        views.append(gz_ref.at[:, lo:lo + n])
        lo += n
    return views


def _resident(shape):
    nd = len(shape)
    return pl.BlockSpec(shape, lambda *_: (0,) * nd, pipeline_mode=pl.Buffered(1))


def _params():
    return pltpu.CompilerParams(dimension_semantics=("arbitrary",), vmem_limit_bytes=VMEM_LIMIT)


class _CastJob(NamedTuple):
    inputs: tuple
    in_specs: tuple
    out_specs: tuple
    out_shapes: tuple
    body: Callable


def _cast_rows_job(w, n_steps):
    rows, cols = w.shape
    per_step = -(-rows // n_steps)
    block = next(b for b in range(BF16_ROWS, rows + 1, BF16_ROWS) if rows % b == 0 and b >= per_step)
    n_blocks = rows // block
    spec = pl.BlockSpec((block, cols), lambda i: ((i * n_blocks) // n_steps, 0))

    def body(ins, outs):
        outs[0][...] = ins[0][...].astype(BF16)

    return _CastJob((w,), (spec,), (spec,), (jax.ShapeDtypeStruct((rows, cols), BF16),), body)


def _w_in_job(w_in_t, n_steps):
    cols = w_in_t.shape[1]
    rows_a = sum(W_IN_SIZES[:W_IN_SPLIT])
    rows_b = sum(W_IN_SIZES[W_IN_SPLIT:])
    assert w_in_t.shape[0] == rows_a + rows_b and rows_a % n_steps == 0
    blk_a = rows_a // n_steps
    per_b = -(-rows_b // n_steps)
    blk_b = next(r for r in range(BF16_ROWS, rows_b + 1, BF16_ROWS) if rows_b % r == 0 and r >= per_b)
    last_b = rows_b // blk_b - 1
    assert blk_a % BF16_ROWS == 0
    in_a = pl.BlockSpec((blk_a, cols), lambda i: (i, 0))
    in_b = pl.BlockSpec((pl.Element(blk_b), pl.Element(cols)),
                        lambda i: (pl.multiple_of(rows_a + blk_b * jnp.minimum(i, last_b), BF16_ROWS), 0))
    out_b = pl.BlockSpec((blk_b, cols), lambda i: (jnp.minimum(i, last_b), 0))

    def body(ins, outs):
        for i_ref, o_ref in zip(ins, outs):
            o_ref[...] = i_ref[...].astype(BF16)

    return _CastJob((w_in_t, w_in_t), (in_a, in_b), (in_a, out_b),
                    (jax.ShapeDtypeStruct((rows_a, cols), BF16), jax.ShapeDtypeStruct((rows_b, cols), BF16)), body)


def _call(kernel_fn, args, in_specs, out_specs, out_shape, jobs=(), *, grid, scratch_shapes=(), name):
    n_in, n_out = len(args), len(out_specs)
    job_args = [a for j in jobs for a in j.inputs]
    job_outs = [o for j in jobs for o in j.out_shapes]

    def body(*refs):
        j_in = refs[n_in:n_in + len(job_args)]
        first_out = n_in + len(job_args)
        j_out = refs[first_out + n_out:first_out + n_out + len(job_outs)]
        kernel_fn(*refs[:n_in], *refs[first_out:first_out + n_out], *refs[first_out + n_out + len(job_outs):])
        for j in jobs:
            j.body(j_in[:len(j.inputs)], j_out[:len(j.out_shapes)])
            j_in, j_out = j_in[len(j.inputs):], j_out[len(j.out_shapes):]

    outs = pl.pallas_call(
        body,
        grid=grid,
        in_specs=[*in_specs, *(sp for j in jobs for sp in j.in_specs)],
        out_specs=[*out_specs, *(sp for j in jobs for sp in j.out_specs)],
        out_shape=[*out_shape, *job_outs],
        scratch_shapes=list(scratch_shapes),
        compiler_params=_params(),
        name=name,
    )(*args, *job_args)
    return outs[:n_out], outs[n_out:]


def _ffn_kernel(x_ref, g_ref, w1_ref, w3_ref, w2_ref, fg_ref, o_ref, *, final_norm):
    x = x_ref[...]
    h = _rms(x, g_ref[...]).astype(BF16)
    acc = jnp.zeros_like(x)
    for lo, hi in zip(FF_SPLITS[:-1], FF_SPLITS[1:]):
        a = _dot(h, w1_ref[:, lo:hi])
        b = _dot(h, w3_ref[:, lo:hi])
        act = (a * jax.nn.sigmoid(a) * b).astype(BF16)
        acc = acc + _dot(act, w2_ref[lo:hi, :])
    y = x + 0.5 * acc
    if final_norm:
        y = _rms(y, fg_ref[...])
    o_ref[...] = y


def _ffn(x, g, w1, w3, w2, fg, final_norm, make_jobs=None):
    t = x.shape[0]
    steps = t // FFN_ROWS
    row = pl.BlockSpec((FFN_ROWS, D_MODEL), lambda i: (i, 0))
    (y,), cast = _call(
        functools.partial(_ffn_kernel, final_norm=final_norm),
        (x, g, w1, w3, w2, fg),
        [row, _resident((1, D_MODEL)), _resident((D_MODEL, D_FF)), _resident((D_MODEL, D_FF)),
         _resident((D_FF, D_MODEL)), _resident((1, D_MODEL))],
        [row], [jax.ShapeDtypeStruct((t, D_MODEL), F32)],
        make_jobs(steps) if make_jobs else (),
        grid=(steps,), name="ffn_final" if final_norm else "ffn")
    return y, cast


def _cumsum_rows(mask_bf16, la):
    hi = la.astype(BF16)
    lo = (la - hi.astype(F32)).astype(BF16)
    return _dot(mask_bf16, hi) + _dot(mask_bf16, lo)


def _decay_column(b_last_row):
    return jnp.transpose(jnp.broadcast_to(jnp.exp(b_last_row), (8, GLA_DK)))[:, 0:1]


def _gla_wide_task(q_ref, k_ref, v_ref, la_ref, gn, o_ref, s_ref):
    c = GLA_WIDE
    ri = lax.broadcasted_iota(jnp.int32, (c, c), 0)
    ci = lax.broadcasted_iota(jnp.int32, (c, c), 1)
    causal = ci <= ri
    cum_incl = causal.astype(BF16)
    row_sl = [slice(ic * c, (ic + 1) * c) for ic in range(ROW_TILE // c)]
    b_all = [_dot(cum_incl, la_ref[rows, :].astype(BF16)) for rows in row_sl]
    units = [(ic, hd) for ic in range(len(row_sl)) for hd in range(GLA_HEADS)]

    def factors(ic, hd):
        rows, kc = row_sl[ic], slice(hd * GLA_DK, (hd + 1) * GLA_DK)
        b = b_all[ic][:, kc]
        qg = (q_ref[rows, kc].astype(F32) * jnp.exp(b)).astype(BF16)
        kg = (k_ref[rows, kc].astype(F32) * jnp.exp(-b)).astype(BF16)
        s = lax.dot_general(qg, kg, (((1,), (1,)), ((), ())), preferred_element_type=F32)
        return qg, kg, s, b[c - 1:c, :]

    nxt = factors(*units[0])
    yield
    for n, (ic, hd) in enumerate(units):
        qg, kg, s, b_last = nxt
        if n + 1 < len(units):
            nxt = factors(*units[n + 1])
        rows, vc = row_sl[ic], slice(hd * GLA_DV, (hd + 1) * GLA_DV)
        v = v_ref[rows, vc]
        s_old = s_ref[hd]
        inter = _dot(qg, s_old.astype(BF16))
        upd = lax.dot_general(kg, v, (((0,), (0,)), ((), ())), preferred_element_type=F32)
        p = jnp.where(causal, s, 0.0).astype(BF16)
        o = _dot(p, v) + inter
        s_ref[hd] = (s_old + upd) * _decay_column(b_last)
        o_ref[rows, vc] = _rms(o, gn).astype(BF16)
        yield


def _gla_anchored_chunks(q_ref, k_ref, v_ref, la_ref, gn, o_ref, s_ref):
    c = GLA_CHUNK
    ri = lax.broadcasted_iota(jnp.int32, (c, c), 0)
    ci = lax.broadcasted_iota(jnp.int32, (c, c), 1)
    causal = ci <= ri
    cum_incl = causal.astype(BF16)
    cum_anchor = (ci < (ri // GLA_SUB) * GLA_SUB).astype(BF16)
    krow = lax.broadcasted_iota(jnp.int32, (c, 1), 0)
    kcol = lax.broadcasted_iota(jnp.int32, (GLA_SUB, c), 1)

    def chunk(ic, carry):
        r0 = pl.multiple_of(ic * c, c)
        la = la_ref[pl.ds(r0, c), :]
        b_all = _cumsum_rows(cum_incl, la)
        beta_all = _cumsum_rows(cum_anchor, la)
        for hd in range(GLA_HEADS):
            kc = slice(hd * GLA_DK, (hd + 1) * GLA_DK)
            vc = slice(hd * GLA_DV, (hd + 1) * GLA_DV)
            q = q_ref[pl.ds(r0, c), kc].astype(F32)
            k = k_ref[pl.ds(r0, c), kc].astype(F32)
            v = v_ref[pl.ds(r0, c), vc]
            b = b_all[:, kc]
            beta = beta_all[:, kc]
            qa = (q * jnp.exp(b - beta)).astype(BF16)
            rows = []
            for blk in range(c // GLA_SUB):
                r_lo, r_hi = blk * GLA_SUB, (blk + 1) * GLA_SUB
                e = jnp.where(krow < r_lo, beta[r_lo:r_lo + 1, :] - b, EXP_TO_ZERO)
                ka = (k * jnp.exp(e)).astype(BF16)
                sc = lax.dot_general(qa[r_lo:r_hi, :], ka, (((1,), (1,)), ((), ())), preferred_element_type=F32)
                qb, bb = q[r_lo:r_hi, :], b[r_lo:r_hi, :]
                for j in range(r_lo, r_hi):
                    w = jnp.exp(jnp.minimum(bb - b[j:j + 1, :], 0.0))
                    sc = jnp.where(kcol == j, jnp.sum(qb * k[j:j + 1, :] * w, axis=1, keepdims=True), sc)
                rows.append(sc)
            p = jnp.where(causal, jnp.concatenate(rows, axis=0), 0.0).astype(BF16)
            s_old = s_ref[hd]
            o = _dot(p, v) + _dot((q * jnp.exp(b)).astype(BF16), s_old.astype(BF16))
            b_last = b[c - 1:c, :]
            k_dec = (k * jnp.exp(b_last - b)).astype(BF16)
            upd = lax.dot_general(k_dec, v, (((0,), (0,)), ((), ())), preferred_element_type=F32)
            s_ref[hd] = s_old * _decay_column(b_last) + upd
            o_ref[pl.ds(r0, c), vc] = _rms(o, gn).astype(BF16)
        return carry

    lax.fori_loop(0, ROW_TILE // c, chunk, 0)


def _interleave(tasks):
    tasks = list(tasks)
    while tasks:
        for t in list(tasks):
            try:
                next(t)
            except StopIteration:
                tasks.remove(t)


def _mix_in_task(x_ref, g_ref, wq_ref, wk_ref, wv_ref, wr_ref, wa_ref, wu_ref, wga_ref, wgb_ref, wal_ref, bal_ref,
                 sr_ref, z_ref, sga_ref, sgb_ref, ubuf, tile_in_seq, keep_halo, gla_inputs):
    h = _rms(x_ref[...], g_ref[...]).astype(BF16)
    a_code = _dot(h, wa_ref[...]).astype(BF16)
    yield
    u = _dot(h, wu_ref[...])
    yield
    r = _dot(h, wr_ref[...])
    sr_ref[...] = (r * _sigmoid(r)).astype(BF16)
    yield
    zg = _dot(a_code, wal_ref[...]) + bal_ref[...]
    yield
    sga_ref[...] = _sigmoid(_dot(h, wga_ref[...])).astype(BF16)
    yield
    sgb_ref[...] = _sigmoid(_dot(h, wgb_ref[...])).astype(BF16)
    yield

    ubuf[POOL_HALO:POOL_HALO + ROW_TILE, :] = u
    pos = lax.broadcasted_iota(jnp.int32, (ROW_TILE, 1), 0) + tile_in_seq * ROW_TILE
    for g, w in enumerate(POOL_WINDOWS):
        cols = slice(g * POOL_GC, (g + 1) * POOL_GC)
        s = ubuf[:, cols]
        sh = 1
        while sh < w:
            s = s + pltpu.roll(s, sh, 0)
            sh *= 2
        cnt = jnp.minimum(pos + 1, w).astype(F32)
        z_ref[:, cols] = (s[POOL_HALO:, :] / cnt - u[:, cols]).astype(BF16)
    ubuf[0:POOL_HALO, :] = jnp.where(keep_halo, ubuf[0:POOL_HALO, :], ubuf[ROW_TILE:ROW_TILE + POOL_HALO, :])

    q = (_dot(h, wq_ref[...]) * (GLA_DK ** -0.5)).astype(BF16)
    yield
    v = _dot(h, wv_ref[...]).astype(BF16)
    yield
    k = _dot(h, wk_ref[...]).astype(BF16)
    la = (jnp.minimum(zg, 0.0) - jnp.log(1.0 + jnp.exp(-jnp.abs(zg)))) * (1.0 / GLA_GATE_TEMP)
    span = None
    for r0 in range(0, ROW_TILE, GLA_WIDE):
        sp = jnp.max(-jnp.sum(la[r0:r0 + GLA_WIDE, :], axis=0, keepdims=True))
        span = sp if span is None else jnp.maximum(span, sp)
    gla_inputs.extend((q, k, v, la, span))


def _mix_gla_kernel(x_ref, g_ref, wt_a_ref, wt_b_ref, w_alpha_ref, bal_ref, gn_ref,
                    gz_ref, o_ref,
                    q_s, k_s, v_s, la_s, s_ref, ubuf, span_s,
                    wq_ref, wk_ref, wv_ref, wr_ref, wa_ref, wu_ref, wga_ref, wgb_ref, wal_ref,
                    *, tiles_per_seq, n_tiles):
    step = pl.program_id(0)
    mix_tile = jnp.minimum(step, n_tiles - 1)
    gla_tile = step - 1

    @pl.when(step == 0)
    def _():
        q_s[...] = jnp.zeros_like(q_s)
        k_s[...] = jnp.zeros_like(k_s)
        v_s[...] = jnp.zeros_like(v_s)
        la_s[...] = jnp.zeros_like(la_s)
        span_s[0] = 0.0
        lo = 0
        for w_ref, n in zip((wq_ref, wk_ref, wv_ref, wr_ref), W_IN_SIZES[:W_IN_SPLIT]):
            w_ref[...] = wt_a_ref[lo:lo + n, :].T
            lo += n
        lane = lax.broadcasted_iota(jnp.int32, (D_MODEL, LANE), 1)
        wa_ref[...] = jnp.where(lane < GLA_RANK, wt_b_ref[0:LANE, :].T, jnp.zeros((), BF16))
        lo = GLA_RANK
        for w_ref, n in zip((wu_ref, wga_ref, wgb_ref), W_IN_SIZES[W_IN_SPLIT + 1:]):
            w_ref[...] = wt_b_ref[lo:lo + n, :].T
            lo += n
        wal_ref[...] = jnp.zeros_like(wal_ref)
        wal_ref[0:GLA_RANK, :] = w_alpha_ref[...].astype(BF16)

    @pl.when(mix_tile % tiles_per_seq == 0)
    def _():
        ubuf[0:POOL_HALO, :] = jnp.zeros((POOL_HALO, POOL_WIDTH), F32)

    @pl.when(jnp.logical_or(step == 0, gla_tile % tiles_per_seq == 0))
    def _():
        s_ref[...] = jnp.zeros_like(s_ref)

    keep_halo = step >= n_tiles - 1
    sr_ref, sga_ref, sgb_ref, z_ref = _gate_views(gz_ref)
    mix_args = (x_ref, g_ref, wq_ref, wk_ref, wv_ref, wr_ref, wa_ref, wu_ref, wga_ref, wgb_ref, wal_ref, bal_ref,
                sr_ref, z_ref, sga_ref, sgb_ref, ubuf, mix_tile % tiles_per_seq, keep_halo)
    gla_args = (q_s, k_s, v_s, la_s, gn_ref[...], o_ref, s_ref)

    def run(gla_tasks):
        nxt = []
        _interleave(gla_tasks + [_mix_in_task(*mix_args, nxt)])
        q_s[...], k_s[...], v_s[...], la_s[...], span_s[0] = nxt

    wide_ok = span_s[0] < GLA_WIDE_MAX_SPAN

    @pl.when(wide_ok)
    def _():
        run([_gla_wide_task(*gla_args)])

    @pl.when(jnp.logical_not(wide_ok))
    def _():
        _gla_anchored_chunks(*gla_args)
        run([])


def _mix_gla(x, g, wt_a, wt_b, w_alpha, bal, gn, seq):
    t = x.shape[0]
    n_tiles = t // ROW_TILE

    def cur(n):
        return pl.BlockSpec((ROW_TILE, n), lambda i: (jnp.minimum(i, n_tiles - 1), 0))

    prev = pl.BlockSpec((ROW_TILE, GLA_VW), lambda i: (jnp.maximum(i - 1, 0), 0))

    def out(n):
        return jax.ShapeDtypeStruct((t, n), BF16)

    outs, _ = _call(
        functools.partial(_mix_gla_kernel, tiles_per_seq=seq // ROW_TILE, n_tiles=n_tiles),
        (x, g, wt_a, wt_b, w_alpha, bal, gn),
        [cur(D_MODEL), _resident((1, D_MODEL)), _resident(wt_a.shape), _resident(wt_b.shape),
         _resident((GLA_RANK, GLA_QK)), _resident((1, GLA_QK)), _resident((1, GLA_DV))],
        [cur(sum(GATE_COLS)), prev],
        [out(sum(GATE_COLS)), out(GLA_VW)],
        grid=(n_tiles + 1,),
        scratch_shapes=[pltpu.VMEM((ROW_TILE, GLA_QK), BF16), pltpu.VMEM((ROW_TILE, GLA_QK), BF16),
                        pltpu.VMEM((ROW_TILE, GLA_VW), BF16), pltpu.VMEM((ROW_TILE, GLA_QK), F32),
                        pltpu.VMEM((GLA_HEADS, GLA_DK, GLA_DV), F32),
                        pltpu.VMEM((POOL_HALO + ROW_TILE, POOL_WIDTH), F32),
                        pltpu.SMEM((1,), F32),
                        *(pltpu.VMEM((D_MODEL, LANE if n == GLA_RANK else n), BF16) for n in W_IN_SIZES),
                        pltpu.VMEM((LANE, GLA_QK), BF16)],
        name="mix_gla")
    return outs


def _mem_kv_kernel(m_ref, g_ref, wk_ref, wv_ref, k_ref, v_ref, wk_s, wv_s):
    @pl.when(pl.program_id(0) == 0)
    def _():
        wk_s[...] = wk_ref[...].astype(BF16)
        wv_s[...] = wv_ref[...].astype(BF16)

    m = _rms(m_ref[...], g_ref[...]).astype(BF16)
    k_ref[...] = _dot(m, wk_s[...]).astype(BF16)
    v_ref[...] = _dot(m, wv_s[...]).astype(BF16)


def _mem_kv(mem, g, wk, wv, mem_len, make_jobs):
    t = mem.shape[0]
    steps = t // mem_len
    row = pl.BlockSpec((mem_len, D_MODEL), lambda i: (i, 0))
    out = jax.ShapeDtypeStruct((t, D_MODEL), BF16)
    return _call(
        _mem_kv_kernel, (mem, g, wk, wv),
        [row, _resident((1, D_MODEL)), _resident((D_MODEL, D_MODEL)), _resident((D_MODEL, D_MODEL))],
        [row, row], [out, out], make_jobs(steps), grid=(steps,),
        scratch_shapes=[pltpu.VMEM((D_MODEL, D_MODEL), BF16)] * 2, name="mem_kv")


def _mix_xattn_kernel(x_ref, o_ref, gz_ref, wua_ref, pm_ref, ps_ref, wub_ref, wmo_ref,
                      xg_ref, wq_ref, k_ref, v_ref, wo_ref, y_ref):
    sr_ref, sga_ref, sgb_ref, z_ref = _gate_views(gz_ref)
    zs = []
    for g in range(POOL_GROUPS):
        cols = slice(g * POOL_GC, (g + 1) * POOL_GC)
        zs.append((_dot(z_ref[:, cols], pm_ref[g].astype(BF16)) * ps_ref[:, cols]).astype(BF16))
    ya = _dot(o_ref[...] * sr_ref[...], wua_ref[...])
    yb = _dot(jnp.concatenate(zs, axis=1), wub_ref[...])
    merged = (sga_ref[...].astype(F32) * ya + sgb_ref[...].astype(F32) * yb).astype(BF16)
    x = x_ref[...] + _dot(merged, wmo_ref[...])

    inv_rms = lax.rsqrt(jnp.mean(x * x, axis=-1, keepdims=True) + EPS)
    q = (_dot((x * xg_ref[...]).astype(BF16), wq_ref[...]) * (inv_rms * (XA_HD ** -0.5))).astype(BF16)

    def scores(hd):
        cols = slice(hd * XA_HD, (hd + 1) * XA_HD)
        return lax.dot_general(q[:, cols], k_ref[:, cols], (((1,), (1,)), ((), ())), preferred_element_type=F32)

    outs = []
    s = scores(0)
    for hd in range(XA_HEADS):
        s_next = scores(hd + 1) if hd + 1 < XA_HEADS else None
        e = jnp.exp(s - jnp.max(s, axis=-1, keepdims=True))
        o = _dot(e.astype(BF16), v_ref[:, hd * XA_HD:(hd + 1) * XA_HD])
        outs.append((o / jnp.sum(e, axis=-1, keepdims=True)).astype(BF16))
        s = s_next
    y_ref[...] = x + _dot(jnp.concatenate(outs, axis=1), wo_ref[...])


def _mix_xattn(x, o, gz, wua, pm, ps, wub, wmo, xg, wq, km, vm, wo, seq, mem_len):
    t = x.shape[0]
    steps = t // ROW_TILE
    tiles_per_seq = seq // ROW_TILE

    def row(n):
        return pl.BlockSpec((ROW_TILE, n), lambda i: (i, 0))

    mem = pl.BlockSpec((mem_len, D_MODEL), lambda i: (i // tiles_per_seq, 0))
    (y,), _ = _call(
        _mix_xattn_kernel, (x, o, gz, wua, pm, ps, wub, wmo, xg, wq, km, vm, wo),
        [row(D_MODEL), row(GLA_VW), row(sum(GATE_COLS)),
         _resident((GLA_VW, D_MODEL)), _resident((POOL_GROUPS, POOL_GC, POOL_GC)),
         _resident((1, POOL_WIDTH)), _resident((POOL_WIDTH, D_MODEL)), _resident((D_MODEL, D_MODEL)),
         _resident((1, D_MODEL)), _resident((D_MODEL, D_MODEL)), mem, mem, _resident((D_MODEL, D_MODEL))],
        [row(D_MODEL)], [jax.ShapeDtypeStruct((t, D_MODEL), F32)],
        grid=(steps,), name="mix_xattn")
    return y


def _layer(x, mem, ffn1_norm, ffn1_w1, ffn1_w3, ffn1_w2, mix_norm, w_in, w_alpha, b_alpha, gla_head_norm,
           w_up_a, pool_mix, pool_scale, w_up_b, w_mix_out, xa_norm, mem_norm, xa_wq, xa_wk, xa_wv, xa_wo,
           ffn2_norm, ffn2_w1, ffn2_w3, ffn2_w2, final_norm, last):
    batch, seq, _ = x.shape
    mem_len = mem.shape[1]
    assert seq % ROW_TILE == 0 and ROW_TILE % GLA_CHUNK == 0 and ROW_TILE % GLA_WIDE == 0

    def vec(a):
        return a.reshape(1, -1).astype(F32)

    def casts(*ws):
        return lambda n_steps: tuple(_cast_rows_job(w, n_steps) for w in ws)

    (km, vm), ffn1_w = _mem_kv(mem.reshape(batch * mem_len, D_MODEL), vec(mem_norm), xa_wk, xa_wv, mem_len,
                               casts(ffn1_w1, ffn1_w3, ffn1_w2))
    later = (w_up_a, w_up_b, w_mix_out, xa_wq, xa_wo, ffn2_w1, ffn2_w3, ffn2_w2)
    x1, cast = _ffn(x.reshape(batch * seq, D_MODEL), vec(ffn1_norm), *ffn1_w, vec(final_norm), False,
                    lambda n_steps: (_w_in_job(jnp.swapaxes(w_in, 0, 1), n_steps), *casts(*later)(n_steps)))
    wt_a, wt_b, wua, wub, wmo, xwq, xwo, *ffn2_w = cast
    gz, o = _mix_gla(x1, vec(mix_norm), wt_a, wt_b, w_alpha, vec(b_alpha), vec(gla_head_norm), seq)
    x3 = _mix_xattn(x1, o, gz, wua, pool_mix, vec(pool_scale), wub, wmo, vec(xa_norm), xwq, km, vm, xwo, seq, mem_len)
    x4, _ = _ffn(x3, vec(ffn2_norm), *ffn2_w, vec(final_norm), last)
    return x4.reshape(batch, seq, D_MODEL)


def kernel(x, mem, ffn1_norm, ffn1_w1, ffn1_w3, ffn1_w2, mix_norm, w_in, w_alpha, b_alpha, gla_head_norm, w_up_a,
           pool_mix, pool_scale, w_up_b, w_mix_out, xa_norm, mem_norm, xa_wq, xa_wk, xa_wv, xa_wo, ffn2_norm,
           ffn2_w1, ffn2_w3, ffn2_w2, final_norm):
    depth = ffn1_norm.shape[0]
    for l in range(depth):
        last = l == depth - 1
        x = _layer(x, mem, ffn1_norm[l], ffn1_w1[l], ffn1_w3[l], ffn1_w2[l], mix_norm[l], w_in[l], w_alpha[l],
                   b_alpha[l], gla_head_norm[l], w_up_a[l], pool_mix[l], pool_scale[l], w_up_b[l], w_mix_out[l],
                   xa_norm[l], mem_norm[l], xa_wq[l], xa_wk[l], xa_wv[l], xa_wo[l], ffn2_norm[l], ffn2_w1[l],
                   ffn2_w3[l], ffn2_w2[l], final_norm, last)
    return x
```

```python
import functools
from typing import Callable, NamedTuple

import jax
import jax.numpy as jnp
from jax import lax
from jax.experimental import pallas as pl
from jax.experimental.pallas import tpu as pltpu

F32 = jnp.float32
BF16 = jnp.bfloat16

D_MODEL = 1024
D_FF = 2816
EPS = 1e-6
GLA_HEADS = 4
GLA_DK = 128
GLA_DV = 256
GLA_QK = GLA_HEADS * GLA_DK
GLA_VW = GLA_HEADS * GLA_DV
GLA_RANK = 16
GLA_GATE_TEMP = 16.0
GLA_CHUNK = 64
GLA_SUB = 16
GLA_WIDE = 256
GLA_WIDE_MAX_SPAN = 60.0
EXP_TO_ZERO = -1e30
POOL_GROUPS = 4
POOL_GC = 128
POOL_WIDTH = POOL_GROUPS * POOL_GC
POOL_WINDOWS = (2, 4, 8, 16)
POOL_HALO = 16
XA_HEADS = 4
XA_HD = 256

LANE = 128
BF16_ROWS = 16
W_IN_SIZES = (GLA_QK, GLA_QK, GLA_VW, GLA_VW, GLA_RANK, POOL_WIDTH, D_MODEL, D_MODEL)
GATE_COLS = (GLA_VW, D_MODEL, D_MODEL, POOL_WIDTH)
W_IN_SPLIT = 4
ROW_TILE = 512
FFN_ROWS = 1024
FF_SPLITS = (0, 1024, 2048, D_FF)
VMEM_LIMIT = 56 * 1024 * 1024
VEC_NAMES = ("ffn1_norm", "mix_norm", "b_alpha", "gla_head_norm", "pool_scale", "xa_norm", "mem_norm", "ffn2_norm",
             "final_norm")
VEC_ROWS = 16


def _rms(x, g):
    return x * lax.rsqrt(jnp.mean(x * x, axis=-1, keepdims=True) + EPS) * g


def _sigmoid(x):
    return 0.5 * jnp.tanh(0.5 * x) + 0.5


def _dot(a, b):
    return jnp.dot(a, b, preferred_element_type=F32)


def _vec(p_ref, name, n=D_MODEL):
    row = VEC_NAMES.index(name)
    return p_ref[row:row + 1, 0:n]


def _pack_vectors(**vectors):
    rows = [jnp.pad(vectors[name].reshape(-1).astype(F32), (0, D_MODEL - vectors[name].size)) for name in VEC_NAMES]
    return jnp.pad(jnp.stack(rows), ((0, VEC_ROWS - len(rows)), (0, 0)))


def _gate_views(gz_ref):
    views, lo = [], 0
    for n in GATE_COLS:
        views.append(gz_ref.at[:, lo:lo + n])
        lo += n
    return views


def _resident(shape):
    nd = len(shape)
    return pl.BlockSpec(shape, lambda *_: (0,) * nd, pipeline_mode=pl.Buffered(1))


def _params():
    return pltpu.CompilerParams(dimension_semantics=("arbitrary",), vmem_limit_bytes=VMEM_LIMIT)


class _CastJob(NamedTuple):
    inputs: tuple
    in_specs: tuple
    out_specs: tuple
    out_shapes: tuple
    body: Callable


def _cast_rows_job(w, n_steps):
    rows, cols = w.shape
    per_step = -(-rows // n_steps)
    block = next(b for b in range(BF16_ROWS, rows + 1, BF16_ROWS) if rows % b == 0 and b >= per_step)
    n_blocks = rows // block
    spec = pl.BlockSpec((block, cols), lambda i: ((i * n_blocks) // n_steps, 0))

    def body(ins, outs):
        outs[0][...] = ins[0][...].astype(BF16)

    return _CastJob((w,), (spec,), (spec,), (jax.ShapeDtypeStruct((rows, cols), BF16),), body)


def _w_in_job(w_in_t, n_steps):
    cols = w_in_t.shape[1]
    rows_a = sum(W_IN_SIZES[:W_IN_SPLIT])
    rows_b = sum(W_IN_SIZES[W_IN_SPLIT:])
    assert w_in_t.shape[0] == rows_a + rows_b and rows_a % n_steps == 0
    blk_a = rows_a // n_steps
    per_b = -(-rows_b // n_steps)
    blk_b = next(r for r in range(BF16_ROWS, rows_b + 1, BF16_ROWS) if rows_b % r == 0 and r >= per_b)
    last_b = rows_b // blk_b - 1
    assert blk_a % BF16_ROWS == 0
    in_a = pl.BlockSpec((blk_a, cols), lambda i: (i, 0))
    in_b = pl.BlockSpec((pl.Element(blk_b), pl.Element(cols)),
                        lambda i: (pl.multiple_of(rows_a + blk_b * jnp.minimum(i, last_b), BF16_ROWS), 0))
    out_b = pl.BlockSpec((blk_b, cols), lambda i: (jnp.minimum(i, last_b), 0))

    def body(ins, outs):
        for i_ref, o_ref in zip(ins, outs):
            o_ref[...] = i_ref[...].astype(BF16)

    return _CastJob((w_in_t, w_in_t), (in_a, in_b), (in_a, out_b),
                    (jax.ShapeDtypeStruct((rows_a, cols), BF16), jax.ShapeDtypeStruct((rows_b, cols), BF16)), body)


def _call(kernel_fn, args, in_specs, out_specs, out_shape, jobs=(), *, grid, scratch_shapes=(), name):
    n_in, n_out = len(args), len(out_specs)
    job_args = [a for j in jobs for a in j.inputs]
    job_outs = [o for j in jobs for o in j.out_shapes]

    def body(*refs):
        j_in = refs[n_in:n_in + len(job_args)]
        first_out = n_in + len(job_args)
        j_out = refs[first_out + n_out:first_out + n_out + len(job_outs)]
        kernel_fn(*refs[:n_in], *refs[first_out:first_out + n_out], *refs[first_out + n_out + len(job_outs):])
        for j in jobs:
            j.body(j_in[:len(j.inputs)], j_out[:len(j.out_shapes)])
            j_in, j_out = j_in[len(j.inputs):], j_out[len(j.out_shapes):]

    outs = pl.pallas_call(
        body,
        grid=grid,
        in_specs=[*in_specs, *(sp for j in jobs for sp in j.in_specs)],
        out_specs=[*out_specs, *(sp for j in jobs for sp in j.out_specs)],
        out_shape=[*out_shape, *job_outs],
        scratch_shapes=list(scratch_shapes),
        compiler_params=_params(),
        name=name,
    )(*args, *job_args)
    return outs[:n_out], outs[n_out:]


def _ffn_kernel(x_ref, p_ref, w1_ref, w3_ref, w2_ref, o_ref, *, norm, final_norm):
    x = x_ref[...]
    h = _rms(x, _vec(p_ref, norm)).astype(BF16)
    acc = jnp.zeros_like(x)
    for lo, hi in zip(FF_SPLITS[:-1], FF_SPLITS[1:]):
        a = _dot(h, w1_ref[:, lo:hi])
        b = _dot(h, w3_ref[:, lo:hi])
        act = (a * jax.nn.sigmoid(a) * b).astype(BF16)
        acc = acc + _dot(act, w2_ref[lo:hi, :])
    y = x + 0.5 * acc
    if final_norm:
        y = _rms(y, _vec(p_ref, "final_norm"))
    o_ref[...] = y


def _ffn(x, vecs, norm, w1, w3, w2, final_norm, make_jobs=None):
    t = x.shape[0]
    steps = t // FFN_ROWS
    row = pl.BlockSpec((FFN_ROWS, D_MODEL), lambda i: (i, 0))
    (y,), cast = _call(
        functools.partial(_ffn_kernel, norm=norm, final_norm=final_norm),
        (x, vecs, w1, w3, w2),
        [row, _resident(vecs.shape), _resident((D_MODEL, D_FF)), _resident((D_MODEL, D_FF)),
         _resident((D_FF, D_MODEL))],
        [row], [jax.ShapeDtypeStruct((t, D_MODEL), F32)],
        make_jobs(steps) if make_jobs else (),
        grid=(steps,), name="ffn_final" if final_norm else "ffn")
    return y, cast


def _cumsum_rows(mask_bf16, la):
    hi = la.astype(BF16)
    lo = (la - hi.astype(F32)).astype(BF16)
    return _dot(mask_bf16, hi) + _dot(mask_bf16, lo)


def _decay_column(b_last_row):
    return jnp.transpose(jnp.broadcast_to(jnp.exp(b_last_row), (8, GLA_DK)))[:, 0:1]


def _gla_wide_task(q_ref, k_ref, v_ref, la_ref, gn, o_ref, s_ref):
    c = GLA_WIDE
    ri = lax.broadcasted_iota(jnp.int32, (c, c), 0)
    ci = lax.broadcasted_iota(jnp.int32, (c, c), 1)
    causal = ci <= ri
    cum_incl = causal.astype(BF16)
    row_sl = [slice(ic * c, (ic + 1) * c) for ic in range(ROW_TILE // c)]
    b_all = [_dot(cum_incl, la_ref[rows, :].astype(BF16)) for rows in row_sl]
    units = [(ic, hd) for ic in range(len(row_sl)) for hd in range(GLA_HEADS)]

    def factors(ic, hd):
        rows, kc = row_sl[ic], slice(hd * GLA_DK, (hd + 1) * GLA_DK)
        b = b_all[ic][:, kc]
        qg = (q_ref[rows, kc].astype(F32) * jnp.exp(b)).astype(BF16)
        kg = (k_ref[rows, kc].astype(F32) * jnp.exp(-b)).astype(BF16)
        s = lax.dot_general(qg, kg, (((1,), (1,)), ((), ())), preferred_element_type=F32)
        return qg, kg, s, b[c - 1:c, :]

    nxt = factors(*units[0])
    yield
    for n, (ic, hd) in enumerate(units):
        qg, kg, s, b_last = nxt
        if n + 1 < len(units):
            nxt = factors(*units[n + 1])
        rows, vc = row_sl[ic], slice(hd * GLA_DV, (hd + 1) * GLA_DV)
        v = v_ref[rows, vc]
        s_old = s_ref[hd]
        inter = _dot(qg, s_old.astype(BF16))
        upd = lax.dot_general(kg, v, (((0,), (0,)), ((), ())), preferred_element_type=F32)
        p = jnp.where(causal, s, 0.0).astype(BF16)
        o = _dot(p, v) + inter
        s_ref[hd] = (s_old + upd) * _decay_column(b_last)
        o_ref[rows, vc] = _rms(o, gn).astype(BF16)
        yield


def _gla_anchored_chunks(q_ref, k_ref, v_ref, la_ref, gn, o_ref, s_ref):
    c = GLA_CHUNK
    ri = lax.broadcasted_iota(jnp.int32, (c, c), 0)
    ci = lax.broadcasted_iota(jnp.int32, (c, c), 1)
    causal = ci <= ri
    cum_incl = causal.astype(BF16)
    cum_anchor = (ci < (ri // GLA_SUB) * GLA_SUB).astype(BF16)
    krow = lax.broadcasted_iota(jnp.int32, (c, 1), 0)
    kcol = lax.broadcasted_iota(jnp.int32, (GLA_SUB, c), 1)

    def chunk(ic, carry):
        r0 = pl.multiple_of(ic * c, c)
        la = la_ref[pl.ds(r0, c), :]
        b_all = _cumsum_rows(cum_incl, la)
        beta_all = _cumsum_rows(cum_anchor, la)
        for hd in range(GLA_HEADS):
            kc = slice(hd * GLA_DK, (hd + 1) * GLA_DK)
            vc = slice(hd * GLA_DV, (hd + 1) * GLA_DV)
            q = q_ref[pl.ds(r0, c), kc].astype(F32)
            k = k_ref[pl.ds(r0, c), kc].astype(F32)
            v = v_ref[pl.ds(r0, c), vc]
            b = b_all[:, kc]
            beta = beta_all[:, kc]
            qa = (q * jnp.exp(b - beta)).astype(BF16)
            rows = []
            for blk in range(c // GLA_SUB):
                r_lo, r_hi = blk * GLA_SUB, (blk + 1) * GLA_SUB
                e = jnp.where(krow < r_lo, beta[r_lo:r_lo + 1, :] - b, EXP_TO_ZERO)
                ka = (k * jnp.exp(e)).astype(BF16)
                sc = lax.dot_general(qa[r_lo:r_hi, :], ka, (((1,), (1,)), ((), ())), preferred_element_type=F32)
                qb, bb = q[r_lo:r_hi, :], b[r_lo:r_hi, :]
                for j in range(r_lo, r_hi):
                    w = jnp.exp(jnp.minimum(bb - b[j:j + 1, :], 0.0))
                    sc = jnp.where(kcol == j, jnp.sum(qb * k[j:j + 1, :] * w, axis=1, keepdims=True), sc)
                rows.append(sc)
            p = jnp.where(causal, jnp.concatenate(rows, axis=0), 0.0).astype(BF16)
            s_old = s_ref[hd]
            o = _dot(p, v) + _dot((q * jnp.exp(b)).astype(BF16), s_old.astype(BF16))
            b_last = b[c - 1:c, :]
            k_dec = (k * jnp.exp(b_last - b)).astype(BF16)
            upd = lax.dot_general(k_dec, v, (((0,), (0,)), ((), ())), preferred_element_type=F32)
            s_ref[hd] = s_old * _decay_column(b_last) + upd
            o_ref[pl.ds(r0, c), vc] = _rms(o, gn).astype(BF16)
        return carry

    lax.fori_loop(0, ROW_TILE // c, chunk, 0)


def _interleave(tasks):
    tasks = list(tasks)
    while tasks:
        for t in list(tasks):
            try:
                next(t)
            except StopIteration:
                tasks.remove(t)


def _mix_in_task(x_ref, p_ref, wq_ref, wk_ref, wv_ref, wr_ref, wa_ref, wu_ref, wga_ref, wgb_ref, wal_ref,
                 sr_ref, z_ref, sga_ref, sgb_ref, ubuf, tile_in_seq, keep_halo, gla_inputs):
    h = _rms(x_ref[...], _vec(p_ref, "mix_norm")).astype(BF16)
    a_code = _dot(h, wa_ref[...]).astype(BF16)
    yield
    u = _dot(h, wu_ref[...])
    yield
    r = _dot(h, wr_ref[...])
    sr_ref[...] = (r * _sigmoid(r)).astype(BF16)
    yield
    zg = _dot(a_code, wal_ref[...]) + _vec(p_ref, "b_alpha", GLA_QK)
    yield
    sga_ref[...] = _sigmoid(_dot(h, wga_ref[...])).astype(BF16)
    yield
    sgb_ref[...] = _sigmoid(_dot(h, wgb_ref[...])).astype(BF16)
    yield

    ubuf[POOL_HALO:POOL_HALO + ROW_TILE, :] = u
    pos = lax.broadcasted_iota(jnp.int32, (ROW_TILE, 1), 0) + tile_in_seq * ROW_TILE
    for g, w in enumerate(POOL_WINDOWS):
        cols = slice(g * POOL_GC, (g + 1) * POOL_GC)
        s = ubuf[:, cols]
        sh = 1
        while sh < w:
            s = s + pltpu.roll(s, sh, 0)
            sh *= 2
        cnt = jnp.minimum(pos + 1, w).astype(F32)
        z_ref[:, cols] = (s[POOL_HALO:, :] / cnt - u[:, cols]).astype(BF16)
    ubuf[0:POOL_HALO, :] = jnp.where(keep_halo, ubuf[0:POOL_HALO, :], ubuf[ROW_TILE:ROW_TILE + POOL_HALO, :])

    q = (_dot(h, wq_ref[...]) * (GLA_DK ** -0.5)).astype(BF16)
    yield
    v = _dot(h, wv_ref[...]).astype(BF16)
    yield
    k = _dot(h, wk_ref[...]).astype(BF16)
    la = (jnp.minimum(zg, 0.0) - jnp.log(1.0 + jnp.exp(-jnp.abs(zg)))) * (1.0 / GLA_GATE_TEMP)
    span = None
    for r0 in range(0, ROW_TILE, GLA_WIDE):
        sp = jnp.max(-jnp.sum(la[r0:r0 + GLA_WIDE, :], axis=0, keepdims=True))
        span = sp if span is None else jnp.maximum(span, sp)
    gla_inputs.extend((q, k, v, la, span))


def _mix_gla_kernel(x_ref, p_ref, wt_a_ref, wt_b_ref, w_alpha_ref,
                    gz_ref, o_ref,
                    q_s, k_s, v_s, la_s, s_ref, ubuf, span_s,
                    wq_ref, wk_ref, wv_ref, wr_ref, wa_ref, wu_ref, wga_ref, wgb_ref, wal_ref,
                    *, tiles_per_seq, n_tiles):
    step = pl.program_id(0)
    mix_tile = jnp.minimum(step, n_tiles - 1)
    gla_tile = step - 1

    @pl.when(step == 0)
    def _():
        q_s[...] = jnp.zeros_like(q_s)
        k_s[...] = jnp.zeros_like(k_s)
        v_s[...] = jnp.zeros_like(v_s)
        la_s[...] = jnp.zeros_like(la_s)
        span_s[0] = 0.0
        lo = 0
        for w_ref, n in zip((wq_ref, wk_ref, wv_ref, wr_ref), W_IN_SIZES[:W_IN_SPLIT]):
            w_ref[...] = wt_a_ref[lo:lo + n, :].T
            lo += n
        lane = lax.broadcasted_iota(jnp.int32, (D_MODEL, LANE), 1)
        wa_ref[...] = jnp.where(lane < GLA_RANK, wt_b_ref[0:LANE, :].T, jnp.zeros((), BF16))
        lo = GLA_RANK
        for w_ref, n in zip((wu_ref, wga_ref, wgb_ref), W_IN_SIZES[W_IN_SPLIT + 1:]):
            w_ref[...] = wt_b_ref[lo:lo + n, :].T
            lo += n
        wal_ref[...] = jnp.zeros_like(wal_ref)
        wal_ref[0:GLA_RANK, :] = w_alpha_ref[...].astype(BF16)

    @pl.when(mix_tile % tiles_per_seq == 0)
    def _():
        ubuf[0:POOL_HALO, :] = jnp.zeros((POOL_HALO, POOL_WIDTH), F32)

    @pl.when(jnp.logical_or(step == 0, gla_tile % tiles_per_seq == 0))
    def _():
        s_ref[...] = jnp.zeros_like(s_ref)

    keep_halo = step >= n_tiles - 1
    sr_ref, sga_ref, sgb_ref, z_ref = _gate_views(gz_ref)
    mix_args = (x_ref, p_ref, wq_ref, wk_ref, wv_ref, wr_ref, wa_ref, wu_ref, wga_ref, wgb_ref, wal_ref,
                sr_ref, z_ref, sga_ref, sgb_ref, ubuf, mix_tile % tiles_per_seq, keep_halo)
    gla_args = (q_s, k_s, v_s, la_s, _vec(p_ref, "gla_head_norm", GLA_DV), o_ref, s_ref)

    def run(gla_tasks):
        nxt = []
        _interleave(gla_tasks + [_mix_in_task(*mix_args, nxt)])
        q_s[...], k_s[...], v_s[...], la_s[...], span_s[0] = nxt

    wide_ok = span_s[0] < GLA_WIDE_MAX_SPAN

    @pl.when(wide_ok)
    def _():
        run([_gla_wide_task(*gla_args)])

    @pl.when(jnp.logical_not(wide_ok))
    def _():
        _gla_anchored_chunks(*gla_args)
        run([])


def _mix_gla(x, vecs, wt_a, wt_b, w_alpha, seq):
    t = x.shape[0]
    n_tiles = t // ROW_TILE

    def cur(n):
        return pl.BlockSpec((ROW_TILE, n), lambda i: (jnp.minimum(i, n_tiles - 1), 0))

    prev = pl.BlockSpec((ROW_TILE, GLA_VW), lambda i: (jnp.maximum(i - 1, 0), 0))

    def out(n):
        return jax.ShapeDtypeStruct((t, n), BF16)

    outs, _ = _call(
        functools.partial(_mix_gla_kernel, tiles_per_seq=seq // ROW_TILE, n_tiles=n_tiles),
        (x, vecs, wt_a, wt_b, w_alpha),
        [cur(D_MODEL), _resident(vecs.shape), _resident(wt_a.shape), _resident(wt_b.shape),
         _resident((GLA_RANK, GLA_QK))],
        [cur(sum(GATE_COLS)), prev],
        [out(sum(GATE_COLS)), out(GLA_VW)],
        grid=(n_tiles + 1,),
        scratch_shapes=[pltpu.VMEM((ROW_TILE, GLA_QK), BF16), pltpu.VMEM((ROW_TILE, GLA_QK), BF16),
                        pltpu.VMEM((ROW_TILE, GLA_VW), BF16), pltpu.VMEM((ROW_TILE, GLA_QK), F32),
                        pltpu.VMEM((GLA_HEADS, GLA_DK, GLA_DV), F32),
                        pltpu.VMEM((POOL_HALO + ROW_TILE, POOL_WIDTH), F32),
                        pltpu.SMEM((1,), F32),
                        *(pltpu.VMEM((D_MODEL, LANE if n == GLA_RANK else n), BF16) for n in W_IN_SIZES),
                        pltpu.VMEM((LANE, GLA_QK), BF16)],
        name="mix_gla")
    return outs


def _mem_kv_kernel(m_ref, p_ref, wk_ref, wv_ref, k_ref, v_ref, wk_s, wv_s):
    @pl.when(pl.program_id(0) == 0)
    def _():
        wk_s[...] = wk_ref[...].astype(BF16)
        wv_s[...] = wv_ref[...].astype(BF16)

    m = _rms(m_ref[...], _vec(p_ref, "mem_norm")).astype(BF16)
    k_ref[...] = _dot(m, wk_s[...]).astype(BF16)
    v_ref[...] = _dot(m, wv_s[...]).astype(BF16)


def _mem_kv(mem, vecs, wk, wv, mem_len, make_jobs):
    t = mem.shape[0]
    steps = t // mem_len
    row = pl.BlockSpec((mem_len, D_MODEL), lambda i: (i, 0))
    out = jax.ShapeDtypeStruct((t, D_MODEL), BF16)
    return _call(
        _mem_kv_kernel, (mem, vecs, wk, wv),
        [row, _resident(vecs.shape), _resident((D_MODEL, D_MODEL)), _resident((D_MODEL, D_MODEL))],
        [row, row], [out, out], make_jobs(steps), grid=(steps,),
        scratch_shapes=[pltpu.VMEM((D_MODEL, D_MODEL), BF16)] * 2, name="mem_kv")


def _mix_xattn_kernel(x_ref, o_ref, gz_ref, p_ref, wua_ref, pm_ref, wub_ref, wmo_ref,
                      wq_ref, k_ref, v_ref, wo_ref, y_ref):
    sr_ref, sga_ref, sgb_ref, z_ref = _gate_views(gz_ref)
    zs = []
    for g in range(POOL_GROUPS):
        cols = slice(g * POOL_GC, (g + 1) * POOL_GC)
        scale = _vec(p_ref, "pool_scale", POOL_WIDTH)[:, cols]
        zs.append((_dot(z_ref[:, cols], pm_ref[g].astype(BF16)) * scale).astype(BF16))
    ya = _dot(o_ref[...] * sr_ref[...], wua_ref[...])
    yb = _dot(jnp.concatenate(zs, axis=1), wub_ref[...])
    merged = (sga_ref[...].astype(F32) * ya + sgb_ref[...].astype(F32) * yb).astype(BF16)
    x = x_ref[...] + _dot(merged, wmo_ref[...])

    inv_rms = lax.rsqrt(jnp.mean(x * x, axis=-1, keepdims=True) + EPS)
    q = (_dot((x * _vec(p_ref, "xa_norm")).astype(BF16), wq_ref[...]) * (inv_rms * (XA_HD ** -0.5))).astype(BF16)

    def scores(hd):
        cols = slice(hd * XA_HD, (hd + 1) * XA_HD)
        return lax.dot_general(q[:, cols], k_ref[:, cols], (((1,), (1,)), ((), ())), preferred_element_type=F32)

    outs = []
    s = scores(0)
    for hd in range(XA_HEADS):
        s_next = scores(hd + 1) if hd + 1 < XA_HEADS else None
        e = jnp.exp(s - jnp.max(s, axis=-1, keepdims=True))
        o = _dot(e.astype(BF16), v_ref[:, hd * XA_HD:(hd + 1) * XA_HD])
        outs.append((o / jnp.sum(e, axis=-1, keepdims=True)).astype(BF16))
        s = s_next
    y_ref[...] = x + _dot(jnp.concatenate(outs, axis=1), wo_ref[...])


def _mix_xattn(x, o, gz, vecs, wua, pm, wub, wmo, wq, km, vm, wo, seq, mem_len):
    t = x.shape[0]
    steps = t // ROW_TILE
    tiles_per_seq = seq // ROW_TILE

    def row(n):
        return pl.BlockSpec((ROW_TILE, n), lambda i: (i, 0))

    mem = pl.BlockSpec((mem_len, D_MODEL), lambda i: (i // tiles_per_seq, 0))
    (y,), _ = _call(
        _mix_xattn_kernel, (x, o, gz, vecs, wua, pm, wub, wmo, wq, km, vm, wo),
        [row(D_MODEL), row(GLA_VW), row(sum(GATE_COLS)), _resident(vecs.shape),
         _resident((GLA_VW, D_MODEL)), _resident((POOL_GROUPS, POOL_GC, POOL_GC)),
         _resident((POOL_WIDTH, D_MODEL)), _resident((D_MODEL, D_MODEL)),
         _resident((D_MODEL, D_MODEL)), mem, mem, _resident((D_MODEL, D_MODEL))],
        [row(D_MODEL)], [jax.ShapeDtypeStruct((t, D_MODEL), F32)],
        grid=(steps,), name="mix_xattn")
    return y


def _layer(x, mem, ffn1_norm, ffn1_w1, ffn1_w3, ffn1_w2, mix_norm, w_in, w_alpha, b_alpha, gla_head_norm,
           w_up_a, pool_mix, pool_scale, w_up_b, w_mix_out, xa_norm, mem_norm, xa_wq, xa_wk, xa_wv, xa_wo,
           ffn2_norm, ffn2_w1, ffn2_w3, ffn2_w2, final_norm, last):
    batch, seq, _ = x.shape
    mem_len = mem.shape[1]
    assert seq % ROW_TILE == 0 and ROW_TILE % GLA_CHUNK == 0 and ROW_TILE % GLA_WIDE == 0

    vecs = _pack_vectors(ffn1_norm=ffn1_norm, mix_norm=mix_norm, b_alpha=b_alpha, gla_head_norm=gla_head_norm,
                         pool_scale=pool_scale, xa_norm=xa_norm, mem_norm=mem_norm, ffn2_norm=ffn2_norm,
                         final_norm=final_norm)

    def casts(*ws):
        return lambda n_steps: tuple(_cast_rows_job(w, n_steps) for w in ws)

    (km, vm), ffn1_w = _mem_kv(mem.reshape(batch * mem_len, D_MODEL), vecs, xa_wk, xa_wv, mem_len,
                               casts(ffn1_w1, ffn1_w3, ffn1_w2))
    later = (w_up_a, w_up_b, w_mix_out, xa_wq, xa_wo, ffn2_w1, ffn2_w3, ffn2_w2)
    x1, cast = _ffn(x.reshape(batch * seq, D_MODEL), vecs, "ffn1_norm", *ffn1_w, False,
                    lambda n_steps: (_w_in_job(jnp.swapaxes(w_in, 0, 1), n_steps), *casts(*later)(n_steps)))
    wt_a, wt_b, wua, wub, wmo, xwq, xwo, *ffn2_w = cast
    gz, o = _mix_gla(x1, vecs, wt_a, wt_b, w_alpha, seq)
    x3 = _mix_xattn(x1, o, gz, vecs, wua, pool_mix, wub, wmo, xwq, km, vm, xwo, seq, mem_len)
    x4, _ = _ffn(x3, vecs, "ffn2_norm", *ffn2_w, last)
    return x4.reshape(batch, seq, D_MODEL)


def kernel(x, mem, ffn1_norm, ffn1_w1, ffn1_w3, ffn1_w2, mix_norm, w_in, w_alpha, b_alpha, gla_head_norm, w_up_a,
           pool_mix, pool_scale, w_up_b, w_mix_out, xa_norm, mem_norm, xa_wq, xa_wk, xa_wv, xa_wo, ffn2_norm,
           ffn2_w1, ffn2_w3, ffn2_w2, final_norm):
    depth = ffn1_norm.shape[0]
    for l in range(depth):
        last = l == depth - 1
        x = _layer(x, mem, ffn1_norm[l], ffn1_w1[l], ffn1_w3[l], ffn1_w2[l], mix_norm[l], w_in[l], w_alpha[l],
                   b_alpha[l], gla_head_norm[l], w_up_a[l], pool_mix[l], pool_scale[l], w_up_b[l], w_mix_out[l],
                   xa_norm[l], mem_norm[l], xa_wq[l], xa_wk[l], xa_wv[l], xa_wo[l], ffn2_norm[l], ffn2_w1[l],
                   ffn2_w3[l], ffn2_w2[l], final_norm, last)
    return x
```

```python
import functools
from typing import Callable, NamedTuple

import jax
import jax.numpy as jnp
from jax import lax
from jax.experimental import pallas as pl
from jax.experimental.pallas import tpu as pltpu

F32 = jnp.float32
BF16 = jnp.bfloat16

D_MODEL = 1024
D_FF = 2816
EPS = 1e-6
GLA_HEADS = 4
GLA_DK = 128
GLA_DV = 256
GLA_QK = GLA_HEADS * GLA_DK
GLA_VW = GLA_HEADS * GLA_DV
GLA_RANK = 16
GLA_GATE_TEMP = 16.0
GLA_CHUNK = 64
GLA_SUB = 16
GLA_WIDE = 256
GLA_WIDE_MAX_SPAN = 60.0
EXP_TO_ZERO = -1e30
POOL_GROUPS = 4
POOL_GC = 128
POOL_WIDTH = POOL_GROUPS * POOL_GC
POOL_WINDOWS = (2, 4, 8, 16)
POOL_HALO = 16
XA_HEADS = 4
XA_HD = 256

LANE = 128
BF16_ROWS = 16
W_IN_SIZES = (GLA_QK, GLA_QK, GLA_VW, GLA_VW, GLA_RANK, POOL_WIDTH, D_MODEL, D_MODEL)
W_IN_SPLIT = 4
ROW_TILE = 512
FFN_ROWS = 1024
FF_SPLITS = (0, 1024, 2048, D_FF)
VMEM_LIMIT = 56 * 1024 * 1024


def _rms(x, g):
    return x * lax.rsqrt(jnp.mean(x * x, axis=-1, keepdims=True) + EPS) * g


def _sigmoid(x):
    return 0.5 * jnp.tanh(0.5 * x) + 0.5


def _dot(a, b):
    return jnp.dot(a, b, preferred_element_type=F32)


def _resident(shape):
    nd = len(shape)
    return pl.BlockSpec(shape, lambda *_: (0,) * nd, pipeline_mode=pl.Buffered(1))


def _params():
    return pltpu.CompilerParams(dimension_semantics=("arbitrary",), vmem_limit_bytes=VMEM_LIMIT)


class _CastJob(NamedTuple):
    inputs: tuple
    in_specs: tuple
    out_specs: tuple
    out_shapes: tuple
    body: Callable


def _cast_rows_job(w, n_steps):
    rows, cols = w.shape
    per_step = -(-rows // n_steps)
    block = next(b for b in range(BF16_ROWS, rows + 1, BF16_ROWS) if rows % b == 0 and b >= per_step)
    n_blocks = rows // block
    spec = pl.BlockSpec((block, cols), lambda i: ((i * n_blocks) // n_steps, 0))

    def body(ins, outs):
        outs[0][...] = ins[0][...].astype(BF16)

    return _CastJob((w,), (spec,), (spec,), (jax.ShapeDtypeStruct((rows, cols), BF16),), body)


def _w_in_job(w_in_t, n_steps):
    cols = w_in_t.shape[1]
    rows_a = sum(W_IN_SIZES[:W_IN_SPLIT])
    rows_b = sum(W_IN_SIZES[W_IN_SPLIT:])
    assert w_in_t.shape[0] == rows_a + rows_b and rows_a % n_steps == 0
    blk_a = rows_a // n_steps
    per_b = -(-rows_b // n_steps)
    blk_b = next(r for r in range(BF16_ROWS, rows_b + 1, BF16_ROWS) if rows_b % r == 0 and r >= per_b)
    last_b = rows_b // blk_b - 1
    assert blk_a % BF16_ROWS == 0
    in_a = pl.BlockSpec((blk_a, cols), lambda i: (i, 0))
    in_b = pl.BlockSpec((pl.Element(blk_b), pl.Element(cols)),
                        lambda i: (pl.multiple_of(rows_a + blk_b * jnp.minimum(i, last_b), BF16_ROWS), 0))
    out_b = pl.BlockSpec((blk_b, cols), lambda i: (jnp.minimum(i, last_b), 0))

    def body(ins, outs):
        for i_ref, o_ref in zip(ins, outs):
            o_ref[...] = i_ref[...].astype(BF16)

    return _CastJob((w_in_t, w_in_t), (in_a, in_b), (in_a, out_b),
                    (jax.ShapeDtypeStruct((rows_a, cols), BF16), jax.ShapeDtypeStruct((rows_b, cols), BF16)), body)


def _call(kernel_fn, args, in_specs, out_specs, out_shape, jobs=(), *, grid, scratch_shapes=(), name):
    n_in, n_out = len(args), len(out_specs)
    job_args = [a for j in jobs for a in j.inputs]
    job_outs = [o for j in jobs for o in j.out_shapes]

    def body(*refs):
        j_in = refs[n_in:n_in + len(job_args)]
        first_out = n_in + len(job_args)
        j_out = refs[first_out + n_out:first_out + n_out + len(job_outs)]
        kernel_fn(*refs[:n_in], *refs[first_out:first_out + n_out], *refs[first_out + n_out + len(job_outs):])
        for j in jobs:
            j.body(j_in[:len(j.inputs)], j_out[:len(j.out_shapes)])
            j_in, j_out = j_in[len(j.inputs):], j_out[len(j.out_shapes):]

    outs = pl.pallas_call(
        body,
        grid=grid,
        in_specs=[*in_specs, *(sp for j in jobs for sp in j.in_specs)],
        out_specs=[*out_specs, *(sp for j in jobs for sp in j.out_specs)],
        out_shape=[*out_shape, *job_outs],
        scratch_shapes=list(scratch_shapes),
        compiler_params=_params(),
        name=name,
    )(*args, *job_args)
    return outs[:n_out], outs[n_out:]


def _ffn_kernel(x_ref, g_ref, w1_ref, w3_ref, w2_ref, fg_ref, o_ref, *, final_norm):
    x = x_ref[...]
    h = _rms(x, g_ref[...]).astype(BF16)
    acc = jnp.zeros_like(x)
    for lo, hi in zip(FF_SPLITS[:-1], FF_SPLITS[1:]):
        a = _dot(h, w1_ref[:, lo:hi])
        b = _dot(h, w3_ref[:, lo:hi])
        act = (a * jax.nn.sigmoid(a) * b).astype(BF16)
        acc = acc + _dot(act, w2_ref[lo:hi, :])
    y = x + 0.5 * acc
    if final_norm:
        y = _rms(y, fg_ref[...])
    o_ref[...] = y


def _ffn(x, g, w1, w3, w2, fg, final_norm, make_jobs=None):
    t = x.shape[0]
    steps = t // FFN_ROWS
    row = pl.BlockSpec((FFN_ROWS, D_MODEL), lambda i: (i, 0))
    (y,), cast = _call(
        functools.partial(_ffn_kernel, final_norm=final_norm),
        (x, g, w1, w3, w2, fg),
        [row, _resident((1, D_MODEL)), _resident((D_MODEL, D_FF)), _resident((D_MODEL, D_FF)),
         _resident((D_FF, D_MODEL)), _resident((1, D_MODEL))],
        [row], [jax.ShapeDtypeStruct((t, D_MODEL), F32)],
        make_jobs(steps) if make_jobs else (),
        grid=(steps,), name="ffn_final" if final_norm else "ffn")
    return y, cast


def _cumsum_rows(mask_bf16, la):
    hi = la.astype(BF16)
    lo = (la - hi.astype(F32)).astype(BF16)
    return _dot(mask_bf16, hi) + _dot(mask_bf16, lo)


def _decay_column(b_last_row):
    return jnp.transpose(jnp.broadcast_to(jnp.exp(b_last_row), (8, GLA_DK)))[:, 0:1]


def _gla_wide_task(q_ref, k_ref, v_ref, la_ref, gn, o_ref, s_ref):
    c = GLA_WIDE
    ri = lax.broadcasted_iota(jnp.int32, (c, c), 0)
    ci = lax.broadcasted_iota(jnp.int32, (c, c), 1)
    causal = ci <= ri
    cum_incl = causal.astype(BF16)
    row_sl = [slice(ic * c, (ic + 1) * c) for ic in range(ROW_TILE // c)]
    b_all = [_dot(cum_incl, la_ref[rows, :].astype(BF16)) for rows in row_sl]
    units = [(ic, hd) for ic in range(len(row_sl)) for hd in range(GLA_HEADS)]

    def factors(ic, hd):
        rows, kc = row_sl[ic], slice(hd * GLA_DK, (hd + 1) * GLA_DK)
        b = b_all[ic][:, kc]
        qg = (q_ref[rows, kc].astype(F32) * jnp.exp(b)).astype(BF16)
        kg = (k_ref[rows, kc].astype(F32) * jnp.exp(-b)).astype(BF16)
        s = lax.dot_general(qg, kg, (((1,), (1,)), ((), ())), preferred_element_type=F32)
        return qg, kg, s, b[c - 1:c, :]

    nxt = factors(*units[0])
    yield
    for n, (ic, hd) in enumerate(units):
        qg, kg, s, b_last = nxt
        if n + 1 < len(units):
            nxt = factors(*units[n + 1])
        rows, vc = row_sl[ic], slice(hd * GLA_DV, (hd + 1) * GLA_DV)
        v = v_ref[rows, vc]
        s_old = s_ref[hd]
        inter = _dot(qg, s_old.astype(BF16))
        upd = lax.dot_general(kg, v, (((0,), (0,)), ((), ())), preferred_element_type=F32)
        p = jnp.where(causal, s, 0.0).astype(BF16)
        o = _dot(p, v) + inter
        s_ref[hd] = (s_old + upd) * _decay_column(b_last)
        o_ref[rows, vc] = _rms(o, gn).astype(BF16)
        yield


def _gla_anchored_chunks(q_ref, k_ref, v_ref, la_ref, gn, o_ref, s_ref):
    c = GLA_CHUNK
    ri = lax.broadcasted_iota(jnp.int32, (c, c), 0)
    ci = lax.broadcasted_iota(jnp.int32, (c, c), 1)
    causal = ci <= ri
    cum_incl = causal.astype(BF16)
    cum_anchor = (ci < (ri // GLA_SUB) * GLA_SUB).astype(BF16)
    krow = lax.broadcasted_iota(jnp.int32, (c, 1), 0)
    kcol = lax.broadcasted_iota(jnp.int32, (GLA_SUB, c), 1)

    def chunk(ic, carry):
        r0 = pl.multiple_of(ic * c, c)
        la = la_ref[pl.ds(r0, c), :]
        b_all = _cumsum_rows(cum_incl, la)
        beta_all = _cumsum_rows(cum_anchor, la)
        for hd in range(GLA_HEADS):
            kc = slice(hd * GLA_DK, (hd + 1) * GLA_DK)
            vc = slice(hd * GLA_DV, (hd + 1) * GLA_DV)
            q = q_ref[pl.ds(r0, c), kc].astype(F32)
            k = k_ref[pl.ds(r0, c), kc].astype(F32)
            v = v_ref[pl.ds(r0, c), vc]
            b = b_all[:, kc]
            beta = beta_all[:, kc]
            qa = (q * jnp.exp(b - beta)).astype(BF16)
            rows = []
            for blk in range(c // GLA_SUB):
                r_lo, r_hi = blk * GLA_SUB, (blk + 1) * GLA_SUB
                e = jnp.where(krow < r_lo, beta[r_lo:r_lo + 1, :] - b, EXP_TO_ZERO)
                ka = (k * jnp.exp(e)).astype(BF16)
                sc = lax.dot_general(qa[r_lo:r_hi, :], ka, (((1,), (1,)), ((), ())), preferred_element_type=F32)
                qb, bb = q[r_lo:r_hi, :], b[r_lo:r_hi, :]
                for j in range(r_lo, r_hi):
                    w = jnp.exp(jnp.minimum(bb - b[j:j + 1, :], 0.0))
                    sc = jnp.where(kcol == j, jnp.sum(qb * k[j:j + 1, :] * w, axis=1, keepdims=True), sc)
                rows.append(sc)
            p = jnp.where(causal, jnp.concatenate(rows, axis=0), 0.0).astype(BF16)
            s_old = s_ref[hd]
            o = _dot(p, v) + _dot((q * jnp.exp(b)).astype(BF16), s_old.astype(BF16))
            b_last = b[c - 1:c, :]
            k_dec = (k * jnp.exp(b_last - b)).astype(BF16)
            upd = lax.dot_general(k_dec, v, (((0,), (0,)), ((), ())), preferred_element_type=F32)
            s_ref[hd] = s_old * _decay_column(b_last) + upd
            o_ref[pl.ds(r0, c), vc] = _rms(o, gn).astype(BF16)
        return carry

    lax.fori_loop(0, ROW_TILE // c, chunk, 0)


def _interleave(tasks):
    tasks = list(tasks)
    while tasks:
        for t in list(tasks):
            try:
                next(t)
            except StopIteration:
                tasks.remove(t)


def _mix_in_task(x_ref, g_ref, wq_ref, wk_ref, wv_ref, wr_ref, wa_ref, wu_ref, wga_ref, wgb_ref, wal_ref, bal_ref,
                 sr_ref, z_ref, sga_ref, sgb_ref, ubuf, tile_in_seq, keep_halo, gla_inputs):
    h = _rms(x_ref[...], g_ref[...]).astype(BF16)
    a_code = _dot(h, wa_ref[...]).astype(BF16)
    yield
    u = _dot(h, wu_ref[...])
    yield
    r = _dot(h, wr_ref[...])
    sr_ref[...] = (r * _sigmoid(r)).astype(BF16)
    yield
    zg = _dot(a_code, wal_ref[...]) + bal_ref[...]
    yield
    sga_ref[...] = _sigmoid(_dot(h, wga_ref[...])).astype(BF16)
    yield
    sgb_ref[...] = _sigmoid(_dot(h, wgb_ref[...])).astype(BF16)
    yield

    ubuf[POOL_HALO:POOL_HALO + ROW_TILE, :] = u
    pos = lax.broadcasted_iota(jnp.int32, (ROW_TILE, 1), 0) + tile_in_seq * ROW_TILE
    for g, w in enumerate(POOL_WINDOWS):
        cols = slice(g * POOL_GC, (g + 1) * POOL_GC)
        s = ubuf[:, cols]
        sh = 1
        while sh < w:
            s = s + pltpu.roll(s, sh, 0)
            sh *= 2
        cnt = jnp.minimum(pos + 1, w).astype(F32)
        z_ref[:, cols] = (s[POOL_HALO:, :] / cnt - u[:, cols]).astype(BF16)
    ubuf[0:POOL_HALO, :] = jnp.where(keep_halo, ubuf[0:POOL_HALO, :], ubuf[ROW_TILE:ROW_TILE + POOL_HALO, :])

    q = (_dot(h, wq_ref[...]) * (GLA_DK ** -0.5)).astype(BF16)
    yield
    v = _dot(h, wv_ref[...]).astype(BF16)
    yield
    k = _dot(h, wk_ref[...]).astype(BF16)
    la = (jnp.minimum(zg, 0.0) - jnp.log(1.0 + jnp.exp(-jnp.abs(zg)))) * (1.0 / GLA_GATE_TEMP)
    span = None
    for r0 in range(0, ROW_TILE, GLA_WIDE):
        sp = jnp.max(-jnp.sum(la[r0:r0 + GLA_WIDE, :], axis=0, keepdims=True))
        span = sp if span is None else jnp.maximum(span, sp)
    gla_inputs.extend((q, k, v, la, span))


def _mix_gla_kernel(x_ref, g_ref, wt_a_ref, wt_b_ref, w_alpha_ref, bal_ref, gn_ref,
                    sr_ref, z_ref, sga_ref, sgb_ref, o_ref,
                    q_s, k_s, v_s, la_s, s_ref, ubuf, span_s,
                    wq_ref, wk_ref, wv_ref, wr_ref, wa_ref, wu_ref, wga_ref, wgb_ref, wal_ref,
                    *, tiles_per_seq, n_tiles):
    step = pl.program_id(0)
    mix_tile = jnp.minimum(step, n_tiles - 1)
    gla_tile = step - 1

    @pl.when(step == 0)
    def _():
        q_s[...] = jnp.zeros_like(q_s)
        k_s[...] = jnp.zeros_like(k_s)
        v_s[...] = jnp.zeros_like(v_s)
        la_s[...] = jnp.zeros_like(la_s)
        span_s[0] = 0.0
        lo = 0
        for w_ref, n in zip((wq_ref, wk_ref, wv_ref, wr_ref), W_IN_SIZES[:W_IN_SPLIT]):
            w_ref[...] = wt_a_ref[lo:lo + n, :].T
            lo += n
        lane = lax.broadcasted_iota(jnp.int32, (D_MODEL, LANE), 1)
        wa_ref[...] = jnp.where(lane < GLA_RANK, wt_b_ref[0:LANE, :].T, jnp.zeros((), BF16))
        lo = GLA_RANK
        for w_ref, n in zip((wu_ref, wga_ref, wgb_ref), W_IN_SIZES[W_IN_SPLIT + 1:]):
            w_ref[...] = wt_b_ref[lo:lo + n, :].T
            lo += n
        wal_ref[...] = jnp.zeros_like(wal_ref)
        wal_ref[0:GLA_RANK, :] = w_alpha_ref[...].astype(BF16)

    @pl.when(mix_tile % tiles_per_seq == 0)
    def _():
        ubuf[0:POOL_HALO, :] = jnp.zeros((POOL_HALO, POOL_WIDTH), F32)

    @pl.when(jnp.logical_or(step == 0, gla_tile % tiles_per_seq == 0))
    def _():
        s_ref[...] = jnp.zeros_like(s_ref)

    keep_halo = step >= n_tiles - 1
    mix_args = (x_ref, g_ref, wq_ref, wk_ref, wv_ref, wr_ref, wa_ref, wu_ref, wga_ref, wgb_ref, wal_ref, bal_ref,
                sr_ref, z_ref, sga_ref, sgb_ref, ubuf, mix_tile % tiles_per_seq, keep_halo)
    gla_args = (q_s, k_s, v_s, la_s, gn_ref[...], o_ref, s_ref)

    def run(gla_tasks):
        nxt = []
        _interleave(gla_tasks + [_mix_in_task(*mix_args, nxt)])
        q_s[...], k_s[...], v_s[...], la_s[...], span_s[0] = nxt

    wide_ok = span_s[0] < GLA_WIDE_MAX_SPAN

    @pl.when(wide_ok)
    def _():
        run([_gla_wide_task(*gla_args)])

    @pl.when(jnp.logical_not(wide_ok))
    def _():
        _gla_anchored_chunks(*gla_args)
        run([])


def _mix_gla(x, g, wt_a, wt_b, w_alpha, bal, gn, seq):
    t = x.shape[0]
    n_tiles = t // ROW_TILE

    def cur(n):
        return pl.BlockSpec((ROW_TILE, n), lambda i: (jnp.minimum(i, n_tiles - 1), 0))

    prev = pl.BlockSpec((ROW_TILE, GLA_VW), lambda i: (jnp.maximum(i - 1, 0), 0))

    def out(n):
        return jax.ShapeDtypeStruct((t, n), BF16)

    outs, _ = _call(
        functools.partial(_mix_gla_kernel, tiles_per_seq=seq // ROW_TILE, n_tiles=n_tiles),
        (x, g, wt_a, wt_b, w_alpha, bal, gn),
        [cur(D_MODEL), _resident((1, D_MODEL)), _resident(wt_a.shape), _resident(wt_b.shape),
         _resident((GLA_RANK, GLA_QK)), _resident((1, GLA_QK)), _resident((1, GLA_DV))],
        [cur(GLA_VW), cur(POOL_WIDTH), cur(D_MODEL), cur(D_MODEL), prev],
        [out(GLA_VW), out(POOL_WIDTH), out(D_MODEL), out(D_MODEL), out(GLA_VW)],
        grid=(n_tiles + 1,),
        scratch_shapes=[pltpu.VMEM((ROW_TILE, GLA_QK), BF16), pltpu.VMEM((ROW_TILE, GLA_QK), BF16),
                        pltpu.VMEM((ROW_TILE, GLA_VW), BF16), pltpu.VMEM((ROW_TILE, GLA_QK), F32),
                        pltpu.VMEM((GLA_HEADS, GLA_DK, GLA_DV), F32),
                        pltpu.VMEM((POOL_HALO + ROW_TILE, POOL_WIDTH), F32),
                        pltpu.SMEM((1,), F32),
                        *(pltpu.VMEM((D_MODEL, LANE if n == GLA_RANK else n), BF16) for n in W_IN_SIZES),
                        pltpu.VMEM((LANE, GLA_QK), BF16)],
        name="mix_gla")
    return outs


def _mem_kv_kernel(m_ref, g_ref, wk_ref, wv_ref, k_ref, v_ref, wk_s, wv_s):
    @pl.when(pl.program_id(0) == 0)
    def _():
        wk_s[...] = wk_ref[...].astype(BF16)
        wv_s[...] = wv_ref[...].astype(BF16)

    m = _rms(m_ref[...], g_ref[...]).astype(BF16)
    k_ref[...] = _dot(m, wk_s[...]).astype(BF16)
    v_ref[...] = _dot(m, wv_s[...]).astype(BF16)


def _mem_kv(mem, g, wk, wv, mem_len, make_jobs):
    t = mem.shape[0]
    steps = t // mem_len
    row = pl.BlockSpec((mem_len, D_MODEL), lambda i: (i, 0))
    out = jax.ShapeDtypeStruct((t, D_MODEL), BF16)
    return _call(
        _mem_kv_kernel, (mem, g, wk, wv),
        [row, _resident((1, D_MODEL)), _resident((D_MODEL, D_MODEL)), _resident((D_MODEL, D_MODEL))],
        [row, row], [out, out], make_jobs(steps), grid=(steps,),
        scratch_shapes=[pltpu.VMEM((D_MODEL, D_MODEL), BF16)] * 2, name="mem_kv")


def _mix_xattn_kernel(x_ref, o_ref, sr_ref, z_ref, sga_ref, sgb_ref, wua_ref, pm_ref, ps_ref, wub_ref, wmo_ref,
                      xg_ref, wq_ref, k_ref, v_ref, wo_ref, y_ref):
    zs = []
    for g in range(POOL_GROUPS):
        cols = slice(g * POOL_GC, (g + 1) * POOL_GC)
        zs.append((_dot(z_ref[:, cols], pm_ref[g].astype(BF16)) * ps_ref[:, cols]).astype(BF16))
    ya = _dot(o_ref[...] * sr_ref[...], wua_ref[...])
    yb = _dot(jnp.concatenate(zs, axis=1), wub_ref[...])
    merged = (sga_ref[...].astype(F32) * ya + sgb_ref[...].astype(F32) * yb).astype(BF16)
    x = x_ref[...] + _dot(merged, wmo_ref[...])

    inv_rms = lax.rsqrt(jnp.mean(x * x, axis=-1, keepdims=True) + EPS)
    q = (_dot((x * xg_ref[...]).astype(BF16), wq_ref[...]) * (inv_rms * (XA_HD ** -0.5))).astype(BF16)

    def scores(hd):
        cols = slice(hd * XA_HD, (hd + 1) * XA_HD)
        return lax.dot_general(q[:, cols], k_ref[:, cols], (((1,), (1,)), ((), ())), preferred_element_type=F32)

    outs = []
    s = scores(0)
    for hd in range(XA_HEADS):
        s_next = scores(hd + 1) if hd + 1 < XA_HEADS else None
        e = jnp.exp(s - jnp.max(s, axis=-1, keepdims=True))
        o = _dot(e.astype(BF16), v_ref[:, hd * XA_HD:(hd + 1) * XA_HD])
        outs.append((o / jnp.sum(e, axis=-1, keepdims=True)).astype(BF16))
        s = s_next
    y_ref[...] = x + _dot(jnp.concatenate(outs, axis=1), wo_ref[...])


def _mix_xattn(x, o, sr, z, sga, sgb, wua, pm, ps, wub, wmo, xg, wq, km, vm, wo, seq, mem_len):
    t = x.shape[0]
    steps = t // ROW_TILE
    tiles_per_seq = seq // ROW_TILE

    def row(n):
        return pl.BlockSpec((ROW_TILE, n), lambda i: (i, 0))

    mem = pl.BlockSpec((mem_len, D_MODEL), lambda i: (i // tiles_per_seq, 0))
    (y,), _ = _call(
        _mix_xattn_kernel, (x, o, sr, z, sga, sgb, wua, pm, ps, wub, wmo, xg, wq, km, vm, wo),
        [row(D_MODEL), row(GLA_VW), row(GLA_VW), row(POOL_WIDTH), row(D_MODEL), row(D_MODEL),
         _resident((GLA_VW, D_MODEL)), _resident((POOL_GROUPS, POOL_GC, POOL_GC)),
         _resident((1, POOL_WIDTH)), _resident((POOL_WIDTH, D_MODEL)), _resident((D_MODEL, D_MODEL)),
         _resident((1, D_MODEL)), _resident((D_MODEL, D_MODEL)), mem, mem, _resident((D_MODEL, D_MODEL))],
        [row(D_MODEL)], [jax.ShapeDtypeStruct((t, D_MODEL), F32)],
        grid=(steps,), name="mix_xattn")
    return y


def _layer(x, mem, ffn1_norm, ffn1_w1, ffn1_w3, ffn1_w2, mix_norm, w_in, w_alpha, b_alpha, gla_head_norm,
           w_up_a, pool_mix, pool_scale, w_up_b, w_mix_out, xa_norm, mem_norm, xa_wq, xa_wk, xa_wv, xa_wo,
           ffn2_norm, ffn2_w1, ffn2_w3, ffn2_w2, final_norm, last):
    batch, seq, _ = x.shape
    mem_len = mem.shape[1]
    assert seq % ROW_TILE == 0 and ROW_TILE % GLA_CHUNK == 0 and ROW_TILE % GLA_WIDE == 0

    def vec(a):
        return a.reshape(1, -1).astype(F32)

    def casts(*ws):
        return lambda n_steps: tuple(_cast_rows_job(w, n_steps) for w in ws)

    (km, vm), ffn1_w = _mem_kv(mem.reshape(batch * mem_len, D_MODEL), vec(mem_norm), xa_wk, xa_wv, mem_len,
                               casts(ffn1_w1, ffn1_w3, ffn1_w2))
    later = (w_up_a, w_up_b, w_mix_out, xa_wq, xa_wo, ffn2_w1, ffn2_w3, ffn2_w2)
    x1, cast = _ffn(x.reshape(batch * seq, D_MODEL), vec(ffn1_norm), *ffn1_w, vec(final_norm), False,
                    lambda n_steps: (_w_in_job(jnp.swapaxes(w_in, 0, 1), n_steps), *casts(*later)(n_steps)))
    wt_a, wt_b, wua, wub, wmo, xwq, xwo, *ffn2_w = cast
    sr, z, sga, sgb, o = _mix_gla(x1, vec(mix_norm), wt_a, wt_b, w_alpha, vec(b_alpha), vec(gla_head_norm), seq)
    x3 = _mix_xattn(x1, o, sr, z, sga, sgb, wua, pool_mix, vec(pool_scale), wub, wmo, vec(xa_norm), xwq, km, vm, xwo,
                    seq, mem_len)
    x4, _ = _ffn(x3, vec(ffn2_norm), *ffn2_w, vec(final_norm), last)
    return x4.reshape(batch, seq, D_MODEL)


def kernel(x, mem, ffn1_norm, ffn1_w1, ffn1_w3, ffn1_w2, mix_norm, w_in, w_alpha, b_alpha, gla_head_norm, w_up_a,
           pool_mix, pool_scale, w_up_b, w_mix_out, xa_norm, mem_norm, xa_wq, xa_wk, xa_wv, xa_wo, ffn2_norm,
           ffn2_w1, ffn2_w3, ffn2_w2, final_norm):
    depth = ffn1_norm.shape[0]
    for l in range(depth):
        last = l == depth - 1
        x = _layer(x, mem, ffn1_norm[l], ffn1_w1[l], ffn1_w3[l], ffn1_w2[l], mix_norm[l], w_in[l], w_alpha[l],
                   b_alpha[l], gla_head_norm[l], w_up_a[l], pool_mix[l], pool_scale[l], w_up_b[l], w_mix_out[l],
                   xa_norm[l], mem_norm[l], xa_wq[l], xa_wk[l], xa_wv[l], xa_wo[l], ffn2_norm[l], ffn2_w1[l],
                   ffn2_w3[l], ffn2_w2[l], final_norm, last)
    return x
```

```python
import functools
from typing import Callable, NamedTuple

import jax
import jax.numpy as jnp
from jax import lax
from jax.experimental import pallas as pl
from jax.experimental.pallas import tpu as pltpu

F32 = jnp.float32
BF16 = jnp.bfloat16

D_MODEL = 1024
D_FF = 2816
EPS = 1e-6
GLA_HEADS = 4
GLA_DK = 128
GLA_DV = 256
GLA_QK = GLA_HEADS * GLA_DK
GLA_VW = GLA_HEADS * GLA_DV
GLA_RANK = 16
GLA_GATE_TEMP = 16.0
GLA_CHUNK = 64
GLA_SUB = 16
GLA_WIDE = 256
GLA_WIDE_MAX_SPAN = 60.0
EXP_TO_ZERO = -1e30
POOL_GROUPS = 4
POOL_GC = 128
POOL_WIDTH = POOL_GROUPS * POOL_GC
POOL_WINDOWS = (2, 4, 8, 16)
POOL_HALO = 16
XA_HEADS = 4
XA_HD = 256

LANE = 128
BF16_ROWS = 16
W_IN_SIZES = (GLA_QK, GLA_QK, GLA_VW, GLA_VW, GLA_RANK, POOL_WIDTH, D_MODEL, D_MODEL)
W_IN_SPLIT = 4
ROW_TILE = 512
FFN_ROWS = 1024
FF_SPLITS = (0, 1024, 2048, D_FF)
VMEM_LIMIT = 56 * 1024 * 1024


def _rms(x, g):
    return x * lax.rsqrt(jnp.mean(x * x, axis=-1, keepdims=True) + EPS) * g


def _sigmoid(x):
    return 0.5 * jnp.tanh(0.5 * x) + 0.5


def _dot(a, b):
    return jnp.dot(a, b, preferred_element_type=F32)


def _resident(shape):
    nd = len(shape)
    return pl.BlockSpec(shape, lambda *_: (0,) * nd, pipeline_mode=pl.Buffered(1))


def _params():
    return pltpu.CompilerParams(dimension_semantics=("arbitrary",), vmem_limit_bytes=VMEM_LIMIT)


class _CastJob(NamedTuple):
    inputs: tuple
    in_specs: tuple
    out_specs: tuple
    out_shapes: tuple
    body: Callable


def _cast_rows_job(w, n_steps):
    rows, cols = w.shape
    per_step = -(-rows // n_steps)
    block = next(b for b in range(BF16_ROWS, rows + 1, BF16_ROWS) if rows % b == 0 and b >= per_step)
    n_blocks = rows // block
    spec = pl.BlockSpec((block, cols), lambda i: ((i * n_blocks) // n_steps, 0))

    def body(ins, outs):
        outs[0][...] = ins[0][...].astype(BF16)

    return _CastJob((w,), (spec,), (spec,), (jax.ShapeDtypeStruct((rows, cols), BF16),), body)


def _w_in_job(w_in_t, n_steps):
    cols = w_in_t.shape[1]
    rows_a = sum(W_IN_SIZES[:W_IN_SPLIT])
    rows_b = sum(W_IN_SIZES[W_IN_SPLIT:])
    assert w_in_t.shape[0] == rows_a + rows_b and rows_a % n_steps == 0
    blk_a = rows_a // n_steps
    per_b = -(-rows_b // n_steps)
    blk_b = next(r for r in range(BF16_ROWS, rows_b + 1, BF16_ROWS) if rows_b % r == 0 and r >= per_b)
    last_b = rows_b // blk_b - 1
    assert blk_a % BF16_ROWS == 0
    in_a = pl.BlockSpec((blk_a, cols), lambda i: (i, 0))
    in_b = pl.BlockSpec((pl.Element(blk_b), pl.Element(cols)),
                        lambda i: (pl.multiple_of(rows_a + blk_b * jnp.minimum(i, last_b), BF16_ROWS), 0))
    out_b = pl.BlockSpec((blk_b, cols), lambda i: (jnp.minimum(i, last_b), 0))

    def body(ins, outs):
        for i_ref, o_ref in zip(ins, outs):
            o_ref[...] = i_ref[...].astype(BF16)

    return _CastJob((w_in_t, w_in_t), (in_a, in_b), (in_a, out_b),
                    (jax.ShapeDtypeStruct((rows_a, cols), BF16), jax.ShapeDtypeStruct((rows_b, cols), BF16)), body)


def _call(kernel_fn, args, in_specs, out_specs, out_shape, jobs=(), *, grid, scratch_shapes=(), name):
    n_in, n_out = len(args), len(out_specs)
    job_args = [a for j in jobs for a in j.inputs]
    job_outs = [o for j in jobs for o in j.out_shapes]

    def body(*refs):
        j_in = refs[n_in:n_in + len(job_args)]
        first_out = n_in + len(job_args)
        j_out = refs[first_out + n_out:first_out + n_out + len(job_outs)]
        kernel_fn(*refs[:n_in], *refs[first_out:first_out + n_out], *refs[first_out + n_out + len(job_outs):])
        for j in jobs:
            j.body(j_in[:len(j.inputs)], j_out[:len(j.out_shapes)])
            j_in, j_out = j_in[len(j.inputs):], j_out[len(j.out_shapes):]

    outs = pl.pallas_call(
        body,
        grid=grid,
        in_specs=[*in_specs, *(sp for j in jobs for sp in j.in_specs)],
        out_specs=[*out_specs, *(sp for j in jobs for sp in j.out_specs)],
        out_shape=[*out_shape, *job_outs],
        scratch_shapes=list(scratch_shapes),
        compiler_params=_params(),
        name=name,
    )(*args, *job_args)
    return outs[:n_out], outs[n_out:]


def _ffn_kernel(x_ref, g_ref, w1_ref, w3_ref, w2_ref, fg_ref, o_ref, *, final_norm):
    x = x_ref[...]
    inv_rms = lax.rsqrt(jnp.mean(x * x, axis=-1, keepdims=True) + EPS)
    h = (x * g_ref[...]).astype(BF16)
    acc = jnp.zeros_like(x)
    for lo, hi in zip(FF_SPLITS[:-1], FF_SPLITS[1:]):
        a = _dot(h, w1_ref[:, lo:hi]) * inv_rms
        b = _dot(h, w3_ref[:, lo:hi]) * inv_rms
        act = (a * jax.nn.sigmoid(a) * b).astype(BF16)
        acc = acc + _dot(act, w2_ref[lo:hi, :])
    y = x + 0.5 * acc
    if final_norm:
        y = _rms(y, fg_ref[...])
    o_ref[...] = y


def _ffn(x, g, w1, w3, w2, fg, final_norm, make_jobs=None):
    t = x.shape[0]
    steps = t // FFN_ROWS
    row = pl.BlockSpec((FFN_ROWS, D_MODEL), lambda i: (i, 0))
    (y,), cast = _call(
        functools.partial(_ffn_kernel, final_norm=final_norm),
        (x, g, w1, w3, w2, fg),
        [row, _resident((1, D_MODEL)), _resident((D_MODEL, D_FF)), _resident((D_MODEL, D_FF)),
         _resident((D_FF, D_MODEL)), _resident((1, D_MODEL))],
        [row], [jax.ShapeDtypeStruct((t, D_MODEL), F32)],
        make_jobs(steps) if make_jobs else (),
        grid=(steps,), name="ffn_final" if final_norm else "ffn")
    return y, cast


def _cumsum_rows(mask_bf16, la):
    hi = la.astype(BF16)
    lo = (la - hi.astype(F32)).astype(BF16)
    return _dot(mask_bf16, hi) + _dot(mask_bf16, lo)


def _decay_column(b_last_row):
    return jnp.transpose(jnp.broadcast_to(jnp.exp(b_last_row), (8, GLA_DK)))[:, 0:1]


def _gla_wide_task(q_ref, k_ref, v_ref, la_ref, gn, o_ref, s_ref):
    c = GLA_WIDE
    ri = lax.broadcasted_iota(jnp.int32, (c, c), 0)
    ci = lax.broadcasted_iota(jnp.int32, (c, c), 1)
    causal = ci <= ri
    cum_incl = causal.astype(BF16)
    row_sl = [slice(ic * c, (ic + 1) * c) for ic in range(ROW_TILE // c)]
    b_all = [_dot(cum_incl, la_ref[rows, :].astype(BF16)) for rows in row_sl]
    units = [(ic, hd) for ic in range(len(row_sl)) for hd in range(GLA_HEADS)]

    def factors(ic, hd):
        rows, kc = row_sl[ic], slice(hd * GLA_DK, (hd + 1) * GLA_DK)
        b = b_all[ic][:, kc]
        qg = (q_ref[rows, kc].astype(F32) * jnp.exp(b)).astype(BF16)
        kg = (k_ref[rows, kc].astype(F32) * jnp.exp(-b)).astype(BF16)
        s = lax.dot_general(qg, kg, (((1,), (1,)), ((), ())), preferred_element_type=F32)
        return qg, kg, s, b[c - 1:c, :]

    nxt = factors(*units[0])
    yield
    for n, (ic, hd) in enumerate(units):
        qg, kg, s, b_last = nxt
        if n + 1 < len(units):
            nxt = factors(*units[n + 1])
        rows, vc = row_sl[ic], slice(hd * GLA_DV, (hd + 1) * GLA_DV)
        v = v_ref[rows, vc]
        s_old = s_ref[hd]
        inter = _dot(qg, s_old.astype(BF16))
        upd = lax.dot_general(kg, v, (((0,), (0,)), ((), ())), preferred_element_type=F32)
        p = jnp.where(causal, s, 0.0).astype(BF16)
        o = _dot(p, v) + inter
        s_ref[hd] = (s_old + upd) * _decay_column(b_last)
        o_ref[rows, vc] = _rms(o, gn).astype(BF16)
        yield


def _gla_anchored_chunks(q_ref, k_ref, v_ref, la_ref, gn, o_ref, s_ref):
    c = GLA_CHUNK
    ri = lax.broadcasted_iota(jnp.int32, (c, c), 0)
    ci = lax.broadcasted_iota(jnp.int32, (c, c), 1)
    causal = ci <= ri
    cum_incl = causal.astype(BF16)
    cum_anchor = (ci < (ri // GLA_SUB) * GLA_SUB).astype(BF16)
    krow = lax.broadcasted_iota(jnp.int32, (c, 1), 0)
    kcol = lax.broadcasted_iota(jnp.int32, (GLA_SUB, c), 1)

    def chunk(ic, carry):
        r0 = pl.multiple_of(ic * c, c)
        la = la_ref[pl.ds(r0, c), :]
        b_all = _cumsum_rows(cum_incl, la)
        beta_all = _cumsum_rows(cum_anchor, la)
        for hd in range(GLA_HEADS):
            kc = slice(hd * GLA_DK, (hd + 1) * GLA_DK)
            vc = slice(hd * GLA_DV, (hd + 1) * GLA_DV)
            q = q_ref[pl.ds(r0, c), kc].astype(F32)
            k = k_ref[pl.ds(r0, c), kc].astype(F32)
            v = v_ref[pl.ds(r0, c), vc]
            b = b_all[:, kc]
            beta = beta_all[:, kc]
            qa = (q * jnp.exp(b - beta)).astype(BF16)
            rows = []
            for blk in range(c // GLA_SUB):
                r_lo, r_hi = blk * GLA_SUB, (blk + 1) * GLA_SUB
                e = jnp.where(krow < r_lo, beta[r_lo:r_lo + 1, :] - b, EXP_TO_ZERO)
                ka = (k * jnp.exp(e)).astype(BF16)
                sc = lax.dot_general(qa[r_lo:r_hi, :], ka, (((1,), (1,)), ((), ())), preferred_element_type=F32)
                qb, bb = q[r_lo:r_hi, :], b[r_lo:r_hi, :]
                for j in range(r_lo, r_hi):
                    w = jnp.exp(jnp.minimum(bb - b[j:j + 1, :], 0.0))
                    sc = jnp.where(kcol == j, jnp.sum(qb * k[j:j + 1, :] * w, axis=1, keepdims=True), sc)
                rows.append(sc)
            p = jnp.where(causal, jnp.concatenate(rows, axis=0), 0.0).astype(BF16)
            s_old = s_ref[hd]
            o = _dot(p, v) + _dot((q * jnp.exp(b)).astype(BF16), s_old.astype(BF16))
            b_last = b[c - 1:c, :]
            k_dec = (k * jnp.exp(b_last - b)).astype(BF16)
            upd = lax.dot_general(k_dec, v, (((0,), (0,)), ((), ())), preferred_element_type=F32)
            s_ref[hd] = s_old * _decay_column(b_last) + upd
            o_ref[pl.ds(r0, c), vc] = _rms(o, gn).astype(BF16)
        return carry

    lax.fori_loop(0, ROW_TILE // c, chunk, 0)


def _interleave(tasks):
    tasks = list(tasks)
    while tasks:
        for t in list(tasks):
            try:
                next(t)
            except StopIteration:
                tasks.remove(t)


def _mix_in_task(x_ref, g_ref, wq_ref, wk_ref, wv_ref, wr_ref, wa_ref, wu_ref, wga_ref, wgb_ref, wal_ref, bal_ref,
                 sr_ref, z_ref, sga_ref, sgb_ref, ubuf, tile_in_seq, keep_halo, gla_inputs):
    h = _rms(x_ref[...], g_ref[...]).astype(BF16)
    a_code = _dot(h, wa_ref[...]).astype(BF16)
    yield
    u = _dot(h, wu_ref[...])
    yield
    r = _dot(h, wr_ref[...])
    sr_ref[...] = (r * _sigmoid(r)).astype(BF16)
    yield
    zg = _dot(a_code, wal_ref[...]) + bal_ref[...]
    yield
    sga_ref[...] = _sigmoid(_dot(h, wga_ref[...])).astype(BF16)
    yield
    sgb_ref[...] = _sigmoid(_dot(h, wgb_ref[...])).astype(BF16)
    yield

    ubuf[POOL_HALO:POOL_HALO + ROW_TILE, :] = u
    pos = lax.broadcasted_iota(jnp.int32, (ROW_TILE, 1), 0) + tile_in_seq * ROW_TILE
    for g, w in enumerate(POOL_WINDOWS):
        cols = slice(g * POOL_GC, (g + 1) * POOL_GC)
        s = ubuf[:, cols]
        sh = 1
        while sh < w:
            s = s + pltpu.roll(s, sh, 0)
            sh *= 2
        cnt = jnp.minimum(pos + 1, w).astype(F32)
        z_ref[:, cols] = (s[POOL_HALO:, :] / cnt - u[:, cols]).astype(BF16)
    ubuf[0:POOL_HALO, :] = jnp.where(keep_halo, ubuf[0:POOL_HALO, :], ubuf[ROW_TILE:ROW_TILE + POOL_HALO, :])

    q = (_dot(h, wq_ref[...]) * (GLA_DK ** -0.5)).astype(BF16)
    yield
    v = _dot(h, wv_ref[...]).astype(BF16)
    yield
    k = _dot(h, wk_ref[...]).astype(BF16)
    la = (jnp.minimum(zg, 0.0) - jnp.log(1.0 + jnp.exp(-jnp.abs(zg)))) * (1.0 / GLA_GATE_TEMP)
    span = None
    for r0 in range(0, ROW_TILE, GLA_WIDE):
        sp = jnp.max(-jnp.sum(la[r0:r0 + GLA_WIDE, :], axis=0, keepdims=True))
        span = sp if span is None else jnp.maximum(span, sp)
    gla_inputs.extend((q, k, v, la, span))


def _mix_gla_kernel(x_ref, g_ref, wt_a_ref, wt_b_ref, wal_ref, bal_ref, gn_ref,
                    sr_ref, z_ref, sga_ref, sgb_ref, o_ref,
                    q_s, k_s, v_s, la_s, s_ref, ubuf, span_s,
                    wq_ref, wk_ref, wv_ref, wr_ref, wa_ref, wu_ref, wga_ref, wgb_ref, *, tiles_per_seq, n_tiles):
    step = pl.program_id(0)
    mix_tile = jnp.minimum(step, n_tiles - 1)
    gla_tile = step - 1

    @pl.when(step == 0)
    def _():
        q_s[...] = jnp.zeros_like(q_s)
        k_s[...] = jnp.zeros_like(k_s)
        v_s[...] = jnp.zeros_like(v_s)
        la_s[...] = jnp.zeros_like(la_s)
        span_s[0] = 0.0
        lo = 0
        for w_ref, n in zip((wq_ref, wk_ref, wv_ref, wr_ref), W_IN_SIZES[:W_IN_SPLIT]):
            w_ref[...] = wt_a_ref[lo:lo + n, :].T
            lo += n
        lane = lax.broadcasted_iota(jnp.int32, (D_MODEL, LANE), 1)
        wa_ref[...] = jnp.where(lane < GLA_RANK, wt_b_ref[0:LANE, :].T, jnp.zeros((), BF16))
        lo = GLA_RANK
        for w_ref, n in zip((wu_ref, wga_ref, wgb_ref), W_IN_SIZES[W_IN_SPLIT + 1:]):
            w_ref[...] = wt_b_ref[lo:lo + n, :].T
            lo += n

    @pl.when(mix_tile % tiles_per_seq == 0)
    def _():
        ubuf[0:POOL_HALO, :] = jnp.zeros((POOL_HALO, POOL_WIDTH), F32)

    @pl.when(jnp.logical_or(step == 0, gla_tile % tiles_per_seq == 0))
    def _():
        s_ref[...] = jnp.zeros_like(s_ref)

    keep_halo = step >= n_tiles - 1
    mix_args = (x_ref, g_ref, wq_ref, wk_ref, wv_ref, wr_ref, wa_ref, wu_ref, wga_ref, wgb_ref, wal_ref, bal_ref,
                sr_ref, z_ref, sga_ref, sgb_ref, ubuf, mix_tile % tiles_per_seq, keep_halo)
    gla_args = (q_s, k_s, v_s, la_s, gn_ref[...], o_ref, s_ref)

    def run(gla_tasks):
        nxt = []
        _interleave(gla_tasks + [_mix_in_task(*mix_args, nxt)])
        q_s[...], k_s[...], v_s[...], la_s[...], span_s[0] = nxt

    wide_ok = span_s[0] < GLA_WIDE_MAX_SPAN

    @pl.when(wide_ok)
    def _():
        run([_gla_wide_task(*gla_args)])

    @pl.when(jnp.logical_not(wide_ok))
    def _():
        _gla_anchored_chunks(*gla_args)
        run([])


def _mix_gla(x, g, wt_a, wt_b, wal, bal, gn, seq):
    t = x.shape[0]
    n_tiles = t // ROW_TILE

    def cur(n):
        return pl.BlockSpec((ROW_TILE, n), lambda i: (jnp.minimum(i, n_tiles - 1), 0))

    prev = pl.BlockSpec((ROW_TILE, GLA_VW), lambda i: (jnp.maximum(i - 1, 0), 0))

    def out(n):
        return jax.ShapeDtypeStruct((t, n), BF16)

    outs, _ = _call(
        functools.partial(_mix_gla_kernel, tiles_per_seq=seq // ROW_TILE, n_tiles=n_tiles),
        (x, g, wt_a, wt_b, wal, bal, gn),
        [cur(D_MODEL), _resident((1, D_MODEL)), _resident(wt_a.shape), _resident(wt_b.shape),
         _resident((LANE, GLA_QK)), _resident((1, GLA_QK)), _resident((1, GLA_DV))],
        [cur(GLA_VW), cur(POOL_WIDTH), cur(D_MODEL), cur(D_MODEL), prev],
        [out(GLA_VW), out(POOL_WIDTH), out(D_MODEL), out(D_MODEL), out(GLA_VW)],
        grid=(n_tiles + 1,),
        scratch_shapes=[pltpu.VMEM((ROW_TILE, GLA_QK), BF16), pltpu.VMEM((ROW_TILE, GLA_QK), BF16),
                        pltpu.VMEM((ROW_TILE, GLA_VW), BF16), pltpu.VMEM((ROW_TILE, GLA_QK), F32),
                        pltpu.VMEM((GLA_HEADS, GLA_DK, GLA_DV), F32),
                        pltpu.VMEM((POOL_HALO + ROW_TILE, POOL_WIDTH), F32),
                        pltpu.SMEM((1,), F32),
                        *(pltpu.VMEM((D_MODEL, LANE if n == GLA_RANK else n), BF16) for n in W_IN_SIZES)],
        name="mix_gla")
    return outs


def _mem_kv_kernel(m_ref, g_ref, wk_ref, wv_ref, k_ref, v_ref, wk_s, wv_s):
    @pl.when(pl.program_id(0) == 0)
    def _():
        wk_s[...] = wk_ref[...].astype(BF16)
        wv_s[...] = wv_ref[...].astype(BF16)

    m = _rms(m_ref[...], g_ref[...]).astype(BF16)
    k_ref[...] = _dot(m, wk_s[...]).astype(BF16)
    v_ref[...] = _dot(m, wv_s[...]).astype(BF16)


def _mem_kv(mem, g, wk, wv, mem_len, make_jobs):
    t = mem.shape[0]
    steps = t // mem_len
    row = pl.BlockSpec((mem_len, D_MODEL), lambda i: (i, 0))
    out = jax.ShapeDtypeStruct((t, D_MODEL), BF16)
    return _call(
        _mem_kv_kernel, (mem, g, wk, wv),
        [row, _resident((1, D_MODEL)), _resident((D_MODEL, D_MODEL)), _resident((D_MODEL, D_MODEL))],
        [row, row], [out, out], make_jobs(steps), grid=(steps,),
        scratch_shapes=[pltpu.VMEM((D_MODEL, D_MODEL), BF16)] * 2, name="mem_kv")


def _mix_xattn_kernel(x_ref, o_ref, sr_ref, z_ref, sga_ref, sgb_ref, wua_ref, pm_ref, ps_ref, wub_ref, wmo_ref,
                      xg_ref, wq_ref, k_ref, v_ref, wo_ref, y_ref):
    zs = []
    for g in range(POOL_GROUPS):
        cols = slice(g * POOL_GC, (g + 1) * POOL_GC)
        zs.append((_dot(z_ref[:, cols], pm_ref[g].astype(BF16)) * ps_ref[:, cols]).astype(BF16))
    ya = _dot(o_ref[...] * sr_ref[...], wua_ref[...])
    yb = _dot(jnp.concatenate(zs, axis=1), wub_ref[...])
    merged = (sga_ref[...].astype(F32) * ya + sgb_ref[...].astype(F32) * yb).astype(BF16)
    x = x_ref[...] + _dot(merged, wmo_ref[...])

    inv_rms = lax.rsqrt(jnp.mean(x * x, axis=-1, keepdims=True) + EPS)
    q = (_dot((x * xg_ref[...]).astype(BF16), wq_ref[...]) * (inv_rms * (XA_HD ** -0.5))).astype(BF16)

    def scores(hd):
        cols = slice(hd * XA_HD, (hd + 1) * XA_HD)
        return lax.dot_general(q[:, cols], k_ref[:, cols], (((1,), (1,)), ((), ())), preferred_element_type=F32)

    outs = []
    s = scores(0)
    for hd in range(XA_HEADS):
        s_next = scores(hd + 1) if hd + 1 < XA_HEADS else None
        e = jnp.exp(s - jnp.max(s, axis=-1, keepdims=True))
        o = _dot(e.astype(BF16), v_ref[:, hd * XA_HD:(hd + 1) * XA_HD])
        outs.append((o / jnp.sum(e, axis=-1, keepdims=True)).astype(BF16))
        s = s_next
    y_ref[...] = x + _dot(jnp.concatenate(outs, axis=1), wo_ref[...])


def _mix_xattn(x, o, sr, z, sga, sgb, wua, pm, ps, wub, wmo, xg, wq, km, vm, wo, seq, mem_len):
    t = x.shape[0]
    steps = t // ROW_TILE
    tiles_per_seq = seq // ROW_TILE

    def row(n):
        return pl.BlockSpec((ROW_TILE, n), lambda i: (i, 0))

    mem = pl.BlockSpec((mem_len, D_MODEL), lambda i: (i // tiles_per_seq, 0))
    (y,), _ = _call(
        _mix_xattn_kernel, (x, o, sr, z, sga, sgb, wua, pm, ps, wub, wmo, xg, wq, km, vm, wo),
        [row(D_MODEL), row(GLA_VW), row(GLA_VW), row(POOL_WIDTH), row(D_MODEL), row(D_MODEL),
         _resident((GLA_VW, D_MODEL)), _resident((POOL_GROUPS, POOL_GC, POOL_GC)),
         _resident((1, POOL_WIDTH)), _resident((POOL_WIDTH, D_MODEL)), _resident((D_MODEL, D_MODEL)),
         _resident((1, D_MODEL)), _resident((D_MODEL, D_MODEL)), mem, mem, _resident((D_MODEL, D_MODEL))],
        [row(D_MODEL)], [jax.ShapeDtypeStruct((t, D_MODEL), F32)],
        grid=(steps,), name="mix_xattn")
    return y


def _layer(x, mem, ffn1_norm, ffn1_w1, ffn1_w3, ffn1_w2, mix_norm, w_in, w_alpha, b_alpha, gla_head_norm,
           w_up_a, pool_mix, pool_scale, w_up_b, w_mix_out, xa_norm, mem_norm, xa_wq, xa_wk, xa_wv, xa_wo,
           ffn2_norm, ffn2_w1, ffn2_w3, ffn2_w2, final_norm, last):
    batch, seq, _ = x.shape
    mem_len = mem.shape[1]
    assert seq % ROW_TILE == 0 and ROW_TILE % GLA_CHUNK == 0 and ROW_TILE % GLA_WIDE == 0

    def vec(a):
        return a.reshape(1, -1).astype(F32)

    def casts(*ws):
        return lambda n_steps: tuple(_cast_rows_job(w, n_steps) for w in ws)

    (km, vm), ffn1_w = _mem_kv(mem.reshape(batch * mem_len, D_MODEL), vec(mem_norm), xa_wk, xa_wv, mem_len,
                               casts(ffn1_w1, ffn1_w3, ffn1_w2))
    later = (w_up_a, w_up_b, w_mix_out, xa_wq, xa_wo, ffn2_w1, ffn2_w3, ffn2_w2)
    x1, cast = _ffn(x.reshape(batch * seq, D_MODEL), vec(ffn1_norm), *ffn1_w, vec(final_norm), False,
                    lambda n_steps: (_w_in_job(jnp.swapaxes(w_in, 0, 1), n_steps), *casts(*later)(n_steps)))
    wt_a, wt_b, wua, wub, wmo, xwq, xwo, *ffn2_w = cast
    wal = jnp.pad(w_alpha.astype(BF16), ((0, LANE - GLA_RANK), (0, 0)))
    sr, z, sga, sgb, o = _mix_gla(x1, vec(mix_norm), wt_a, wt_b, wal, vec(b_alpha), vec(gla_head_norm), seq)
    x3 = _mix_xattn(x1, o, sr, z, sga, sgb, wua, pool_mix, vec(pool_scale), wub, wmo, vec(xa_norm), xwq, km, vm, xwo,
                    seq, mem_len)
    x4, _ = _ffn(x3, vec(ffn2_norm), *ffn2_w, vec(final_norm), last)
    return x4.reshape(batch, seq, D_MODEL)


def kernel(x, mem, ffn1_norm, ffn1_w1, ffn1_w3, ffn1_w2, mix_norm, w_in, w_alpha, b_alpha, gla_head_norm, w_up_a,
           pool_mix, pool_scale, w_up_b, w_mix_out, xa_norm, mem_norm, xa_wq, xa_wk, xa_wv, xa_wo, ffn2_norm,
           ffn2_w1, ffn2_w3, ffn2_w2, final_norm):
    depth = ffn1_norm.shape[0]
    for l in range(depth):
        last = l == depth - 1
        x = _layer(x, mem, ffn1_norm[l], ffn1_w1[l], ffn1_w3[l], ffn1_w2[l], mix_norm[l], w_in[l], w_alpha[l],
                   b_alpha[l], gla_head_norm[l], w_up_a[l], pool_mix[l], pool_scale[l], w_up_b[l], w_mix_out[l],
                   xa_norm[l], mem_norm[l], xa_wq[l], xa_wk[l], xa_wv[l], xa_wo[l], ffn2_norm[l], ffn2_w1[l],
                   ffn2_w3[l], ffn2_w2[l], final_norm, last)
    return x
```

```python
import functools
from typing import Callable, NamedTuple

import jax
import jax.numpy as jnp
from jax import lax
from jax.experimental import pallas as pl
from jax.experimental.pallas import tpu as pltpu

F32 = jnp.float32
BF16 = jnp.bfloat16

D_MODEL = 1024
D_FF = 2816
EPS = 1e-6
GLA_HEADS = 4
GLA_DK = 128
GLA_DV = 256
GLA_QK = GLA_HEADS * GLA_DK
GLA_VW = GLA_HEADS * GLA_DV
GLA_RANK = 16
GLA_GATE_TEMP = 16.0
GLA_CHUNK = 64
GLA_SUB = 16
GLA_WIDE = 256
GLA_WIDE_MAX_SPAN = 60.0
EXP_TO_ZERO = -1e30
POOL_GROUPS = 4
POOL_GC = 128
POOL_WIDTH = POOL_GROUPS * POOL_GC
POOL_WINDOWS = (2, 4, 8, 16)
POOL_HALO = 16
XA_HEADS = 4
XA_HD = 256

LANE = 128
BF16_ROWS = 16
W_IN_SIZES = (GLA_QK, GLA_QK, GLA_VW, GLA_VW, GLA_RANK, POOL_WIDTH, D_MODEL, D_MODEL)
W_IN_SPLIT = 4
ROW_TILE = 512
FFN_ROWS = 1024
FF_SPLITS = (0, 768, 1536, 2304, D_FF)
VMEM_LIMIT = 56 * 1024 * 1024


def _rms(x, g):
    return x * lax.rsqrt(jnp.mean(x * x, axis=-1, keepdims=True) + EPS) * g


def _sigmoid(x):
    return 0.5 * jnp.tanh(0.5 * x) + 0.5


def _dot(a, b):
    return jnp.dot(a, b, preferred_element_type=F32)


def _resident(shape):
    nd = len(shape)
    return pl.BlockSpec(shape, lambda *_: (0,) * nd, pipeline_mode=pl.Buffered(1))


def _params():
    return pltpu.CompilerParams(dimension_semantics=("arbitrary",), vmem_limit_bytes=VMEM_LIMIT)


class _CastJob(NamedTuple):
    inputs: tuple
    in_specs: tuple
    out_specs: tuple
    out_shapes: tuple
    body: Callable


def _cast_rows_job(w, n_steps):
    rows, cols = w.shape
    per_step = -(-rows // n_steps)
    block = next(b for b in range(BF16_ROWS, rows + 1, BF16_ROWS) if rows % b == 0 and b >= per_step)
    n_blocks = rows // block
    spec = pl.BlockSpec((block, cols), lambda i: ((i * n_blocks) // n_steps, 0))

    def body(ins, outs):
        outs[0][...] = ins[0][...].astype(BF16)

    return _CastJob((w,), (spec,), (spec,), (jax.ShapeDtypeStruct((rows, cols), BF16),), body)


def _w_in_job(w_in_t, n_steps):
    cols = w_in_t.shape[1]
    rows_a = sum(W_IN_SIZES[:W_IN_SPLIT])
    rows_b = sum(W_IN_SIZES[W_IN_SPLIT:])
    assert w_in_t.shape[0] == rows_a + rows_b and rows_a % n_steps == 0
    blk_a = rows_a // n_steps
    per_b = -(-rows_b // n_steps)
    blk_b = next(r for r in range(BF16_ROWS, rows_b + 1, BF16_ROWS) if rows_b % r == 0 and r >= per_b)
    last_b = rows_b // blk_b - 1
    assert blk_a % BF16_ROWS == 0
    in_a = pl.BlockSpec((blk_a, cols), lambda i: (i, 0))
    in_b = pl.BlockSpec((pl.Element(blk_b), pl.Element(cols)),
                        lambda i: (pl.multiple_of(rows_a + blk_b * jnp.minimum(i, last_b), BF16_ROWS), 0))
    out_b = pl.BlockSpec((blk_b, cols), lambda i: (jnp.minimum(i, last_b), 0))

    def body(ins, outs):
        for i_ref, o_ref in zip(ins, outs):
            o_ref[...] = i_ref[...].astype(BF16)

    return _CastJob((w_in_t, w_in_t), (in_a, in_b), (in_a, out_b),
                    (jax.ShapeDtypeStruct((rows_a, cols), BF16), jax.ShapeDtypeStruct((rows_b, cols), BF16)), body)


def _call(kernel_fn, args, in_specs, out_specs, out_shape, jobs=(), *, grid, scratch_shapes=(), name):
    n_in, n_out = len(args), len(out_specs)
    job_args = [a for j in jobs for a in j.inputs]
    job_outs = [o for j in jobs for o in j.out_shapes]

    def body(*refs):
        j_in = refs[n_in:n_in + len(job_args)]
        first_out = n_in + len(job_args)
        j_out = refs[first_out + n_out:first_out + n_out + len(job_outs)]
        kernel_fn(*refs[:n_in], *refs[first_out:first_out + n_out], *refs[first_out + n_out + len(job_outs):])
        for j in jobs:
            j.body(j_in[:len(j.inputs)], j_out[:len(j.out_shapes)])
            j_in, j_out = j_in[len(j.inputs):], j_out[len(j.out_shapes):]

    outs = pl.pallas_call(
        body,
        grid=grid,
        in_specs=[*in_specs, *(sp for j in jobs for sp in j.in_specs)],
        out_specs=[*out_specs, *(sp for j in jobs for sp in j.out_specs)],
        out_shape=[*out_shape, *job_outs],
        scratch_shapes=list(scratch_shapes),
        compiler_params=_params(),
        name=name,
    )(*args, *job_args)
    return outs[:n_out], outs[n_out:]


def _ffn_kernel(x_ref, g_ref, w1_ref, w3_ref, w2_ref, fg_ref, o_ref, *, final_norm):
    x = x_ref[...]
    inv_rms = lax.rsqrt(jnp.mean(x * x, axis=-1, keepdims=True) + EPS)
    h = (x * g_ref[...]).astype(BF16)
    acc = jnp.zeros_like(x)
    for lo, hi in zip(FF_SPLITS[:-1], FF_SPLITS[1:]):
        a = _dot(h, w1_ref[:, lo:hi]) * inv_rms
        b = _dot(h, w3_ref[:, lo:hi]) * inv_rms
        act = (a * _sigmoid(a) * b).astype(BF16)
        acc = acc + _dot(act, w2_ref[lo:hi, :])
    y = x + 0.5 * acc
    if final_norm:
        y = _rms(y, fg_ref[...])
    o_ref[...] = y


def _ffn(x, g, w1, w3, w2, fg, final_norm, make_jobs=None):
    t = x.shape[0]
    steps = t // FFN_ROWS
    row = pl.BlockSpec((FFN_ROWS, D_MODEL), lambda i: (i, 0))
    (y,), cast = _call(
        functools.partial(_ffn_kernel, final_norm=final_norm),
        (x, g, w1, w3, w2, fg),
        [row, _resident((1, D_MODEL)), _resident((D_MODEL, D_FF)), _resident((D_MODEL, D_FF)),
         _resident((D_FF, D_MODEL)), _resident((1, D_MODEL))],
        [row], [jax.ShapeDtypeStruct((t, D_MODEL), F32)],
        make_jobs(steps) if make_jobs else (),
        grid=(steps,), name="ffn_final" if final_norm else "ffn")
    return y, cast


def _cumsum_rows(mask_bf16, la):
    hi = la.astype(BF16)
    lo = (la - hi.astype(F32)).astype(BF16)
    return _dot(mask_bf16, hi) + _dot(mask_bf16, lo)


def _decay_column(b_last_row):
    return jnp.transpose(jnp.broadcast_to(jnp.exp(b_last_row), (8, GLA_DK)))[:, 0:1]


def _gla_wide_task(q_ref, k_ref, v_ref, la_ref, gn, o_ref, s_ref):
    c = GLA_WIDE
    ri = lax.broadcasted_iota(jnp.int32, (c, c), 0)
    ci = lax.broadcasted_iota(jnp.int32, (c, c), 1)
    causal = ci <= ri
    cum_incl = causal.astype(BF16)
    row_sl = [slice(ic * c, (ic + 1) * c) for ic in range(ROW_TILE // c)]
    b_all = [_dot(cum_incl, la_ref[rows, :].astype(BF16)) for rows in row_sl]
    units = [(ic, hd) for ic in range(len(row_sl)) for hd in range(GLA_HEADS)]

    def factors(ic, hd):
        rows, kc = row_sl[ic], slice(hd * GLA_DK, (hd + 1) * GLA_DK)
        b = b_all[ic][:, kc]
        qg = (q_ref[rows, kc].astype(F32) * jnp.exp(b)).astype(BF16)
        kg = (k_ref[rows, kc].astype(F32) * jnp.exp(-b)).astype(BF16)
        s = lax.dot_general(qg, kg, (((1,), (1,)), ((), ())), preferred_element_type=F32)
        return qg, kg, s, b[c - 1:c, :]

    nxt = factors(*units[0])
    yield
    for n, (ic, hd) in enumerate(units):
        qg, kg, s, b_last = nxt
        if n + 1 < len(units):
            nxt = factors(*units[n + 1])
        rows, vc = row_sl[ic], slice(hd * GLA_DV, (hd + 1) * GLA_DV)
        v = v_ref[rows, vc]
        s_old = s_ref[hd]
        inter = _dot(qg, s_old.astype(BF16))
        upd = lax.dot_general(kg, v, (((0,), (0,)), ((), ())), preferred_element_type=F32)
        p = jnp.where(causal, s, 0.0).astype(BF16)
        o = _dot(p, v) + inter
        s_ref[hd] = (s_old + upd) * _decay_column(b_last)
        o_ref[rows, vc] = _rms(o, gn).astype(BF16)
        yield


def _gla_anchored_chunks(q_ref, k_ref, v_ref, la_ref, gn, o_ref, s_ref):
    c = GLA_CHUNK
    ri = lax.broadcasted_iota(jnp.int32, (c, c), 0)
    ci = lax.broadcasted_iota(jnp.int32, (c, c), 1)
    causal = ci <= ri
    cum_incl = causal.astype(BF16)
    cum_anchor = (ci < (ri // GLA_SUB) * GLA_SUB).astype(BF16)
    krow = lax.broadcasted_iota(jnp.int32, (c, 1), 0)
    kcol = lax.broadcasted_iota(jnp.int32, (GLA_SUB, c), 1)

    def chunk(ic, carry):
        r0 = pl.multiple_of(ic * c, c)
        la = la_ref[pl.ds(r0, c), :]
        b_all = _cumsum_rows(cum_incl, la)
        beta_all = _cumsum_rows(cum_anchor, la)
        for hd in range(GLA_HEADS):
            kc = slice(hd * GLA_DK, (hd + 1) * GLA_DK)
            vc = slice(hd * GLA_DV, (hd + 1) * GLA_DV)
            q = q_ref[pl.ds(r0, c), kc].astype(F32)
            k = k_ref[pl.ds(r0, c), kc].astype(F32)
            v = v_ref[pl.ds(r0, c), vc]
            b = b_all[:, kc]
            beta = beta_all[:, kc]
            qa = (q * jnp.exp(b - beta)).astype(BF16)
            rows = []
            for blk in range(c // GLA_SUB):
                r_lo, r_hi = blk * GLA_SUB, (blk + 1) * GLA_SUB
                e = jnp.where(krow < r_lo, beta[r_lo:r_lo + 1, :] - b, EXP_TO_ZERO)
                ka = (k * jnp.exp(e)).astype(BF16)
                sc = lax.dot_general(qa[r_lo:r_hi, :], ka, (((1,), (1,)), ((), ())), preferred_element_type=F32)
                qb, bb = q[r_lo:r_hi, :], b[r_lo:r_hi, :]
                for j in range(r_lo, r_hi):
                    w = jnp.exp(jnp.minimum(bb - b[j:j + 1, :], 0.0))
                    sc = jnp.where(kcol == j, jnp.sum(qb * k[j:j + 1, :] * w, axis=1, keepdims=True), sc)
                rows.append(sc)
            p = jnp.where(causal, jnp.concatenate(rows, axis=0), 0.0).astype(BF16)
            s_old = s_ref[hd]
            o = _dot(p, v) + _dot((q * jnp.exp(b)).astype(BF16), s_old.astype(BF16))
            b_last = b[c - 1:c, :]
            k_dec = (k * jnp.exp(b_last - b)).astype(BF16)
            upd = lax.dot_general(k_dec, v, (((0,), (0,)), ((), ())), preferred_element_type=F32)
            s_ref[hd] = s_old * _decay_column(b_last) + upd
            o_ref[pl.ds(r0, c), vc] = _rms(o, gn).astype(BF16)
        return carry

    lax.fori_loop(0, ROW_TILE // c, chunk, 0)


def _interleave(tasks):
    tasks = list(tasks)
    while tasks:
        for t in list(tasks):
            try:
                next(t)
            except StopIteration:
                tasks.remove(t)


def _mix_in_task(x_ref, g_ref, wq_ref, wk_ref, wv_ref, wr_ref, wa_ref, wu_ref, wga_ref, wgb_ref, wal_ref, bal_ref,
                 sr_ref, z_ref, sga_ref, sgb_ref, ubuf, tile_in_seq, keep_halo, gla_inputs):
    h = _rms(x_ref[...], g_ref[...]).astype(BF16)
    a_code = _dot(h, wa_ref[...]).astype(BF16)
    yield
    u = _dot(h, wu_ref[...])
    yield
    r = _dot(h, wr_ref[...])
    sr_ref[...] = (r * _sigmoid(r)).astype(BF16)
    yield
    zg = _dot(a_code, wal_ref[...]) + bal_ref[...]
    yield
    sga_ref[...] = _sigmoid(_dot(h, wga_ref[...])).astype(BF16)
    yield
    sgb_ref[...] = _sigmoid(_dot(h, wgb_ref[...])).astype(BF16)
    yield

    ubuf[POOL_HALO:POOL_HALO + ROW_TILE, :] = u
    pos = lax.broadcasted_iota(jnp.int32, (ROW_TILE, 1), 0) + tile_in_seq * ROW_TILE
    for g, w in enumerate(POOL_WINDOWS):
        cols = slice(g * POOL_GC, (g + 1) * POOL_GC)
        s = ubuf[:, cols]
        sh = 1
        while sh < w:
            s = s + pltpu.roll(s, sh, 0)
            sh *= 2
        cnt = jnp.minimum(pos + 1, w).astype(F32)
        z_ref[:, cols] = (s[POOL_HALO:, :] / cnt - u[:, cols]).astype(BF16)
    ubuf[0:POOL_HALO, :] = jnp.where(keep_halo, ubuf[0:POOL_HALO, :], ubuf[ROW_TILE:ROW_TILE + POOL_HALO, :])

    q = (_dot(h, wq_ref[...]) * (GLA_DK ** -0.5)).astype(BF16)
    yield
    v = _dot(h, wv_ref[...]).astype(BF16)
    yield
    k = _dot(h, wk_ref[...]).astype(BF16)
    la = (jnp.minimum(zg, 0.0) - jnp.log(1.0 + jnp.exp(-jnp.abs(zg)))) * (1.0 / GLA_GATE_TEMP)
    span = None
    for r0 in range(0, ROW_TILE, GLA_WIDE):
        sp = jnp.max(-jnp.sum(la[r0:r0 + GLA_WIDE, :], axis=0, keepdims=True))
        span = sp if span is None else jnp.maximum(span, sp)
    gla_inputs.extend((q, k, v, la, span))


def _mix_gla_kernel(x_ref, g_ref, wt_a_ref, wt_b_ref, wal_ref, bal_ref, gn_ref,
                    sr_ref, z_ref, sga_ref, sgb_ref, o_ref,
                    q_s, k_s, v_s, la_s, s_ref, ubuf, span_s,
                    wq_ref, wk_ref, wv_ref, wr_ref, wa_ref, wu_ref, wga_ref, wgb_ref, *, tiles_per_seq, n_tiles):
    step = pl.program_id(0)
    mix_tile = jnp.minimum(step, n_tiles - 1)
    gla_tile = step - 1

    @pl.when(step == 0)
    def _():
        q_s[...] = jnp.zeros_like(q_s)
        k_s[...] = jnp.zeros_like(k_s)
        v_s[...] = jnp.zeros_like(v_s)
        la_s[...] = jnp.zeros_like(la_s)
        span_s[0] = 0.0
        lo = 0
        for w_ref, n in zip((wq_ref, wk_ref, wv_ref, wr_ref), W_IN_SIZES[:W_IN_SPLIT]):
            w_ref[...] = wt_a_ref[lo:lo + n, :].T
            lo += n
        lane = lax.broadcasted_iota(jnp.int32, (D_MODEL, LANE), 1)
        wa_ref[...] = jnp.where(lane < GLA_RANK, wt_b_ref[0:LANE, :].T, jnp.zeros((), BF16))
        lo = GLA_RANK
        for w_ref, n in zip((wu_ref, wga_ref, wgb_ref), W_IN_SIZES[W_IN_SPLIT + 1:]):
            w_ref[...] = wt_b_ref[lo:lo + n, :].T
            lo += n

    @pl.when(mix_tile % tiles_per_seq == 0)
    def _():
        ubuf[0:POOL_HALO, :] = jnp.zeros((POOL_HALO, POOL_WIDTH), F32)

    @pl.when(jnp.logical_or(step == 0, gla_tile % tiles_per_seq == 0))
    def _():
        s_ref[...] = jnp.zeros_like(s_ref)

    keep_halo = step >= n_tiles - 1
    mix_args = (x_ref, g_ref, wq_ref, wk_ref, wv_ref, wr_ref, wa_ref, wu_ref, wga_ref, wgb_ref, wal_ref, bal_ref,
                sr_ref, z_ref, sga_ref, sgb_ref, ubuf, mix_tile % tiles_per_seq, keep_halo)
    gla_args = (q_s, k_s, v_s, la_s, gn_ref[...], o_ref, s_ref)

    def run(gla_tasks):
        nxt = []
        _interleave(gla_tasks + [_mix_in_task(*mix_args, nxt)])
        q_s[...], k_s[...], v_s[...], la_s[...], span_s[0] = nxt

    wide_ok = span_s[0] < GLA_WIDE_MAX_SPAN

    @pl.when(wide_ok)
    def _():
        run([_gla_wide_task(*gla_args)])

    @pl.when(jnp.logical_not(wide_ok))
    def _():
        _gla_anchored_chunks(*gla_args)
        run([])


def _mix_gla(x, g, wt_a, wt_b, wal, bal, gn, seq):
    t = x.shape[0]
    n_tiles = t // ROW_TILE

    def cur(n):
        return pl.BlockSpec((ROW_TILE, n), lambda i: (jnp.minimum(i, n_tiles - 1), 0))

    prev = pl.BlockSpec((ROW_TILE, GLA_VW), lambda i: (jnp.maximum(i - 1, 0), 0))

    def out(n):
        return jax.ShapeDtypeStruct((t, n), BF16)

    outs, _ = _call(
        functools.partial(_mix_gla_kernel, tiles_per_seq=seq // ROW_TILE, n_tiles=n_tiles),
        (x, g, wt_a, wt_b, wal, bal, gn),
        [cur(D_MODEL), _resident((1, D_MODEL)), _resident(wt_a.shape), _resident(wt_b.shape),
         _resident((LANE, GLA_QK)), _resident((1, GLA_QK)), _resident((1, GLA_DV))],
        [cur(GLA_VW), cur(POOL_WIDTH), cur(D_MODEL), cur(D_MODEL), prev],
        [out(GLA_VW), out(POOL_WIDTH), out(D_MODEL), out(D_MODEL), out(GLA_VW)],
        grid=(n_tiles + 1,),
        scratch_shapes=[pltpu.VMEM((ROW_TILE, GLA_QK), BF16), pltpu.VMEM((ROW_TILE, GLA_QK), BF16),
                        pltpu.VMEM((ROW_TILE, GLA_VW), BF16), pltpu.VMEM((ROW_TILE, GLA_QK), F32),
                        pltpu.VMEM((GLA_HEADS, GLA_DK, GLA_DV), F32),
                        pltpu.VMEM((POOL_HALO + ROW_TILE, POOL_WIDTH), F32),
                        pltpu.SMEM((1,), F32),
                        *(pltpu.VMEM((D_MODEL, LANE if n == GLA_RANK else n), BF16) for n in W_IN_SIZES)],
        name="mix_gla")
    return outs


def _mem_kv_kernel(m_ref, g_ref, wk_ref, wv_ref, k_ref, v_ref, wk_s, wv_s):
    @pl.when(pl.program_id(0) == 0)
    def _():
        wk_s[...] = wk_ref[...].astype(BF16)
        wv_s[...] = wv_ref[...].astype(BF16)

    m = _rms(m_ref[...], g_ref[...]).astype(BF16)
    k_ref[...] = _dot(m, wk_s[...]).astype(BF16)
    v_ref[...] = _dot(m, wv_s[...]).astype(BF16)


def _mem_kv(mem, g, wk, wv, mem_len, make_jobs):
    t = mem.shape[0]
    steps = t // mem_len
    row = pl.BlockSpec((mem_len, D_MODEL), lambda i: (i, 0))
    out = jax.ShapeDtypeStruct((t, D_MODEL), BF16)
    return _call(
        _mem_kv_kernel, (mem, g, wk, wv),
        [row, _resident((1, D_MODEL)), _resident((D_MODEL, D_MODEL)), _resident((D_MODEL, D_MODEL))],
        [row, row], [out, out], make_jobs(steps), grid=(steps,),
        scratch_shapes=[pltpu.VMEM((D_MODEL, D_MODEL), BF16)] * 2, name="mem_kv")


def _mix_xattn_kernel(x_ref, o_ref, sr_ref, z_ref, sga_ref, sgb_ref, wua_ref, pm_ref, ps_ref, wub_ref, wmo_ref,
                      xg_ref, wq_ref, k_ref, v_ref, wo_ref, y_ref):
    zs = []
    for g in range(POOL_GROUPS):
        cols = slice(g * POOL_GC, (g + 1) * POOL_GC)
        zs.append((_dot(z_ref[:, cols], pm_ref[g].astype(BF16)) * ps_ref[:, cols]).astype(BF16))
    ya = _dot(o_ref[...] * sr_ref[...], wua_ref[...])
    yb = _dot(jnp.concatenate(zs, axis=1), wub_ref[...])
    merged = (sga_ref[...].astype(F32) * ya + sgb_ref[...].astype(F32) * yb).astype(BF16)
    x = x_ref[...] + _dot(merged, wmo_ref[...])

    inv_rms = lax.rsqrt(jnp.mean(x * x, axis=-1, keepdims=True) + EPS)
    q = (_dot((x * xg_ref[...]).astype(BF16), wq_ref[...]) * (inv_rms * (XA_HD ** -0.5))).astype(BF16)

    def scores(hd):
        cols = slice(hd * XA_HD, (hd + 1) * XA_HD)
        return lax.dot_general(q[:, cols], k_ref[:, cols], (((1,), (1,)), ((), ())), preferred_element_type=F32)

    outs = []
    s = scores(0)
    for hd in range(XA_HEADS):
        s_next = scores(hd + 1) if hd + 1 < XA_HEADS else None
        e = jnp.exp(s - jnp.max(s, axis=-1, keepdims=True))
        o = _dot(e.astype(BF16), v_ref[:, hd * XA_HD:(hd + 1) * XA_HD])
        outs.append((o / jnp.sum(e, axis=-1, keepdims=True)).astype(BF16))
        s = s_next
    y_ref[...] = x + _dot(jnp.concatenate(outs, axis=1), wo_ref[...])


def _mix_xattn(x, o, sr, z, sga, sgb, wua, pm, ps, wub, wmo, xg, wq, km, vm, wo, seq, mem_len):
    t = x.shape[0]
    steps = t // ROW_TILE
    tiles_per_seq = seq // ROW_TILE

    def row(n):
        return pl.BlockSpec((ROW_TILE, n), lambda i: (i, 0))

    mem = pl.BlockSpec((mem_len, D_MODEL), lambda i: (i // tiles_per_seq, 0))
    (y,), _ = _call(
        _mix_xattn_kernel, (x, o, sr, z, sga, sgb, wua, pm, ps, wub, wmo, xg, wq, km, vm, wo),
        [row(D_MODEL), row(GLA_VW), row(GLA_VW), row(POOL_WIDTH), row(D_MODEL), row(D_MODEL),
         _resident((GLA_VW, D_MODEL)), _resident((POOL_GROUPS, POOL_GC, POOL_GC)),
         _resident((1, POOL_WIDTH)), _resident((POOL_WIDTH, D_MODEL)), _resident((D_MODEL, D_MODEL)),
         _resident((1, D_MODEL)), _resident((D_MODEL, D_MODEL)), mem, mem, _resident((D_MODEL, D_MODEL))],
        [row(D_MODEL)], [jax.ShapeDtypeStruct((t, D_MODEL), F32)],
        grid=(steps,), name="mix_xattn")
    return y


def _layer(x, mem, ffn1_norm, ffn1_w1, ffn1_w3, ffn1_w2, mix_norm, w_in, w_alpha, b_alpha, gla_head_norm,
           w_up_a, pool_mix, pool_scale, w_up_b, w_mix_out, xa_norm, mem_norm, xa_wq, xa_wk, xa_wv, xa_wo,
           ffn2_norm, ffn2_w1, ffn2_w3, ffn2_w2, final_norm, last):
    batch, seq, _ = x.shape
    mem_len = mem.shape[1]
    assert seq % ROW_TILE == 0 and ROW_TILE % GLA_CHUNK == 0 and ROW_TILE % GLA_WIDE == 0

    def vec(a):
        return a.reshape(1, -1).astype(F32)

    def casts(*ws):
        return lambda n_steps: tuple(_cast_rows_job(w, n_steps) for w in ws)

    (km, vm), ffn1_w = _mem_kv(mem.reshape(batch * mem_len, D_MODEL), vec(mem_norm), xa_wk, xa_wv, mem_len,
                               casts(ffn1_w1, ffn1_w3, ffn1_w2))
    later = (w_up_a, w_up_b, w_mix_out, xa_wq, xa_wo, ffn2_w1, ffn2_w3, ffn2_w2)
    x1, cast = _ffn(x.reshape(batch * seq, D_MODEL), vec(ffn1_norm), *ffn1_w, vec(final_norm), False,
                    lambda n_steps: (_w_in_job(jnp.swapaxes(w_in, 0, 1), n_steps), *casts(*later)(n_steps)))
    wt_a, wt_b, wua, wub, wmo, xwq, xwo, *ffn2_w = cast
    wal = jnp.pad(w_alpha.astype(BF16), ((0, LANE - GLA_RANK), (0, 0)))
    sr, z, sga, sgb, o = _mix_gla(x1, vec(mix_norm), wt_a, wt_b, wal, vec(b_alpha), vec(gla_head_norm), seq)
    x3 = _mix_xattn(x1, o, sr, z, sga, sgb, wua, pool_mix, vec(pool_scale), wub, wmo, vec(xa_norm), xwq, km, vm, xwo,
                    seq, mem_len)
    x4, _ = _ffn(x3, vec(ffn2_norm), *ffn2_w, vec(final_norm), last)
    return x4.reshape(batch, seq, D_MODEL)


def kernel(x, mem, ffn1_norm, ffn1_w1, ffn1_w3, ffn1_w2, mix_norm, w_in, w_alpha, b_alpha, gla_head_norm, w_up_a,
           pool_mix, pool_scale, w_up_b, w_mix_out, xa_norm, mem_norm, xa_wq, xa_wk, xa_wv, xa_wo, ffn2_norm,
           ffn2_w1, ffn2_w3, ffn2_w2, final_norm):
    depth = ffn1_norm.shape[0]
    for l in range(depth):
        last = l == depth - 1
        x = _layer(x, mem, ffn1_norm[l], ffn1_w1[l], ffn1_w3[l], ffn1_w2[l], mix_norm[l], w_in[l], w_alpha[l],
                   b_alpha[l], gla_head_norm[l], w_up_a[l], pool_mix[l], pool_scale[l], w_up_b[l], w_mix_out[l],
                   xa_norm[l], mem_norm[l], xa_wq[l], xa_wk[l], xa_wv[l], xa_wo[l], ffn2_norm[l], ffn2_w1[l],
                   ffn2_w3[l], ffn2_w2[l], final_norm, last)
    return x
```

```python
import functools
from typing import Callable, NamedTuple

import jax
import jax.numpy as jnp
from jax import lax
from jax.experimental import pallas as pl
from jax.experimental.pallas import tpu as pltpu

F32 = jnp.float32
BF16 = jnp.bfloat16

D_MODEL = 1024
D_FF = 2816
EPS = 1e-6
GLA_HEADS = 4
GLA_DK = 128
GLA_DV = 256
GLA_QK = GLA_HEADS * GLA_DK
GLA_VW = GLA_HEADS * GLA_DV
GLA_RANK = 16
GLA_GATE_TEMP = 16.0
GLA_CHUNK = 64
GLA_SUB = 16
GLA_WIDE = 256
GLA_WIDE_MAX_SPAN = 60.0
EXP_TO_ZERO = -1e30
POOL_GROUPS = 4
POOL_GC = 128
POOL_WIDTH = POOL_GROUPS * POOL_GC
POOL_WINDOWS = (2, 4, 8, 16)
POOL_HALO = 16
XA_HEADS = 4
XA_HD = 256

LANE = 128
BF16_ROWS = 16
W_IN_SIZES = (GLA_QK, GLA_QK, GLA_VW, GLA_VW, GLA_RANK, POOL_WIDTH, D_MODEL, D_MODEL)
W_IN_SPLIT = 4
ROW_TILE = 512
XATTN_ROWS = 1024
FFN_ROWS = 1024
FF_SPLITS = (0, 768, 1536, 2304, D_FF)
VMEM_LIMIT = 56 * 1024 * 1024


def _rms(x, g):
    return x * lax.rsqrt(jnp.mean(x * x, axis=-1, keepdims=True) + EPS) * g


def _sigmoid(x):
    return 0.5 * jnp.tanh(0.5 * x) + 0.5


def _dot(a, b):
    return jnp.dot(a, b, preferred_element_type=F32)


def _resident(shape):
    nd = len(shape)
    return pl.BlockSpec(shape, lambda *_: (0,) * nd, pipeline_mode=pl.Buffered(1))


def _params():
    return pltpu.CompilerParams(dimension_semantics=("arbitrary",), vmem_limit_bytes=VMEM_LIMIT)


class _CastJob(NamedTuple):
    inputs: tuple
    in_specs: tuple
    out_specs: tuple
    out_shapes: tuple
    body: Callable


def _cast_rows_job(w, n_steps):
    rows, cols = w.shape
    per_step = -(-rows // n_steps)
    block = next(b for b in range(BF16_ROWS, rows + 1, BF16_ROWS) if rows % b == 0 and b >= per_step)
    n_blocks = rows // block
    spec = pl.BlockSpec((block, cols), lambda i: ((i * n_blocks) // n_steps, 0))

    def body(ins, outs):
        outs[0][...] = ins[0][...].astype(BF16)

    return _CastJob((w,), (spec,), (spec,), (jax.ShapeDtypeStruct((rows, cols), BF16),), body)


def _w_in_job(w_in_t, n_steps):
    cols = w_in_t.shape[1]
    rows_a = sum(W_IN_SIZES[:W_IN_SPLIT])
    rows_b = sum(W_IN_SIZES[W_IN_SPLIT:])
    assert w_in_t.shape[0] == rows_a + rows_b and rows_a % n_steps == 0
    blk_a = rows_a // n_steps
    per_b = -(-rows_b // n_steps)
    blk_b = next(r for r in range(BF16_ROWS, rows_b + 1, BF16_ROWS) if rows_b % r == 0 and r >= per_b)
    last_b = rows_b // blk_b - 1
    assert blk_a % BF16_ROWS == 0
    in_a = pl.BlockSpec((blk_a, cols), lambda i: (i, 0))
    in_b = pl.BlockSpec((pl.Element(blk_b), pl.Element(cols)),
                        lambda i: (pl.multiple_of(rows_a + blk_b * jnp.minimum(i, last_b), BF16_ROWS), 0))
    out_b = pl.BlockSpec((blk_b, cols), lambda i: (jnp.minimum(i, last_b), 0))

    def body(ins, outs):
        for i_ref, o_ref in zip(ins, outs):
            o_ref[...] = i_ref[...].astype(BF16)

    return _CastJob((w_in_t, w_in_t), (in_a, in_b), (in_a, out_b),
                    (jax.ShapeDtypeStruct((rows_a, cols), BF16), jax.ShapeDtypeStruct((rows_b, cols), BF16)), body)


def _call(kernel_fn, args, in_specs, out_specs, out_shape, jobs=(), *, grid, scratch_shapes=(), name):
    n_in, n_out = len(args), len(out_specs)
    job_args = [a for j in jobs for a in j.inputs]
    job_outs = [o for j in jobs for o in j.out_shapes]

    def body(*refs):
        j_in = refs[n_in:n_in + len(job_args)]
        first_out = n_in + len(job_args)
        j_out = refs[first_out + n_out:first_out + n_out + len(job_outs)]
        kernel_fn(*refs[:n_in], *refs[first_out:first_out + n_out], *refs[first_out + n_out + len(job_outs):])
        for j in jobs:
            j.body(j_in[:len(j.inputs)], j_out[:len(j.out_shapes)])
            j_in, j_out = j_in[len(j.inputs):], j_out[len(j.out_shapes):]

    outs = pl.pallas_call(
        body,
        grid=grid,
        in_specs=[*in_specs, *(sp for j in jobs for sp in j.in_specs)],
        out_specs=[*out_specs, *(sp for j in jobs for sp in j.out_specs)],
        out_shape=[*out_shape, *job_outs],
        scratch_shapes=list(scratch_shapes),
        compiler_params=_params(),
        name=name,
    )(*args, *job_args)
    return outs[:n_out], outs[n_out:]


def _ffn_kernel(x_ref, g_ref, w1_ref, w3_ref, w2_ref, fg_ref, o_ref, *, final_norm):
    x = x_ref[...]
    inv_rms = lax.rsqrt(jnp.mean(x * x, axis=-1, keepdims=True) + EPS)
    h = (x * g_ref[...]).astype(BF16)
    acc = jnp.zeros_like(x)
    for lo, hi in zip(FF_SPLITS[:-1], FF_SPLITS[1:]):
        a = _dot(h, w1_ref[:, lo:hi]) * inv_rms
        b = _dot(h, w3_ref[:, lo:hi]) * inv_rms
        act = (a * _sigmoid(a) * b).astype(BF16)
        acc = acc + _dot(act, w2_ref[lo:hi, :])
    y = x + 0.5 * acc
    if final_norm:
        y = _rms(y, fg_ref[...])
    o_ref[...] = y


def _ffn(x, g, w1, w3, w2, fg, final_norm, make_jobs=None):
    t = x.shape[0]
    steps = t // FFN_ROWS
    row = pl.BlockSpec((FFN_ROWS, D_MODEL), lambda i: (i, 0))
    (y,), cast = _call(
        functools.partial(_ffn_kernel, final_norm=final_norm),
        (x, g, w1, w3, w2, fg),
        [row, _resident((1, D_MODEL)), _resident((D_MODEL, D_FF)), _resident((D_MODEL, D_FF)),
         _resident((D_FF, D_MODEL)), _resident((1, D_MODEL))],
        [row], [jax.ShapeDtypeStruct((t, D_MODEL), F32)],
        make_jobs(steps) if make_jobs else (),
        grid=(steps,), name="ffn_final" if final_norm else "ffn")
    return y, cast


def _cumsum_rows(mask_bf16, la):
    hi = la.astype(BF16)
    lo = (la - hi.astype(F32)).astype(BF16)
    return _dot(mask_bf16, hi) + _dot(mask_bf16, lo)


def _decay_column(b_last_row):
    return jnp.transpose(jnp.broadcast_to(jnp.exp(b_last_row), (8, GLA_DK)))[:, 0:1]


def _gla_wide_task(q_ref, k_ref, v_ref, la_ref, gn, o_ref, s_ref):
    c = GLA_WIDE
    ri = lax.broadcasted_iota(jnp.int32, (c, c), 0)
    ci = lax.broadcasted_iota(jnp.int32, (c, c), 1)
    causal = ci <= ri
    cum_incl = causal.astype(BF16)
    row_sl = [slice(ic * c, (ic + 1) * c) for ic in range(ROW_TILE // c)]
    b_all = [_dot(cum_incl, la_ref[rows, :].astype(BF16)) for rows in row_sl]
    units = [(ic, hd) for ic in range(len(row_sl)) for hd in range(GLA_HEADS)]

    def factors(ic, hd):
        rows, kc = row_sl[ic], slice(hd * GLA_DK, (hd + 1) * GLA_DK)
        b = b_all[ic][:, kc]
        qg = (q_ref[rows, kc].astype(F32) * jnp.exp(b)).astype(BF16)
        kg = (k_ref[rows, kc].astype(F32) * jnp.exp(-b)).astype(BF16)
        s = lax.dot_general(qg, kg, (((1,), (1,)), ((), ())), preferred_element_type=F32)
        return qg, kg, s, b[c - 1:c, :]

    nxt = factors(*units[0])
    yield
    for n, (ic, hd) in enumerate(units):
        qg, kg, s, b_last = nxt
        if n + 1 < len(units):
            nxt = factors(*units[n + 1])
        rows, vc = row_sl[ic], slice(hd * GLA_DV, (hd + 1) * GLA_DV)
        v = v_ref[rows, vc]
        s_old = s_ref[hd]
        inter = _dot(qg, s_old.astype(BF16))
        upd = lax.dot_general(kg, v, (((0,), (0,)), ((), ())), preferred_element_type=F32)
        p = jnp.where(causal, s, 0.0).astype(BF16)
        o = _dot(p, v) + inter
        s_ref[hd] = (s_old + upd) * _decay_column(b_last)
        o_ref[rows, vc] = _rms(o, gn).astype(BF16)
        yield


def _gla_anchored_chunks(q_ref, k_ref, v_ref, la_ref, gn, o_ref, s_ref):
    c = GLA_CHUNK
    ri = lax.broadcasted_iota(jnp.int32, (c, c), 0)
    ci = lax.broadcasted_iota(jnp.int32, (c, c), 1)
    causal = ci <= ri
    cum_incl = causal.astype(BF16)
    cum_anchor = (ci < (ri // GLA_SUB) * GLA_SUB).astype(BF16)
    krow = lax.broadcasted_iota(jnp.int32, (c, 1), 0)
    kcol = lax.broadcasted_iota(jnp.int32, (GLA_SUB, c), 1)

    def chunk(ic, carry):
        r0 = pl.multiple_of(ic * c, c)
        la = la_ref[pl.ds(r0, c), :]
        b_all = _cumsum_rows(cum_incl, la)
        beta_all = _cumsum_rows(cum_anchor, la)
        for hd in range(GLA_HEADS):
            kc = slice(hd * GLA_DK, (hd + 1) * GLA_DK)
            vc = slice(hd * GLA_DV, (hd + 1) * GLA_DV)
            q = q_ref[pl.ds(r0, c), kc].astype(F32)
            k = k_ref[pl.ds(r0, c), kc].astype(F32)
            v = v_ref[pl.ds(r0, c), vc]
            b = b_all[:, kc]
            beta = beta_all[:, kc]
            qa = (q * jnp.exp(b - beta)).astype(BF16)
            rows = []
            for blk in range(c // GLA_SUB):
                r_lo, r_hi = blk * GLA_SUB, (blk + 1) * GLA_SUB
                e = jnp.where(krow < r_lo, beta[r_lo:r_lo + 1, :] - b, EXP_TO_ZERO)
                ka = (k * jnp.exp(e)).astype(BF16)
                sc = lax.dot_general(qa[r_lo:r_hi, :], ka, (((1,), (1,)), ((), ())), preferred_element_type=F32)
                qb, bb = q[r_lo:r_hi, :], b[r_lo:r_hi, :]
                for j in range(r_lo, r_hi):
                    w = jnp.exp(jnp.minimum(bb - b[j:j + 1, :], 0.0))
                    sc = jnp.where(kcol == j, jnp.sum(qb * k[j:j + 1, :] * w, axis=1, keepdims=True), sc)
                rows.append(sc)
            p = jnp.where(causal, jnp.concatenate(rows, axis=0), 0.0).astype(BF16)
            s_old = s_ref[hd]
            o = _dot(p, v) + _dot((q * jnp.exp(b)).astype(BF16), s_old.astype(BF16))
            b_last = b[c - 1:c, :]
            k_dec = (k * jnp.exp(b_last - b)).astype(BF16)
            upd = lax.dot_general(k_dec, v, (((0,), (0,)), ((), ())), preferred_element_type=F32)
            s_ref[hd] = s_old * _decay_column(b_last) + upd
            o_ref[pl.ds(r0, c), vc] = _rms(o, gn).astype(BF16)
        return carry

    lax.fori_loop(0, ROW_TILE // c, chunk, 0)


def _interleave(tasks):
    tasks = list(tasks)
    while tasks:
        for t in list(tasks):
            try:
                next(t)
            except StopIteration:
                tasks.remove(t)


def _mix_in_task(x_ref, g_ref, wq_ref, wk_ref, wv_ref, wr_ref, wa_ref, wu_ref, wga_ref, wgb_ref, wal_ref, bal_ref,
                 sr_ref, z_ref, sga_ref, sgb_ref, ubuf, tile_in_seq, keep_halo, gla_inputs):
    h = _rms(x_ref[...], g_ref[...]).astype(BF16)
    a_code = _dot(h, wa_ref[...]).astype(BF16)
    yield
    u = _dot(h, wu_ref[...])
    yield
    r = _dot(h, wr_ref[...])
    sr_ref[...] = (r * _sigmoid(r)).astype(BF16)
    yield
    zg = _dot(a_code, wal_ref[...]) + bal_ref[...]
    yield
    sga_ref[...] = _sigmoid(_dot(h, wga_ref[...])).astype(BF16)
    yield
    sgb_ref[...] = _sigmoid(_dot(h, wgb_ref[...])).astype(BF16)
    yield

    ubuf[POOL_HALO:POOL_HALO + ROW_TILE, :] = u
    pos = lax.broadcasted_iota(jnp.int32, (ROW_TILE, 1), 0) + tile_in_seq * ROW_TILE
    for g, w in enumerate(POOL_WINDOWS):
        cols = slice(g * POOL_GC, (g + 1) * POOL_GC)
        s = ubuf[:, cols]
        sh = 1
        while sh < w:
            s = s + pltpu.roll(s, sh, 0)
            sh *= 2
        cnt = jnp.minimum(pos + 1, w).astype(F32)
        z_ref[:, cols] = (s[POOL_HALO:, :] / cnt - u[:, cols]).astype(BF16)
    ubuf[0:POOL_HALO, :] = jnp.where(keep_halo, ubuf[0:POOL_HALO, :], ubuf[ROW_TILE:ROW_TILE + POOL_HALO, :])

    q = (_dot(h, wq_ref[...]) * (GLA_DK ** -0.5)).astype(BF16)
    yield
    v = _dot(h, wv_ref[...]).astype(BF16)
    yield
    k = _dot(h, wk_ref[...]).astype(BF16)
    la = (jnp.minimum(zg, 0.0) - jnp.log(1.0 + jnp.exp(-jnp.abs(zg)))) * (1.0 / GLA_GATE_TEMP)
    span = None
    for r0 in range(0, ROW_TILE, GLA_WIDE):
        sp = jnp.max(-jnp.sum(la[r0:r0 + GLA_WIDE, :], axis=0, keepdims=True))
        span = sp if span is None else jnp.maximum(span, sp)
    gla_inputs.extend((q, k, v, la, span))


def _mix_gla_kernel(x_ref, g_ref, wt_a_ref, wt_b_ref, wal_ref, bal_ref, gn_ref,
                    sr_ref, z_ref, sga_ref, sgb_ref, o_ref,
                    q_s, k_s, v_s, la_s, s_ref, ubuf, span_s,
                    wq_ref, wk_ref, wv_ref, wr_ref, wa_ref, wu_ref, wga_ref, wgb_ref, *, tiles_per_seq, n_tiles):
    step = pl.program_id(0)
    mix_tile = jnp.minimum(step, n_tiles - 1)
    gla_tile = step - 1

    @pl.when(step == 0)
    def _():
        q_s[...] = jnp.zeros_like(q_s)
        k_s[...] = jnp.zeros_like(k_s)
        v_s[...] = jnp.zeros_like(v_s)
        la_s[...] = jnp.zeros_like(la_s)
        span_s[0] = 0.0
        lo = 0
        for w_ref, n in zip((wq_ref, wk_ref, wv_ref, wr_ref), W_IN_SIZES[:W_IN_SPLIT]):
            w_ref[...] = wt_a_ref[lo:lo + n, :].T
            lo += n
        lane = lax.broadcasted_iota(jnp.int32, (D_MODEL, LANE), 1)
        wa_ref[...] = jnp.where(lane < GLA_RANK, wt_b_ref[0:LANE, :].T, jnp.zeros((), BF16))
        lo = GLA_RANK
        for w_ref, n in zip((wu_ref, wga_ref, wgb_ref), W_IN_SIZES[W_IN_SPLIT + 1:]):
            w_ref[...] = wt_b_ref[lo:lo + n, :].T
            lo += n

    @pl.when(mix_tile % tiles_per_seq == 0)
    def _():
        ubuf[0:POOL_HALO, :] = jnp.zeros((POOL_HALO, POOL_WIDTH), F32)

    @pl.when(jnp.logical_or(step == 0, gla_tile % tiles_per_seq == 0))
    def _():
        s_ref[...] = jnp.zeros_like(s_ref)

    keep_halo = step >= n_tiles - 1
    mix_args = (x_ref, g_ref, wq_ref, wk_ref, wv_ref, wr_ref, wa_ref, wu_ref, wga_ref, wgb_ref, wal_ref, bal_ref,
                sr_ref, z_ref, sga_ref, sgb_ref, ubuf, mix_tile % tiles_per_seq, keep_halo)
    gla_args = (q_s, k_s, v_s, la_s, gn_ref[...], o_ref, s_ref)

    def run(gla_tasks):
        nxt = []
        _interleave(gla_tasks + [_mix_in_task(*mix_args, nxt)])
        q_s[...], k_s[...], v_s[...], la_s[...], span_s[0] = nxt

    wide_ok = span_s[0] < GLA_WIDE_MAX_SPAN

    @pl.when(wide_ok)
    def _():
        run([_gla_wide_task(*gla_args)])

    @pl.when(jnp.logical_not(wide_ok))
    def _():
        _gla_anchored_chunks(*gla_args)
        run([])


def _mix_gla(x, g, wt_a, wt_b, wal, bal, gn, seq):
    t = x.shape[0]
    n_tiles = t // ROW_TILE

    def cur(n):
        return pl.BlockSpec((ROW_TILE, n), lambda i: (jnp.minimum(i, n_tiles - 1), 0))

    prev = pl.BlockSpec((ROW_TILE, GLA_VW), lambda i: (jnp.maximum(i - 1, 0), 0))

    def out(n):
        return jax.ShapeDtypeStruct((t, n), BF16)

    outs, _ = _call(
        functools.partial(_mix_gla_kernel, tiles_per_seq=seq // ROW_TILE, n_tiles=n_tiles),
        (x, g, wt_a, wt_b, wal, bal, gn),
        [cur(D_MODEL), _resident((1, D_MODEL)), _resident(wt_a.shape), _resident(wt_b.shape),
         _resident((LANE, GLA_QK)), _resident((1, GLA_QK)), _resident((1, GLA_DV))],
        [cur(GLA_VW), cur(POOL_WIDTH), cur(D_MODEL), cur(D_MODEL), prev],
        [out(GLA_VW), out(POOL_WIDTH), out(D_MODEL), out(D_MODEL), out(GLA_VW)],
        grid=(n_tiles + 1,),
        scratch_shapes=[pltpu.VMEM((ROW_TILE, GLA_QK), BF16), pltpu.VMEM((ROW_TILE, GLA_QK), BF16),
                        pltpu.VMEM((ROW_TILE, GLA_VW), BF16), pltpu.VMEM((ROW_TILE, GLA_QK), F32),
                        pltpu.VMEM((GLA_HEADS, GLA_DK, GLA_DV), F32),
                        pltpu.VMEM((POOL_HALO + ROW_TILE, POOL_WIDTH), F32),
                        pltpu.SMEM((1,), F32),
                        *(pltpu.VMEM((D_MODEL, LANE if n == GLA_RANK else n), BF16) for n in W_IN_SIZES)],
        name="mix_gla")
    return outs


def _mem_kv_kernel(m_ref, g_ref, wk_ref, wv_ref, k_ref, v_ref, wk_s, wv_s):
    @pl.when(pl.program_id(0) == 0)
    def _():
        wk_s[...] = wk_ref[...].astype(BF16)
        wv_s[...] = wv_ref[...].astype(BF16)

    m = _rms(m_ref[...], g_ref[...]).astype(BF16)
    k_ref[...] = _dot(m, wk_s[...]).astype(BF16)
    v_ref[...] = _dot(m, wv_s[...]).astype(BF16)


def _mem_kv(mem, g, wk, wv, mem_len, make_jobs):
    t = mem.shape[0]
    steps = t // mem_len
    row = pl.BlockSpec((mem_len, D_MODEL), lambda i: (i, 0))
    out = jax.ShapeDtypeStruct((t, D_MODEL), BF16)
    return _call(
        _mem_kv_kernel, (mem, g, wk, wv),
        [row, _resident((1, D_MODEL)), _resident((D_MODEL, D_MODEL)), _resident((D_MODEL, D_MODEL))],
        [row, row], [out, out], make_jobs(steps), grid=(steps,),
        scratch_shapes=[pltpu.VMEM((D_MODEL, D_MODEL), BF16)] * 2, name="mem_kv")


def _mix_xattn_kernel(x_ref, o_ref, sr_ref, z_ref, sga_ref, sgb_ref, wua_ref, pm_ref, ps_ref, wub_ref, wmo_ref,
                      xg_ref, wq_ref, k_ref, v_ref, wo_ref, y_ref):
    zs = []
    for g in range(POOL_GROUPS):
        cols = slice(g * POOL_GC, (g + 1) * POOL_GC)
        zs.append((_dot(z_ref[:, cols], pm_ref[g].astype(BF16)) * ps_ref[:, cols]).astype(BF16))
    ya = _dot(o_ref[...] * sr_ref[...], wua_ref[...])
    yb = _dot(jnp.concatenate(zs, axis=1), wub_ref[...])
    merged = (sga_ref[...].astype(F32) * ya + sgb_ref[...].astype(F32) * yb).astype(BF16)
    x = x_ref[...] + _dot(merged, wmo_ref[...])

    inv_rms = lax.rsqrt(jnp.mean(x * x, axis=-1, keepdims=True) + EPS)
    q = (_dot((x * xg_ref[...]).astype(BF16), wq_ref[...]) * (inv_rms * (XA_HD ** -0.5))).astype(BF16)

    def scores(hd):
        cols = slice(hd * XA_HD, (hd + 1) * XA_HD)
        return lax.dot_general(q[:, cols], k_ref[:, cols], (((1,), (1,)), ((), ())), preferred_element_type=F32)

    outs = []
    s = scores(0)
    for hd in range(XA_HEADS):
        s_next = scores(hd + 1) if hd + 1 < XA_HEADS else None
        e = jnp.exp(s - jnp.max(s, axis=-1, keepdims=True))
        o = _dot(e.astype(BF16), v_ref[:, hd * XA_HD:(hd + 1) * XA_HD])
        outs.append((o / jnp.sum(e, axis=-1, keepdims=True)).astype(BF16))
        s = s_next
    y_ref[...] = x + _dot(jnp.concatenate(outs, axis=1), wo_ref[...])


def _mix_xattn(x, o, sr, z, sga, sgb, wua, pm, ps, wub, wmo, xg, wq, km, vm, wo, seq, mem_len):
    t = x.shape[0]
    steps = t // XATTN_ROWS
    tiles_per_seq = seq // XATTN_ROWS

    def row(n):
        return pl.BlockSpec((XATTN_ROWS, n), lambda i: (i, 0))

    mem =pl.BlockSpec((mem_len, D_MODEL), lambda i: (i // tiles_per_seq, 0))
    (y,), _ = _call(
        _mix_xattn_kernel, (x, o, sr, z, sga, sgb, wua, pm, ps, wub, wmo, xg, wq, km, vm, wo),
        [row(D_MODEL), row(GLA_VW), row(GLA_VW), row(POOL_WIDTH), row(D_MODEL), row(D_MODEL),
         _resident((GLA_VW, D_MODEL)), _resident((POOL_GROUPS, POOL_GC, POOL_GC)),
         _resident((1, POOL_WIDTH)), _resident((POOL_WIDTH, D_MODEL)), _resident((D_MODEL, D_MODEL)),
         _resident((1, D_MODEL)), _resident((D_MODEL, D_MODEL)), mem, mem, _resident((D_MODEL, D_MODEL))],
        [row(D_MODEL)], [jax.ShapeDtypeStruct((t, D_MODEL), F32)],
        grid=(steps,), name="mix_xattn")
    return y


def _layer(x, mem, ffn1_norm, ffn1_w1, ffn1_w3, ffn1_w2, mix_norm, w_in, w_alpha, b_alpha, gla_head_norm,
           w_up_a, pool_mix, pool_scale, w_up_b, w_mix_out, xa_norm, mem_norm, xa_wq, xa_wk, xa_wv, xa_wo,
           ffn2_norm, ffn2_w1, ffn2_w3, ffn2_w2, final_norm, last):
    batch, seq, _ = x.shape
    mem_len = mem.shape[1]
    assert seq % ROW_TILE == 0 and ROW_TILE % GLA_CHUNK == 0 and ROW_TILE % GLA_WIDE == 0

    def vec(a):
        return a.reshape(1, -1).astype(F32)

    def casts(*ws):
        return lambda n_steps: tuple(_cast_rows_job(w, n_steps) for w in ws)

    (km, vm), ffn1_w = _mem_kv(mem.reshape(batch * mem_len, D_MODEL), vec(mem_norm), xa_wk, xa_wv, mem_len,
                               casts(ffn1_w1, ffn1_w3, ffn1_w2))
    later = (w_up_a, w_up_b, w_mix_out, xa_wq, xa_wo, ffn2_w1, ffn2_w3, ffn2_w2)
    x1, cast = _ffn(x.reshape(batch * seq, D_MODEL), vec(ffn1_norm), *ffn1_w, vec(final_norm), False,
                    lambda n_steps: (_w_in_job(jnp.swapaxes(w_in, 0, 1), n_steps), *casts(*later)(n_steps)))
    wt_a, wt_b, wua, wub, wmo, xwq, xwo, *ffn2_w = cast
    wal = jnp.pad(w_alpha.astype(BF16), ((0, LANE - GLA_RANK), (0, 0)))
    sr, z, sga, sgb, o = _mix_gla(x1, vec(mix_norm), wt_a, wt_b, wal, vec(b_alpha), vec(gla_head_norm), seq)
    x3 = _mix_xattn(x1, o, sr, z, sga, sgb, wua, pool_mix, vec(pool_scale), wub, wmo, vec(xa_norm), xwq, km, vm, xwo,
                    seq, mem_len)
    x4, _ = _ffn(x3, vec(ffn2_norm), *ffn2_w, vec(final_norm), last)
    return x4.reshape(batch, seq, D_MODEL)


def kernel(x, mem, ffn1_norm, ffn1_w1, ffn1_w3, ffn1_w2, mix_norm, w_in, w_alpha, b_alpha, gla_head_norm, w_up_a,
           pool_mix, pool_scale, w_up_b, w_mix_out, xa_norm, mem_norm, xa_wq, xa_wk, xa_wv, xa_wo, ffn2_norm,
           ffn2_w1, ffn2_w3, ffn2_w2, final_norm):
    depth = ffn1_norm.shape[0]
    for l in range(depth):
        last = l == depth - 1
        x = _layer(x, mem, ffn1_norm[l], ffn1_w1[l], ffn1_w3[l], ffn1_w2[l], mix_norm[l], w_in[l], w_alpha[l],
                   b_alpha[l], gla_head_norm[l], w_up_a[l], pool_mix[l], pool_scale[l], w_up_b[l], w_mix_out[l],
                   xa_norm[l], mem_norm[l], xa_wq[l], xa_wk[l], xa_wv[l], xa_wo[l], ffn2_norm[l], ffn2_w1[l],
                   ffn2_w3[l], ffn2_w2[l], final_norm, last)
    return x
```

```python
import functools
from typing import Callable, NamedTuple

import jax
import jax.numpy as jnp
from jax import lax
from jax.experimental import pallas as pl
from jax.experimental.pallas import tpu as pltpu

F32 = jnp.float32
BF16 = jnp.bfloat16

D_MODEL = 1024
D_FF = 2816
EPS = 1e-6
GLA_HEADS = 4
GLA_DK = 128
GLA_DV = 256
GLA_QK = GLA_HEADS * GLA_DK
GLA_VW = GLA_HEADS * GLA_DV
GLA_RANK = 16
GLA_GATE_TEMP = 16.0
GLA_CHUNK = 64
GLA_SUB = 16
GLA_WIDE = 256
GLA_WIDE_MAX_SPAN = 60.0
EXP_TO_ZERO = -1e30
POOL_GROUPS = 4
POOL_GC = 128
POOL_WIDTH = POOL_GROUPS * POOL_GC
POOL_WINDOWS = (2, 4, 8, 16)
POOL_HALO = 16
XA_HEADS = 4
XA_HD = 256

LANE = 128
BF16_ROWS = 16
W_IN_SIZES = (GLA_QK, GLA_QK, GLA_VW, GLA_VW, GLA_RANK, POOL_WIDTH, D_MODEL, D_MODEL)
W_IN_SPLIT = 4
ROW_TILE = 512
XATTN_ROWS = 1024
FFN_ROWS = 1024
FF_SPLITS = (0, 768, 1536, 2304, D_FF)
VMEM_LIMIT = 56 * 1024 * 1024


def _rms(x, g):
    return x * lax.rsqrt(jnp.mean(x * x, axis=-1, keepdims=True) + EPS) * g


def _sigmoid(x):
    return 0.5 * jnp.tanh(0.5 * x) + 0.5


def _dot(a, b):
    return jnp.dot(a, b, preferred_element_type=F32)


def _resident(shape):
    nd = len(shape)
    return pl.BlockSpec(shape, lambda *_: (0,) * nd, pipeline_mode=pl.Buffered(1))


def _params():
    return pltpu.CompilerParams(dimension_semantics=("arbitrary",), vmem_limit_bytes=VMEM_LIMIT)


class _CastJob(NamedTuple):
    inputs: tuple
    in_specs: tuple
    out_specs: tuple
    out_shapes: tuple
    body: Callable


def _cast_rows_job(w, n_steps):
    rows, cols = w.shape
    per_step = -(-rows // n_steps)
    block = next(b for b in range(BF16_ROWS, rows + 1, BF16_ROWS) if rows % b == 0 and b >= per_step)
    n_blocks = rows // block
    spec = pl.BlockSpec((block, cols), lambda i: ((i * n_blocks) // n_steps, 0))

    def body(ins, outs):
        outs[0][...] = ins[0][...].astype(BF16)

    return _CastJob((w,), (spec,), (spec,), (jax.ShapeDtypeStruct((rows, cols), BF16),), body)


def _w_in_job(w_in_t, n_steps):
    cols = w_in_t.shape[1]
    rows_a = sum(W_IN_SIZES[:W_IN_SPLIT])
    rows_b = sum(W_IN_SIZES[W_IN_SPLIT:])
    assert w_in_t.shape[0] == rows_a + rows_b and rows_a % n_steps == 0
    blk_a = rows_a // n_steps
    per_b = -(-rows_b // n_steps)
    blk_b = next(r for r in range(BF16_ROWS, rows_b + 1, BF16_ROWS) if rows_b % r == 0 and r >= per_b)
    last_b = rows_b // blk_b - 1
    assert blk_a % BF16_ROWS == 0
    in_a = pl.BlockSpec((blk_a, cols), lambda i: (i, 0))
    in_b = pl.BlockSpec((pl.Element(blk_b), pl.Element(cols)),
                        lambda i: (pl.multiple_of(rows_a + blk_b * jnp.minimum(i, last_b), BF16_ROWS), 0))
    out_b = pl.BlockSpec((blk_b, cols), lambda i: (jnp.minimum(i, last_b), 0))

    def body(ins, outs):
        for i_ref, o_ref in zip(ins, outs):
            o_ref[...] = i_ref[...].astype(BF16)

    return _CastJob((w_in_t, w_in_t), (in_a, in_b), (in_a, out_b),
                    (jax.ShapeDtypeStruct((rows_a, cols), BF16), jax.ShapeDtypeStruct((rows_b, cols), BF16)), body)


def _call(kernel_fn, args, in_specs, out_specs, out_shape, jobs=(), *, grid, scratch_shapes=(), name):
    n_in, n_out = len(args), len(out_specs)
    job_args = [a for j in jobs for a in j.inputs]
    job_outs = [o for j in jobs for o in j.out_shapes]

    def body(*refs):
        j_in = refs[n_in:n_in + len(job_args)]
        first_out = n_in + len(job_args)
        j_out = refs[first_out + n_out:first_out + n_out + len(job_outs)]
        kernel_fn(*refs[:n_in], *refs[first_out:first_out + n_out], *refs[first_out + n_out + len(job_outs):])
        for j in jobs:
            j.body(j_in[:len(j.inputs)], j_out[:len(j.out_shapes)])
            j_in, j_out = j_in[len(j.inputs):], j_out[len(j.out_shapes):]

    outs = pl.pallas_call(
        body,
        grid=grid,
        in_specs=[*in_specs, *(sp for j in jobs for sp in j.in_specs)],
        out_specs=[*out_specs, *(sp for j in jobs for sp in j.out_specs)],
        out_shape=[*out_shape, *job_outs],
        scratch_shapes=list(scratch_shapes),
        compiler_params=_params(),
        name=name,
    )(*args, *job_args)
    return outs[:n_out], outs[n_out:]


def _ffn_kernel(x_ref, g_ref, w1_ref, w3_ref, w2_ref, fg_ref, o_ref, *, final_norm):
    x = x_ref[...]
    inv_rms = lax.rsqrt(jnp.mean(x * x, axis=-1, keepdims=True) + EPS)
    h = (x * g_ref[...]).astype(BF16)
    acc = jnp.zeros_like(x)
    for lo, hi in zip(FF_SPLITS[:-1], FF_SPLITS[1:]):
        a = _dot(h, w1_ref[:, lo:hi]) * inv_rms
        b = _dot(h, w3_ref[:, lo:hi]) * inv_rms
        act = (a * _sigmoid(a) * b).astype(BF16)
        acc = acc + _dot(act, w2_ref[lo:hi, :])
    y = x + 0.5 * acc
    if final_norm:
        y = _rms(y, fg_ref[...])
    o_ref[...] = y


def _ffn(x, g, w1, w3, w2, fg, final_norm, make_jobs=None):
    t = x.shape[0]
    steps = t // FFN_ROWS
    row = pl.BlockSpec((FFN_ROWS, D_MODEL), lambda i: (i, 0))
    (y,), cast = _call(
        functools.partial(_ffn_kernel, final_norm=final_norm),
        (x, g, w1, w3, w2, fg),
        [row, _resident((1, D_MODEL)), _resident((D_MODEL, D_FF)), _resident((D_MODEL, D_FF)),
         _resident((D_FF, D_MODEL)), _resident((1, D_MODEL))],
        [row], [jax.ShapeDtypeStruct((t, D_MODEL), F32)],
        make_jobs(steps) if make_jobs else (),
        grid=(steps,), name="ffn_final" if final_norm else "ffn")
    return y, cast


def _cumsum_rows(mask_bf16, la):
    hi = la.astype(BF16)
    lo = (la - hi.astype(F32)).astype(BF16)
    return _dot(mask_bf16, hi) + _dot(mask_bf16, lo)


def _decay_column(b_last_row):
    return jnp.transpose(jnp.broadcast_to(jnp.exp(b_last_row), (8, GLA_DK)))[:, 0:1]


def _gla_wide_task(q_ref, k_ref, v_ref, la_ref, gn, o_ref, s_ref):
    c = GLA_WIDE
    ri = lax.broadcasted_iota(jnp.int32, (c, c), 0)
    ci = lax.broadcasted_iota(jnp.int32, (c, c), 1)
    causal = ci <= ri
    cum_incl = causal.astype(BF16)
    row_sl = [slice(ic * c, (ic + 1) * c) for ic in range(ROW_TILE // c)]
    b_all = [_dot(cum_incl, la_ref[rows, :].astype(BF16)) for rows in row_sl]
    units = [(ic, hd) for ic in range(len(row_sl)) for hd in range(GLA_HEADS)]

    def factors(ic, hd):
        rows, kc = row_sl[ic], slice(hd * GLA_DK, (hd + 1) * GLA_DK)
        b = b_all[ic][:, kc]
        qg = (q_ref[rows, kc].astype(F32) * jnp.exp(b)).astype(BF16)
        kg = (k_ref[rows, kc].astype(F32) * jnp.exp(-b)).astype(BF16)
        s = lax.dot_general(qg, kg, (((1,), (1,)), ((), ())), preferred_element_type=F32)
        return qg, kg, s, b[c - 1:c, :]

    nxt = factors(*units[0])
    yield
    for n, (ic, hd) in enumerate(units):
        qg, kg, s, b_last = nxt
        if n + 1 < len(units):
            nxt = factors(*units[n + 1])
        rows, vc = row_sl[ic], slice(hd * GLA_DV, (hd + 1) * GLA_DV)
        v = v_ref[rows, vc]
        s_old = s_ref[hd]
        inter = _dot(qg, s_old.astype(BF16))
        upd = lax.dot_general(kg, v, (((0,), (0,)), ((), ())), preferred_element_type=F32)
        p = jnp.where(causal, s, 0.0).astype(BF16)
        o = _dot(p, v) + inter
        s_ref[hd] = (s_old + upd) * _decay_column(b_last)
        o_ref[rows, vc] = _rms(o, gn).astype(BF16)
        yield


def _gla_anchored_chunks(q_ref, k_ref, v_ref, la_ref, gn, o_ref, s_ref):
    c = GLA_CHUNK
    ri = lax.broadcasted_iota(jnp.int32, (c, c), 0)
    ci = lax.broadcasted_iota(jnp.int32, (c, c), 1)
    causal = ci <= ri
    cum_incl = causal.astype(BF16)
    cum_anchor = (ci < (ri // GLA_SUB) * GLA_SUB).astype(BF16)
    krow = lax.broadcasted_iota(jnp.int32, (c, 1), 0)
    kcol = lax.broadcasted_iota(jnp.int32, (GLA_SUB, c), 1)

    def chunk(ic, carry):
        r0 = pl.multiple_of(ic * c, c)
        la = la_ref[pl.ds(r0, c), :]
        b_all = _cumsum_rows(cum_incl, la)
        beta_all = _cumsum_rows(cum_anchor, la)
        for hd in range(GLA_HEADS):
            kc = slice(hd * GLA_DK, (hd + 1) * GLA_DK)
            vc = slice(hd * GLA_DV, (hd + 1) * GLA_DV)
            q = q_ref[pl.ds(r0, c), kc].astype(F32)
            k = k_ref[pl.ds(r0, c), kc].astype(F32)
            v = v_ref[pl.ds(r0, c), vc]
            b = b_all[:, kc]
            beta = beta_all[:, kc]
            qa = (q * jnp.exp(b - beta)).astype(BF16)
            rows = []
            for blk in range(c // GLA_SUB):
                r_lo, r_hi = blk * GLA_SUB, (blk + 1) * GLA_SUB
                e = jnp.where(krow < r_lo, beta[r_lo:r_lo + 1, :] - b, EXP_TO_ZERO)
                ka = (k * jnp.exp(e)).astype(BF16)
                sc = lax.dot_general(qa[r_lo:r_hi, :], ka, (((1,), (1,)), ((), ())), preferred_element_type=F32)
                qb, bb = q[r_lo:r_hi, :], b[r_lo:r_hi, :]
                for j in range(r_lo, r_hi):
                    w = jnp.exp(jnp.minimum(bb - b[j:j + 1, :], 0.0))
                    sc = jnp.where(kcol == j, jnp.sum(qb * k[j:j + 1, :] * w, axis=1, keepdims=True), sc)
                rows.append(sc)
            p = jnp.where(causal, jnp.concatenate(rows, axis=0), 0.0).astype(BF16)
            s_old = s_ref[hd]
            o = _dot(p, v) + _dot((q * jnp.exp(b)).astype(BF16), s_old.astype(BF16))
            b_last = b[c - 1:c, :]
            k_dec = (k * jnp.exp(b_last - b)).astype(BF16)
            upd = lax.dot_general(k_dec, v, (((0,), (0,)), ((), ())), preferred_element_type=F32)
            s_ref[hd] = s_old * _decay_column(b_last) + upd
            o_ref[pl.ds(r0, c), vc] = _rms(o, gn).astype(BF16)
        return carry

    lax.fori_loop(0, ROW_TILE // c, chunk, 0)


def _interleave(tasks):
    tasks = list(tasks)
    while tasks:
        for t in list(tasks):
            try:
                next(t)
            except StopIteration:
                tasks.remove(t)


def _mix_in_task(x_ref, g_ref, wq_ref, wk_ref, wv_ref, wr_ref, wa_ref, wu_ref, wga_ref, wgb_ref, wal_ref, bal_ref,
                 sr_ref, z_ref, sga_ref, sgb_ref, ubuf, tile_in_seq, gla_inputs):
    h = _rms(x_ref[...], g_ref[...]).astype(BF16)
    a_code = _dot(h, wa_ref[...]).astype(BF16)
    yield
    u = _dot(h, wu_ref[...])
    yield
    r = _dot(h, wr_ref[...])
    sr_ref[...] = (r * _sigmoid(r)).astype(BF16)
    yield
    zg = _dot(a_code, wal_ref[...]) + bal_ref[...]
    yield
    sga_ref[...] = _sigmoid(_dot(h, wga_ref[...])).astype(BF16)
    yield
    sgb_ref[...] = _sigmoid(_dot(h, wgb_ref[...])).astype(BF16)
    yield

    ubuf[POOL_HALO:POOL_HALO + ROW_TILE, :] = u
    pos = lax.broadcasted_iota(jnp.int32, (ROW_TILE, 1), 0) + tile_in_seq * ROW_TILE
    for g, w in enumerate(POOL_WINDOWS):
        cols = slice(g * POOL_GC, (g + 1) * POOL_GC)
        s = ubuf[:, cols]
        sh = 1
        while sh < w:
            s = s + pltpu.roll(s, sh, 0)
            sh *= 2
        cnt = jnp.minimum(pos + 1, w).astype(F32)
        z_ref[:, cols] = (s[POOL_HALO:, :] / cnt - u[:, cols]).astype(BF16)
    ubuf[0:POOL_HALO, :] = ubuf[ROW_TILE:ROW_TILE + POOL_HALO, :]

    q = (_dot(h, wq_ref[...]) * (GLA_DK ** -0.5)).astype(BF16)
    yield
    v = _dot(h, wv_ref[...]).astype(BF16)
    yield
    k = _dot(h, wk_ref[...]).astype(BF16)
    la = (jnp.minimum(zg, 0.0) - jnp.log(1.0 + jnp.exp(-jnp.abs(zg)))) * (1.0 / GLA_GATE_TEMP)
    span = None
    for r0 in range(0, ROW_TILE, GLA_WIDE):
        sp = jnp.max(-jnp.sum(la[r0:r0 + GLA_WIDE, :], axis=0, keepdims=True))
        span = sp if span is None else jnp.maximum(span, sp)
    gla_inputs.extend((q, k, v, la, span))


def _mix_gla_kernel(x_ref, g_ref, wt_a_ref, wt_b_ref, wal_ref, bal_ref, gn_ref,
                    sr_ref, z_ref, sga_ref, sgb_ref, o_ref,
                    q_s, k_s, v_s, la_s, s_ref, ubuf, span_s,
                    wq_ref, wk_ref, wv_ref, wr_ref, wa_ref, wu_ref, wga_ref, wgb_ref, *, tiles_per_seq, n_tiles):
    step = pl.program_id(0)
    mix_tile = jnp.minimum(step, n_tiles - 1)
    gla_tile = step - 1

    @pl.when(step == 0)
    def _():
        q_s[...] = jnp.zeros_like(q_s)
        k_s[...] = jnp.zeros_like(k_s)
        v_s[...] = jnp.zeros_like(v_s)
        la_s[...] = jnp.zeros_like(la_s)
        span_s[0] = 0.0
        lo = 0
        for w_ref, n in zip((wq_ref, wk_ref, wv_ref, wr_ref), W_IN_SIZES[:W_IN_SPLIT]):
            w_ref[...] = wt_a_ref[lo:lo + n, :].T
            lo += n
        lane = lax.broadcasted_iota(jnp.int32, (D_MODEL, LANE), 1)
        wa_ref[...] = jnp.where(lane < GLA_RANK, wt_b_ref[0:LANE, :].T, jnp.zeros((), BF16))
        lo = GLA_RANK
        for w_ref, n in zip((wu_ref, wga_ref, wgb_ref), W_IN_SIZES[W_IN_SPLIT + 1:]):
            w_ref[...] = wt_b_ref[lo:lo + n, :].T
            lo += n

    @pl.when(mix_tile % tiles_per_seq == 0)
    def _():
        ubuf[0:POOL_HALO, :] = jnp.zeros((POOL_HALO, POOL_WIDTH), F32)

    @pl.when(jnp.logical_or(step == 0, gla_tile % tiles_per_seq == 0))
    def _():
        s_ref[...] = jnp.zeros_like(s_ref)

    mix_args = (x_ref, g_ref, wq_ref, wk_ref, wv_ref, wr_ref, wa_ref, wu_ref, wga_ref, wgb_ref, wal_ref, bal_ref,
                sr_ref, z_ref, sga_ref, sgb_ref, ubuf, mix_tile % tiles_per_seq)
    gla_args = (q_s, k_s, v_s, la_s, gn_ref[...], o_ref, s_ref)

    def run(gla_tasks):
        nxt = []
        _interleave(gla_tasks + [_mix_in_task(*mix_args, nxt)])
        q_s[...], k_s[...], v_s[...], la_s[...], span_s[0] = nxt

    wide_ok = span_s[0] < GLA_WIDE_MAX_SPAN
    last = step == n_tiles

    @pl.when(jnp.logical_and(wide_ok, jnp.logical_not(last)))
    def _():
        run([_gla_wide_task(*gla_args)])

    @pl.when(jnp.logical_and(wide_ok, last))
    def _():
        _interleave([_gla_wide_task(*gla_args)])

    @pl.when(jnp.logical_not(wide_ok))
    def _():
        _gla_anchored_chunks(*gla_args)
        pl.when(jnp.logical_not(last))(lambda: run([]))


def _mix_gla(x, g, wt_a, wt_b, wal, bal, gn, seq):
    t = x.shape[0]
    n_tiles = t // ROW_TILE

    def cur(n):
        return pl.BlockSpec((ROW_TILE, n), lambda i: (jnp.minimum(i, n_tiles - 1), 0))

    prev = pl.BlockSpec((ROW_TILE, GLA_VW), lambda i: (jnp.maximum(i - 1, 0), 0))

    def out(n):
        return jax.ShapeDtypeStruct((t, n), BF16)

    outs, _ = _call(
        functools.partial(_mix_gla_kernel, tiles_per_seq=seq // ROW_TILE, n_tiles=n_tiles),
        (x, g, wt_a, wt_b, wal, bal, gn),
        [cur(D_MODEL), _resident((1, D_MODEL)), _resident(wt_a.shape), _resident(wt_b.shape),
         _resident((LANE, GLA_QK)), _resident((1, GLA_QK)), _resident((1, GLA_DV))],
        [cur(GLA_VW), cur(POOL_WIDTH), cur(D_MODEL), cur(D_MODEL), prev],
        [out(GLA_VW), out(POOL_WIDTH), out(D_MODEL), out(D_MODEL), out(GLA_VW)],
        grid=(n_tiles + 1,),
        scratch_shapes=[pltpu.VMEM((ROW_TILE, GLA_QK), BF16), pltpu.VMEM((ROW_TILE, GLA_QK), BF16),
                        pltpu.VMEM((ROW_TILE, GLA_VW), BF16), pltpu.VMEM((ROW_TILE, GLA_QK), F32),
                        pltpu.VMEM((GLA_HEADS, GLA_DK, GLA_DV), F32),
                        pltpu.VMEM((POOL_HALO + ROW_TILE, POOL_WIDTH), F32),
                        pltpu.SMEM((1,), F32),
                        *(pltpu.VMEM((D_MODEL, LANE if n == GLA_RANK else n), BF16) for n in W_IN_SIZES)],
        name="mix_gla")
    return outs


def _mem_kv_kernel(m_ref, g_ref, wk_ref, wv_ref, k_ref, v_ref, wk_s, wv_s):
    @pl.when(pl.program_id(0) == 0)
    def _():
        wk_s[...] = wk_ref[...].astype(BF16)
        wv_s[...] = wv_ref[...].astype(BF16)

    m = _rms(m_ref[...], g_ref[...]).astype(BF16)
    k_ref[...] = _dot(m, wk_s[...]).astype(BF16)
    v_ref[...] = _dot(m, wv_s[...]).astype(BF16)


def _mem_kv(mem, g, wk, wv, mem_len, make_jobs):
    t = mem.shape[0]
    steps = t // mem_len
    row = pl.BlockSpec((mem_len, D_MODEL), lambda i: (i, 0))
    out = jax.ShapeDtypeStruct((t, D_MODEL), BF16)
    return _call(
        _mem_kv_kernel, (mem, g, wk, wv),
        [row, _resident((1, D_MODEL)), _resident((D_MODEL, D_MODEL)), _resident((D_MODEL, D_MODEL))],
        [row, row], [out, out], make_jobs(steps), grid=(steps,),
        scratch_shapes=[pltpu.VMEM((D_MODEL, D_MODEL), BF16)] * 2, name="mem_kv")


def _mix_xattn_kernel(x_ref, o_ref, sr_ref, z_ref, sga_ref, sgb_ref, wua_ref, pm_ref, ps_ref, wub_ref, wmo_ref,
                      xg_ref, wq_ref, k_ref, v_ref, wo_ref, y_ref):
    zs = []
    for g in range(POOL_GROUPS):
        cols = slice(g * POOL_GC, (g + 1) * POOL_GC)
        zs.append((_dot(z_ref[:, cols], pm_ref[g].astype(BF16)) * ps_ref[:, cols]).astype(BF16))
    ya = _dot(o_ref[...] * sr_ref[...], wua_ref[...])
    yb = _dot(jnp.concatenate(zs, axis=1), wub_ref[...])
    merged = (sga_ref[...].astype(F32) * ya + sgb_ref[...].astype(F32) * yb).astype(BF16)
    x = x_ref[...] + _dot(merged, wmo_ref[...])

    inv_rms = lax.rsqrt(jnp.mean(x * x, axis=-1, keepdims=True) + EPS)
    q = (_dot((x * xg_ref[...]).astype(BF16), wq_ref[...]) * (inv_rms * (XA_HD ** -0.5))).astype(BF16)

    def scores(hd):
        cols = slice(hd * XA_HD, (hd + 1) * XA_HD)
        return lax.dot_general(q[:, cols], k_ref[:, cols], (((1,), (1,)), ((), ())), preferred_element_type=F32)

    outs = []
    s = scores(0)
    for hd in range(XA_HEADS):
        s_next = scores(hd + 1) if hd + 1 < XA_HEADS else None
        e = jnp.exp(s - jnp.max(s, axis=-1, keepdims=True))
        o = _dot(e.astype(BF16), v_ref[:, hd * XA_HD:(hd + 1) * XA_HD])
        outs.append((o / jnp.sum(e, axis=-1, keepdims=True)).astype(BF16))
        s = s_next
    y_ref[...] = x + _dot(jnp.concatenate(outs, axis=1), wo_ref[...])


def _mix_xattn(x, o, sr, z, sga, sgb, wua, pm, ps, wub, wmo, xg, wq, km, vm, wo, seq, mem_len):
    t = x.shape[0]
    steps = t // XATTN_ROWS
    tiles_per_seq = seq // XATTN_ROWS

    def row(n):
        return pl.BlockSpec((XATTN_ROWS, n), lambda i: (i, 0))

    mem =pl.BlockSpec((mem_len, D_MODEL), lambda i: (i // tiles_per_seq, 0))
    (y,), _ = _call(
        _mix_xattn_kernel, (x, o, sr, z, sga, sgb, wua, pm, ps, wub, wmo, xg, wq, km, vm, wo),
        [row(D_MODEL), row(GLA_VW), row(GLA_VW), row(POOL_WIDTH), row(D_MODEL), row(D_MODEL),
         _resident((GLA_VW, D_MODEL)), _resident((POOL_GROUPS, POOL_GC, POOL_GC)),
         _resident((1, POOL_WIDTH)), _resident((POOL_WIDTH, D_MODEL)), _resident((D_MODEL, D_MODEL)),
         _resident((1, D_MODEL)), _resident((D_MODEL, D_MODEL)), mem, mem, _resident((D_MODEL, D_MODEL))],
        [row(D_MODEL)], [jax.ShapeDtypeStruct((t, D_MODEL), F32)],
        grid=(steps,), name="mix_xattn")
    return y


def _layer(x, mem, ffn1_norm, ffn1_w1, ffn1_w3, ffn1_w2, mix_norm, w_in, w_alpha, b_alpha, gla_head_norm,
           w_up_a, pool_mix, pool_scale, w_up_b, w_mix_out, xa_norm, mem_norm, xa_wq, xa_wk, xa_wv, xa_wo,
           ffn2_norm, ffn2_w1, ffn2_w3, ffn2_w2, final_norm, last):
    batch, seq, _ = x.shape
    mem_len = mem.shape[1]
    assert seq % ROW_TILE == 0 and ROW_TILE % GLA_CHUNK == 0 and ROW_TILE % GLA_WIDE == 0

    def vec(a):
        return a.reshape(1, -1).astype(F32)

    def casts(*ws):
        return lambda n_steps: tuple(_cast_rows_job(w, n_steps) for w in ws)

    (km, vm), ffn1_w = _mem_kv(mem.reshape(batch * mem_len, D_MODEL), vec(mem_norm), xa_wk, xa_wv, mem_len,
                               casts(ffn1_w1, ffn1_w3, ffn1_w2))
    later = (w_up_a, w_up_b, w_mix_out, xa_wq, xa_wo, ffn2_w1, ffn2_w3, ffn2_w2)
    x1, cast = _ffn(x.reshape(batch * seq, D_MODEL), vec(ffn1_norm), *ffn1_w, vec(final_norm), False,
                    lambda n_steps: (_w_in_job(jnp.swapaxes(w_in, 0, 1), n_steps), *casts(*later)(n_steps)))
    wt_a, wt_b, wua, wub, wmo, xwq, xwo, *ffn2_w = cast
    wal = jnp.pad(w_alpha.astype(BF16), ((0, LANE - GLA_RANK), (0, 0)))
    sr, z, sga, sgb, o = _mix_gla(x1, vec(mix_norm), wt_a, wt_b, wal, vec(b_alpha), vec(gla_head_norm), seq)
    x3 = _mix_xattn(x1, o, sr, z, sga, sgb, wua, pool_mix, vec(pool_scale), wub, wmo, vec(xa_norm), xwq, km, vm, xwo,
                    seq, mem_len)
    x4, _ = _ffn(x3, vec(ffn2_norm), *ffn2_w, vec(final_norm), last)
    return x4.reshape(batch, seq, D_MODEL)


def kernel(x, mem, ffn1_norm, ffn1_w1, ffn1_w3, ffn1_w2, mix_norm, w_in, w_alpha, b_alpha, gla_head_norm, w_up_a,
           pool_mix, pool_scale, w_up_b, w_mix_out, xa_norm, mem_norm, xa_wq, xa_wk, xa_wv, xa_wo, ffn2_norm,
           ffn2_w1, ffn2_w3, ffn2_w2, final_norm):
    depth = ffn1_norm.shape[0]
    for l in range(depth):
        last = l == depth - 1
        x = _layer(x, mem, ffn1_norm[l], ffn1_w1[l], ffn1_w3[l], ffn1_w2[l], mix_norm[l], w_in[l], w_alpha[l],
                   b_alpha[l], gla_head_norm[l], w_up_a[l], pool_mix[l], pool_scale[l], w_up_b[l], w_mix_out[l],
                   xa_norm[l], mem_norm[l], xa_wq[l], xa_wk[l], xa_wv[l], xa_wo[l], ffn2_norm[l], ffn2_w1[l],
                   ffn2_w3[l], ffn2_w2[l], final_norm, last)
    return x
```

```python
import functools
from typing import Callable, NamedTuple

import jax
import jax.numpy as jnp
from jax import lax
from jax.experimental import pallas as pl
from jax.experimental.pallas import tpu as pltpu

F32 = jnp.float32
BF16 = jnp.bfloat16

D_MODEL = 1024
D_FF = 2816
EPS = 1e-6
GLA_HEADS = 4
GLA_DK = 128
GLA_DV = 256
GLA_QK = GLA_HEADS * GLA_DK
GLA_VW = GLA_HEADS * GLA_DV
GLA_RANK = 16
GLA_GATE_TEMP = 16.0
GLA_CHUNK = 64
GLA_SUB = 16
GLA_WIDE = 256
GLA_WIDE_MAX_SPAN = 60.0
EXP_TO_ZERO = -1e30
POOL_GROUPS = 4
POOL_GC = 128
POOL_WIDTH = POOL_GROUPS * POOL_GC
POOL_WINDOWS = (2, 4, 8, 16)
POOL_HALO = 16
XA_HEADS = 4
XA_HD = 256

LANE = 128
BF16_ROWS = 16
W_IN_SIZES = (GLA_QK, GLA_QK, GLA_VW, GLA_VW, GLA_RANK, POOL_WIDTH, D_MODEL, D_MODEL)
W_IN_SPLIT = 4
ROW_TILE = 1024
XATTN_ROWS = 1024
FFN_ROWS = 1024
FF_SPLITS = (0, 768, 1536, 2304, D_FF)
VMEM_LIMIT = 56 * 1024 * 1024


def _rms(x, g):
    return x * lax.rsqrt(jnp.mean(x * x, axis=-1, keepdims=True) + EPS) * g


def _sigmoid(x):
    return 0.5 * jnp.tanh(0.5 * x) + 0.5


def _dot(a, b):
    return jnp.dot(a, b, preferred_element_type=F32)


def _resident(shape):
    nd = len(shape)
    return pl.BlockSpec(shape, lambda *_: (0,) * nd, pipeline_mode=pl.Buffered(1))


def _params():
    return pltpu.CompilerParams(dimension_semantics=("arbitrary",), vmem_limit_bytes=VMEM_LIMIT)


class _CastJob(NamedTuple):
    inputs: tuple
    in_specs: tuple
    out_specs: tuple
    out_shapes: tuple
    body: Callable


def _cast_rows_job(w, n_steps):
    rows, cols = w.shape
    per_step = -(-rows // n_steps)
    block = next(b for b in range(BF16_ROWS, rows + 1, BF16_ROWS) if rows % b == 0 and b >= per_step)
    n_blocks = rows // block
    spec = pl.BlockSpec((block, cols), lambda i: ((i * n_blocks) // n_steps, 0))

    def body(ins, outs):
        outs[0][...] = ins[0][...].astype(BF16)

    return _CastJob((w,), (spec,), (spec,), (jax.ShapeDtypeStruct((rows, cols), BF16),), body)


def _w_in_job(w_in_t, n_steps):
    d = w_in_t.shape[1]
    rows_a = sum(W_IN_SIZES[:W_IN_SPLIT])
    rows_b = sum(W_IN_SIZES[W_IN_SPLIT + 1:])
    assert w_in_t.shape[0] == rows_a + GLA_RANK + rows_b and rows_a % LANE == 0 and rows_b % LANE == 0
    assert rows_a % GLA_RANK == 0

    def split(rows):
        slabs = rows // LANE
        per = next(k for k in range(1, slabs + 1) if slabs % k == 0 and slabs // k <= n_steps)
        return LANE * per, slabs // per - 1

    (blk_a, last_a), (blk_b, last_b) = split(rows_a), split(rows_b)
    in_a = pl.BlockSpec((blk_a, d), lambda i: (jnp.minimum(i, last_a), 0))
    in_b = pl.BlockSpec((pl.Element(blk_b), pl.Element(d)),
                        lambda i: (pl.multiple_of(rows_a + GLA_RANK + blk_b * jnp.minimum(i, last_b), BF16_ROWS), 0))
    in_c = pl.BlockSpec((GLA_RANK, d), lambda i: (rows_a // GLA_RANK, 0))
    out_a = pl.BlockSpec((d, blk_a), lambda i: (0, jnp.minimum(i, last_a)))
    out_b = pl.BlockSpec((d, blk_b), lambda i: (0, jnp.minimum(i, last_b)))
    out_c = pl.BlockSpec((d, LANE), lambda i: (0, 0))

    def body(ins, outs):
        outs[0][...] = ins[0][...].T.astype(BF16)
        outs[1][...] = ins[1][...].T.astype(BF16)
        code = jnp.concatenate([ins[2][...], jnp.zeros((LANE - GLA_RANK, d), F32)], axis=0)
        outs[2][...] = code.T.astype(BF16)

    return _CastJob((w_in_t,) * 3, (in_a, in_b, in_c), (out_a, out_b, out_c),
                    tuple(jax.ShapeDtypeStruct((d, n), BF16) for n in (rows_a, rows_b, LANE)), body)


def _call(kernel_fn, args, in_specs, out_specs, out_shape, jobs=(), *, grid, scratch_shapes=(), name):
    n_in, n_out = len(args), len(out_specs)
    job_args = [a for j in jobs for a in j.inputs]
    job_outs = [o for j in jobs for o in j.out_shapes]

    def body(*refs):
        j_in = refs[n_in:n_in + len(job_args)]
        first_out = n_in + len(job_args)
        j_out = refs[first_out + n_out:first_out + n_out + len(job_outs)]
        kernel_fn(*refs[:n_in], *refs[first_out:first_out + n_out], *refs[first_out + n_out + len(job_outs):])
        for j in jobs:
            j.body(j_in[:len(j.inputs)], j_out[:len(j.out_shapes)])
            j_in, j_out = j_in[len(j.inputs):], j_out[len(j.out_shapes):]

    outs = pl.pallas_call(
        body,
        grid=grid,
        in_specs=[*in_specs, *(sp for j in jobs for sp in j.in_specs)],
        out_specs=[*out_specs, *(sp for j in jobs for sp in j.out_specs)],
        out_shape=[*out_shape, *job_outs],
        scratch_shapes=list(scratch_shapes),
        compiler_params=_params(),
        name=name,
    )(*args, *job_args)
    return outs[:n_out], outs[n_out:]


def _ffn_kernel(x_ref, g_ref, w1_ref, w3_ref, w2_ref, fg_ref, o_ref, *, final_norm):
    x = x_ref[...]
    inv_rms = lax.rsqrt(jnp.mean(x * x, axis=-1, keepdims=True) + EPS)
    h = (x * g_ref[...]).astype(BF16)
    acc = jnp.zeros_like(x)
    for lo, hi in zip(FF_SPLITS[:-1], FF_SPLITS[1:]):
        a = _dot(h, w1_ref[:, lo:hi]) * inv_rms
        b = _dot(h, w3_ref[:, lo:hi]) * inv_rms
        act = (a * _sigmoid(a) * b).astype(BF16)
        acc = acc + _dot(act, w2_ref[lo:hi, :])
    y = x + 0.5 * acc
    if final_norm:
        y = _rms(y, fg_ref[...])
    o_ref[...] = y


def _ffn(x, g, w1, w3, w2, fg, final_norm, make_jobs=None):
    t = x.shape[0]
    steps = t // FFN_ROWS
    row = pl.BlockSpec((FFN_ROWS, D_MODEL), lambda i: (i, 0))
    (y,), cast = _call(
        functools.partial(_ffn_kernel, final_norm=final_norm),
        (x, g, w1, w3, w2, fg),
        [row, _resident((1, D_MODEL)), _resident((D_MODEL, D_FF)), _resident((D_MODEL, D_FF)),
         _resident((D_FF, D_MODEL)), _resident((1, D_MODEL))],
        [row], [jax.ShapeDtypeStruct((t, D_MODEL), F32)],
        make_jobs(steps) if make_jobs else (),
        grid=(steps,), name="ffn_final" if final_norm else "ffn")
    return y, cast


def _cumsum_rows(mask_bf16, la):
    hi = la.astype(BF16)
    lo = (la - hi.astype(F32)).astype(BF16)
    return _dot(mask_bf16, hi) + _dot(mask_bf16, lo)


def _decay_column(b_last_row):
    return jnp.transpose(jnp.broadcast_to(jnp.exp(b_last_row), (8, GLA_DK)))[:, 0:1]


def _gla_wide_task(q_ref, k_ref, v_ref, la_ref, gn, o_ref, s_ref):
    c = GLA_WIDE
    ri = lax.broadcasted_iota(jnp.int32, (c, c), 0)
    ci = lax.broadcasted_iota(jnp.int32, (c, c), 1)
    causal = ci <= ri
    cum_incl = causal.astype(BF16)
    row_sl = [slice(ic * c, (ic + 1) * c) for ic in range(ROW_TILE // c)]
    b_all = [_dot(cum_incl, la_ref[rows, :].astype(BF16)) for rows in row_sl]
    units = [(ic, hd) for ic in range(len(row_sl)) for hd in range(GLA_HEADS)]

    def factors(ic, hd):
        rows, kc = row_sl[ic], slice(hd * GLA_DK, (hd + 1) * GLA_DK)
        b = b_all[ic][:, kc]
        qg = (q_ref[rows, kc].astype(F32) * jnp.exp(b)).astype(BF16)
        kg = (k_ref[rows, kc].astype(F32) * jnp.exp(-b)).astype(BF16)
        s = lax.dot_general(qg, kg, (((1,), (1,)), ((), ())), preferred_element_type=F32)
        return qg, kg, s, b[c - 1:c, :]

    nxt = factors(*units[0])
    yield
    for n, (ic, hd) in enumerate(units):
        qg, kg, s, b_last = nxt
        if n + 1 < len(units):
            nxt = factors(*units[n + 1])
        rows, vc = row_sl[ic], slice(hd * GLA_DV, (hd + 1) * GLA_DV)
        v = v_ref[rows, vc]
        s_old = s_ref[hd]
        inter = _dot(qg, s_old.astype(BF16))
        upd = lax.dot_general(kg, v, (((0,), (0,)), ((), ())), preferred_element_type=F32)
        p = jnp.where(causal, s, 0.0).astype(BF16)
        o = _dot(p, v) + inter
        s_ref[hd] = (s_old + upd) * _decay_column(b_last)
        o_ref[rows, vc] = _rms(o, gn).astype(BF16)
        yield


def _gla_anchored_chunks(q_ref, k_ref, v_ref, la_ref, gn, o_ref, s_ref):
    c = GLA_CHUNK
    ri = lax.broadcasted_iota(jnp.int32, (c, c), 0)
    ci = lax.broadcasted_iota(jnp.int32, (c, c), 1)
    causal = ci <= ri
    cum_incl = causal.astype(BF16)
    cum_anchor = (ci < (ri // GLA_SUB) * GLA_SUB).astype(BF16)
    krow = lax.broadcasted_iota(jnp.int32, (c, 1), 0)
    kcol = lax.broadcasted_iota(jnp.int32, (GLA_SUB, c), 1)

    def chunk(ic, carry):
        r0 = pl.multiple_of(ic * c, c)
        la = la_ref[pl.ds(r0, c), :]
        b_all = _cumsum_rows(cum_incl, la)
        beta_all = _cumsum_rows(cum_anchor, la)
        for hd in range(GLA_HEADS):
            kc = slice(hd * GLA_DK, (hd + 1) * GLA_DK)
            vc = slice(hd * GLA_DV, (hd + 1) * GLA_DV)
            q = q_ref[pl.ds(r0, c), kc].astype(F32)
            k = k_ref[pl.ds(r0, c), kc].astype(F32)
            v = v_ref[pl.ds(r0, c), vc]
            b = b_all[:, kc]
            beta = beta_all[:, kc]
            qa = (q * jnp.exp(b - beta)).astype(BF16)
            rows = []
            for blk in range(c // GLA_SUB):
                r_lo, r_hi = blk * GLA_SUB, (blk + 1) * GLA_SUB
                e = jnp.where(krow < r_lo, beta[r_lo:r_lo + 1, :] - b, EXP_TO_ZERO)
                ka = (k * jnp.exp(e)).astype(BF16)
                sc = lax.dot_general(qa[r_lo:r_hi, :], ka, (((1,), (1,)), ((), ())), preferred_element_type=F32)
                qb, bb = q[r_lo:r_hi, :], b[r_lo:r_hi, :]
                for j in range(r_lo, r_hi):
                    w = jnp.exp(jnp.minimum(bb - b[j:j + 1, :], 0.0))
                    sc = jnp.where(kcol == j, jnp.sum(qb * k[j:j + 1, :] * w, axis=1, keepdims=True), sc)
                rows.append(sc)
            p = jnp.where(causal, jnp.concatenate(rows, axis=0), 0.0).astype(BF16)
            s_old = s_ref[hd]
            o = _dot(p, v) + _dot((q * jnp.exp(b)).astype(BF16), s_old.astype(BF16))
            b_last = b[c - 1:c, :]
            k_dec = (k * jnp.exp(b_last - b)).astype(BF16)
            upd = lax.dot_general(k_dec, v, (((0,), (0,)), ((), ())), preferred_element_type=F32)
            s_ref[hd] = s_old * _decay_column(b_last) + upd
            o_ref[pl.ds(r0, c), vc] = _rms(o, gn).astype(BF16)
        return carry

    lax.fori_loop(0, ROW_TILE // c, chunk, 0)


def _interleave(tasks):
    tasks = list(tasks)
    while tasks:
        for t in list(tasks):
            try:
                next(t)
            except StopIteration:
                tasks.remove(t)


def _mix_in_task(x_ref, g_ref, wq_ref, wk_ref, wv_ref, wr_ref, wa_ref, wu_ref, wga_ref, wgb_ref, wal_ref, bal_ref,
                 sr_ref, z_ref, sga_ref, sgb_ref, ubuf, tile_in_seq, gla_inputs):
    h = _rms(x_ref[...], g_ref[...]).astype(BF16)
    a_code = _dot(h, wa_ref[...]).astype(BF16)
    yield
    u = _dot(h, wu_ref[...])
    yield
    r = _dot(h, wr_ref[...])
    sr_ref[...] = (r * _sigmoid(r)).astype(BF16)
    yield
    zg = _dot(a_code, wal_ref[...]) + bal_ref[...]
    yield
    sga_ref[...] = _sigmoid(_dot(h, wga_ref[...])).astype(BF16)
    yield
    sgb_ref[...] = _sigmoid(_dot(h, wgb_ref[...])).astype(BF16)
    yield

    ubuf[POOL_HALO:POOL_HALO + ROW_TILE, :] = u
    pos = lax.broadcasted_iota(jnp.int32, (ROW_TILE, 1), 0) + tile_in_seq * ROW_TILE
    for g, w in enumerate(POOL_WINDOWS):
        cols = slice(g * POOL_GC, (g + 1) * POOL_GC)
        s = ubuf[:, cols]
        sh = 1
        while sh < w:
            s = s + pltpu.roll(s, sh, 0)
            sh *= 2
        cnt = jnp.minimum(pos + 1, w).astype(F32)
        z_ref[:, cols] = (s[POOL_HALO:, :] / cnt - u[:, cols]).astype(BF16)
    ubuf[0:POOL_HALO, :] = ubuf[ROW_TILE:ROW_TILE + POOL_HALO, :]

    q = (_dot(h, wq_ref[...]) * (GLA_DK ** -0.5)).astype(BF16)
    yield
    v = _dot(h, wv_ref[...]).astype(BF16)
    yield
    k = _dot(h, wk_ref[...]).astype(BF16)
    la = (jnp.minimum(zg, 0.0) - jnp.log(1.0 + jnp.exp(-jnp.abs(zg)))) * (1.0 / GLA_GATE_TEMP)
    span = None
    for r0 in range(0, ROW_TILE, GLA_WIDE):
        sp = jnp.max(-jnp.sum(la[r0:r0 + GLA_WIDE, :], axis=0, keepdims=True))
        span = sp if span is None else jnp.maximum(span, sp)
    gla_inputs.extend((q, k, v, la, span))


def _mix_gla_kernel(x_ref, g_ref, w_a_ref, w_b_ref, wa_ref, wal_ref, bal_ref, gn_ref,
                    sr_ref, z_ref, sga_ref, sgb_ref, o_ref,
                    q_s, k_s, v_s, la_s, s_ref, ubuf, span_s, *, tiles_per_seq, n_tiles):
    step = pl.program_id(0)
    mix_tile = jnp.minimum(step, n_tiles - 1)
    gla_tile = step - 1

    @pl.when(step == 0)
    def _():
        q_s[...] = jnp.zeros_like(q_s)
        k_s[...] = jnp.zeros_like(k_s)
        v_s[...] = jnp.zeros_like(v_s)
        la_s[...] = jnp.zeros_like(la_s)
        span_s[0] = 0.0

    @pl.when(mix_tile % tiles_per_seq == 0)
    def _():
        ubuf[0:POOL_HALO, :] = jnp.zeros((POOL_HALO, POOL_WIDTH), F32)

    @pl.when(jnp.logical_or(step == 0, gla_tile % tiles_per_seq == 0))
    def _():
        s_ref[...] = jnp.zeros_like(s_ref)

    views, lo = [], 0
    for n in W_IN_SIZES[:W_IN_SPLIT]:
        views.append(w_a_ref.at[:, lo:lo + n])
        lo += n
    lo = 0
    for n in W_IN_SIZES[W_IN_SPLIT + 1:]:
        views.append(w_b_ref.at[:, lo:lo + n])
        lo += n
    wq_ref, wk_ref, wv_ref, wr_ref, wu_ref, wga_ref, wgb_ref = views
    mix_args = (x_ref, g_ref, wq_ref, wk_ref, wv_ref, wr_ref, wa_ref, wu_ref, wga_ref, wgb_ref, wal_ref, bal_ref,
                sr_ref, z_ref, sga_ref, sgb_ref, ubuf, mix_tile % tiles_per_seq)
    gla_args = (q_s, k_s, v_s, la_s, gn_ref[...], o_ref, s_ref)

    def run(gla_tasks):
        nxt = []
        _interleave(gla_tasks + [_mix_in_task(*mix_args, nxt)])
        q_s[...], k_s[...], v_s[...], la_s[...], span_s[0] = nxt

    wide_ok = span_s[0] < GLA_WIDE_MAX_SPAN
    last = step == n_tiles

    @pl.when(jnp.logical_and(wide_ok, jnp.logical_not(last)))
    def _():
        run([_gla_wide_task(*gla_args)])

    @pl.when(jnp.logical_and(wide_ok, last))
    def _():
        _interleave([_gla_wide_task(*gla_args)])

    @pl.when(jnp.logical_not(wide_ok))
    def _():
        _gla_anchored_chunks(*gla_args)
        pl.when(jnp.logical_not(last))(lambda: run([]))


def _mix_gla(x, g, w_a, w_b, wa, wal, bal, gn, seq):
    t = x.shape[0]
    n_tiles = t // ROW_TILE

    def cur(n):
        return pl.BlockSpec((ROW_TILE, n), lambda i: (jnp.minimum(i, n_tiles - 1), 0))

    prev = pl.BlockSpec((ROW_TILE, GLA_VW), lambda i: (jnp.maximum(i - 1, 0), 0))

    def out(n):
        return jax.ShapeDtypeStruct((t, n), BF16)

    outs, _ = _call(
        functools.partial(_mix_gla_kernel, tiles_per_seq=seq // ROW_TILE, n_tiles=n_tiles),
        (x, g, w_a, w_b, wa, wal, bal, gn),
        [cur(D_MODEL), _resident((1, D_MODEL)), _resident(w_a.shape), _resident(w_b.shape), _resident(wa.shape),
         _resident((LANE, GLA_QK)), _resident((1, GLA_QK)), _resident((1, GLA_DV))],
        [cur(GLA_VW), cur(POOL_WIDTH), cur(D_MODEL), cur(D_MODEL), prev],
        [out(GLA_VW), out(POOL_WIDTH), out(D_MODEL), out(D_MODEL), out(GLA_VW)],
        grid=(n_tiles + 1,),
        scratch_shapes=[pltpu.VMEM((ROW_TILE, GLA_QK), BF16), pltpu.VMEM((ROW_TILE, GLA_QK), BF16),
                        pltpu.VMEM((ROW_TILE, GLA_VW), BF16), pltpu.VMEM((ROW_TILE, GLA_QK), F32),
                        pltpu.VMEM((GLA_HEADS, GLA_DK, GLA_DV), F32),
                        pltpu.VMEM((POOL_HALO + ROW_TILE, POOL_WIDTH), F32),
                        pltpu.SMEM((1,), F32)],
        name="mix_gla")
    return outs


def _mem_kv_kernel(m_ref, g_ref, wk_ref, wv_ref, k_ref, v_ref, wk_s, wv_s):
    @pl.when(pl.program_id(0) == 0)
    def _():
        wk_s[...] = wk_ref[...].astype(BF16)
        wv_s[...] = wv_ref[...].astype(BF16)

    m = _rms(m_ref[...], g_ref[...]).astype(BF16)
    k_ref[...] = _dot(m, wk_s[...]).astype(BF16)
    v_ref[...] = _dot(m, wv_s[...]).astype(BF16)


def _mem_kv(mem, g, wk, wv, mem_len, make_jobs):
    t = mem.shape[0]
    steps = t // mem_len
    row = pl.BlockSpec((mem_len, D_MODEL), lambda i: (i, 0))
    out = jax.ShapeDtypeStruct((t, D_MODEL), BF16)
    return _call(
        _mem_kv_kernel, (mem, g, wk, wv),
        [row, _resident((1, D_MODEL)), _resident((D_MODEL, D_MODEL)), _resident((D_MODEL, D_MODEL))],
        [row, row], [out, out], make_jobs(steps), grid=(steps,),
        scratch_shapes=[pltpu.VMEM((D_MODEL, D_MODEL), BF16)] * 2, name="mem_kv")


def _mix_xattn_kernel(x_ref, o_ref, sr_ref, z_ref, sga_ref, sgb_ref, wua_ref, pm_ref, ps_ref, wub_ref, wmo_ref,
                      xg_ref, wq_ref, k_ref, v_ref, wo_ref, y_ref):
    zs = []
    for g in range(POOL_GROUPS):
        cols = slice(g * POOL_GC, (g + 1) * POOL_GC)
        zs.append((_dot(z_ref[:, cols], pm_ref[g].astype(BF16)) * ps_ref[:, cols]).astype(BF16))
    ya = _dot(o_ref[...] * sr_ref[...], wua_ref[...])
    yb = _dot(jnp.concatenate(zs, axis=1), wub_ref[...])
    merged = (sga_ref[...].astype(F32) * ya + sgb_ref[...].astype(F32) * yb).astype(BF16)
    x = x_ref[...] + _dot(merged, wmo_ref[...])

    inv_rms = lax.rsqrt(jnp.mean(x * x, axis=-1, keepdims=True) + EPS)
    q = (_dot((x * xg_ref[...]).astype(BF16), wq_ref[...]) * (inv_rms * (XA_HD ** -0.5))).astype(BF16)

    def scores(hd):
        cols = slice(hd * XA_HD, (hd + 1) * XA_HD)
        return lax.dot_general(q[:, cols], k_ref[:, cols], (((1,), (1,)), ((), ())), preferred_element_type=F32)

    outs = []
    s = scores(0)
    for hd in range(XA_HEADS):
        s_next = scores(hd + 1) if hd + 1 < XA_HEADS else None
        e = jnp.exp(s - jnp.max(s, axis=-1, keepdims=True))
        o = _dot(e.astype(BF16), v_ref[:, hd * XA_HD:(hd + 1) * XA_HD])
        outs.append((o / jnp.sum(e, axis=-1, keepdims=True)).astype(BF16))
        s = s_next
    y_ref[...] = x + _dot(jnp.concatenate(outs, axis=1), wo_ref[...])


def _mix_xattn(x, o, sr, z, sga, sgb, wua, pm, ps, wub, wmo, xg, wq, km, vm, wo, seq, mem_len):
    t = x.shape[0]
    steps = t // XATTN_ROWS
    tiles_per_seq = seq // XATTN_ROWS

    def row(n):
        return pl.BlockSpec((XATTN_ROWS, n), lambda i: (i, 0))

    mem =pl.BlockSpec((mem_len, D_MODEL), lambda i: (i // tiles_per_seq, 0))
    (y,), _ = _call(
        _mix_xattn_kernel, (x, o, sr, z, sga, sgb, wua, pm, ps, wub, wmo, xg, wq, km, vm, wo),
        [row(D_MODEL), row(GLA_VW), row(GLA_VW), row(POOL_WIDTH), row(D_MODEL), row(D_MODEL),
         _resident((GLA_VW, D_MODEL)), _resident((POOL_GROUPS, POOL_GC, POOL_GC)),
         _resident((1, POOL_WIDTH)), _resident((POOL_WIDTH, D_MODEL)), _resident((D_MODEL, D_MODEL)),
         _resident((1, D_MODEL)), _resident((D_MODEL, D_MODEL)), mem, mem, _resident((D_MODEL, D_MODEL))],
        [row(D_MODEL)], [jax.ShapeDtypeStruct((t, D_MODEL), F32)],
        grid=(steps,), name="mix_xattn")
    return y


def _layer(x, mem, ffn1_norm, ffn1_w1, ffn1_w3, ffn1_w2, mix_norm, w_in, w_alpha, b_alpha, gla_head_norm,
           w_up_a, pool_mix, pool_scale, w_up_b, w_mix_out, xa_norm, mem_norm, xa_wq, xa_wk, xa_wv, xa_wo,
           ffn2_norm, ffn2_w1, ffn2_w3, ffn2_w2, final_norm, last):
    batch, seq, _ = x.shape
    mem_len = mem.shape[1]
    assert seq % ROW_TILE == 0 and ROW_TILE % GLA_CHUNK == 0 and ROW_TILE % GLA_WIDE == 0

    def vec(a):
        return a.reshape(1, -1).astype(F32)

    def casts(*ws):
        return lambda n_steps: tuple(_cast_rows_job(w, n_steps) for w in ws)

    (km, vm), ffn1_w = _mem_kv(mem.reshape(batch * mem_len, D_MODEL), vec(mem_norm), xa_wk, xa_wv, mem_len,
                               casts(ffn1_w1, ffn1_w3, ffn1_w2))
    later = (w_up_a, w_up_b, w_mix_out, xa_wq, xa_wo, ffn2_w1, ffn2_w3, ffn2_w2)
    x1, cast = _ffn(x.reshape(batch * seq, D_MODEL), vec(ffn1_norm), *ffn1_w, vec(final_norm), False,
                    lambda n_steps: (_w_in_job(jnp.swapaxes(w_in, 0, 1), n_steps), *casts(*later)(n_steps)))
    w_a, w_b, wa, wua, wub, wmo, xwq, xwo, *ffn2_w = cast
    wal = jnp.pad(w_alpha.astype(BF16), ((0, LANE - GLA_RANK), (0, 0)))
    sr, z, sga, sgb, o = _mix_gla(x1, vec(mix_norm), w_a, w_b, wa, wal, vec(b_alpha), vec(gla_head_norm), seq)
    x3 = _mix_xattn(x1, o, sr, z, sga, sgb, wua, pool_mix, vec(pool_scale), wub, wmo, vec(xa_norm), xwq, km, vm, xwo,
                    seq, mem_len)
    x4, _ = _ffn(x3, vec(ffn2_norm), *ffn2_w, vec(final_norm), last)
    return x4.reshape(batch, seq, D_MODEL)


def kernel(x, mem, ffn1_norm, ffn1_w1, ffn1_w3, ffn1_w2, mix_norm, w_in, w_alpha, b_alpha, gla_head_norm, w_up_a,
           pool_mix, pool_scale, w_up_b, w_mix_out, xa_norm, mem_norm, xa_wq, xa_wk, xa_wv, xa_wo, ffn2_norm,
           ffn2_w1, ffn2_w3, ffn2_w2, final_norm):
    depth = ffn1_norm.shape[0]
    for l in range(depth):
        last = l == depth - 1
        x = _layer(x, mem, ffn1_norm[l], ffn1_w1[l], ffn1_w3[l], ffn1_w2[l], mix_norm[l], w_in[l], w_alpha[l],
                   b_alpha[l], gla_head_norm[l], w_up_a[l], pool_mix[l], pool_scale[l], w_up_b[l], w_mix_out[l],
                   xa_norm[l], mem_norm[l], xa_wq[l], xa_wk[l], xa_wv[l], xa_wo[l], ffn2_norm[l], ffn2_w1[l],
                   ffn2_w3[l], ffn2_w2[l], final_norm, last)
    return x
```

```python
import functools
from typing import Callable, NamedTuple

import jax
import jax.numpy as jnp
from jax import lax
from jax.experimental import pallas as pl
from jax.experimental.pallas import tpu as pltpu

F32 = jnp.float32
BF16 = jnp.bfloat16

D_MODEL = 1024
D_FF = 2816
EPS = 1e-6
GLA_HEADS = 4
GLA_DK = 128
GLA_DV = 256
GLA_QK = GLA_HEADS * GLA_DK
GLA_VW = GLA_HEADS * GLA_DV
GLA_RANK = 16
GLA_GATE_TEMP = 16.0
GLA_CHUNK = 64
GLA_SUB = 16
GLA_WIDE = 256
GLA_WIDE_MAX_SPAN = 60.0
EXP_TO_ZERO = -1e30
POOL_GROUPS = 4
POOL_GC = 128
POOL_WIDTH = POOL_GROUPS * POOL_GC
POOL_WINDOWS = (2, 4, 8, 16)
POOL_HALO = 16
XA_HEADS = 4
XA_HD = 256

LANE = 128
BF16_ROWS = 16
W_IN_SIZES = (GLA_QK, GLA_QK, GLA_VW, GLA_VW, GLA_RANK, POOL_WIDTH, D_MODEL, D_MODEL)
W_IN_SPLIT = 4
ROW_TILE = 512
XATTN_ROWS = 1024
FFN_ROWS = 1024
FF_SPLITS = (0, 768, 1536, 2304, D_FF)
VMEM_LIMIT = 56 * 1024 * 1024


def _rms(x, g):
    return x * lax.rsqrt(jnp.mean(x * x, axis=-1, keepdims=True) + EPS) * g


def _sigmoid(x):
    return 0.5 * jnp.tanh(0.5 * x) + 0.5


def _dot(a, b):
    return jnp.dot(a, b, preferred_element_type=F32)


def _resident(shape):
    nd = len(shape)
    return pl.BlockSpec(shape, lambda *_: (0,) * nd, pipeline_mode=pl.Buffered(1))


def _params():
    return pltpu.CompilerParams(dimension_semantics=("arbitrary",), vmem_limit_bytes=VMEM_LIMIT)


class _CastJob(NamedTuple):
    inputs: tuple
    in_specs: tuple
    out_specs: tuple
    out_shapes: tuple
    body: Callable


def _cast_rows_job(w, n_steps):
    rows, cols = w.shape
    per_step = -(-rows // n_steps)
    block = next(b for b in range(BF16_ROWS, rows + 1, BF16_ROWS) if rows % b == 0 and b >= per_step)
    n_blocks = rows // block
    spec = pl.BlockSpec((block, cols), lambda i: ((i * n_blocks) // n_steps, 0))

    def body(ins, outs):
        outs[0][...] = ins[0][...].astype(BF16)

    return _CastJob((w,), (spec,), (spec,), (jax.ShapeDtypeStruct((rows, cols), BF16),), body)


def _w_in_job(w_in_t, n_steps):
    cols = w_in_t.shape[1]
    rows_a = sum(W_IN_SIZES[:W_IN_SPLIT])
    rows_b = sum(W_IN_SIZES[W_IN_SPLIT:])
    assert w_in_t.shape[0] == rows_a + rows_b and rows_a % n_steps == 0
    blk_a = rows_a // n_steps
    per_b = -(-rows_b // n_steps)
    blk_b = next(r for r in range(BF16_ROWS, rows_b + 1, BF16_ROWS) if rows_b % r == 0 and r >= per_b)
    last_b = rows_b // blk_b - 1
    assert blk_a % BF16_ROWS == 0
    in_a = pl.BlockSpec((blk_a, cols), lambda i: (i, 0))
    in_b = pl.BlockSpec((pl.Element(blk_b), pl.Element(cols)),
                        lambda i: (pl.multiple_of(rows_a + blk_b * jnp.minimum(i, last_b), BF16_ROWS), 0))
    out_b = pl.BlockSpec((blk_b, cols), lambda i: (jnp.minimum(i, last_b), 0))

    def body(ins, outs):
        for i_ref, o_ref in zip(ins, outs):
            o_ref[...] = i_ref[...].astype(BF16)

    return _CastJob((w_in_t, w_in_t), (in_a, in_b), (in_a, out_b),
                    (jax.ShapeDtypeStruct((rows_a, cols), BF16), jax.ShapeDtypeStruct((rows_b, cols), BF16)), body)


def _call(kernel_fn, args, in_specs, out_specs, out_shape, jobs=(), *, grid, scratch_shapes=(), name):
    n_in, n_out = len(args), len(out_specs)
    job_args = [a for j in jobs for a in j.inputs]
    job_outs = [o for j in jobs for o in j.out_shapes]

    def body(*refs):
        j_in = refs[n_in:n_in + len(job_args)]
        first_out = n_in + len(job_args)
        j_out = refs[first_out + n_out:first_out + n_out + len(job_outs)]
        kernel_fn(*refs[:n_in], *refs[first_out:first_out + n_out], *refs[first_out + n_out + len(job_outs):])
        for j in jobs:
            j.body(j_in[:len(j.inputs)], j_out[:len(j.out_shapes)])
            j_in, j_out = j_in[len(j.inputs):], j_out[len(j.out_shapes):]

    outs = pl.pallas_call(
        body,
        grid=grid,
        in_specs=[*in_specs, *(sp for j in jobs for sp in j.in_specs)],
        out_specs=[*out_specs, *(sp for j in jobs for sp in j.out_specs)],
        out_shape=[*out_shape, *job_outs],
        scratch_shapes=list(scratch_shapes),
        compiler_params=_params(),
        name=name,
    )(*args, *job_args)
    return outs[:n_out], outs[n_out:]


def _ffn_kernel(x_ref, g_ref, w1_ref, w3_ref, w2_ref, fg_ref, o_ref, *, final_norm):
    x = x_ref[...]
    inv_rms = lax.rsqrt(jnp.mean(x * x, axis=-1, keepdims=True) + EPS)
    h = (x * g_ref[...]).astype(BF16)
    acc = jnp.zeros_like(x)
    for lo, hi in zip(FF_SPLITS[:-1], FF_SPLITS[1:]):
        a = _dot(h, w1_ref[:, lo:hi]) * inv_rms
        b = _dot(h, w3_ref[:, lo:hi]) * inv_rms
        act = (a * _sigmoid(a) * b).astype(BF16)
        acc = acc + _dot(act, w2_ref[lo:hi, :])
    y = x + 0.5 * acc
    if final_norm:
        y = _rms(y, fg_ref[...])
    o_ref[...] = y


def _ffn(x, g, w1, w3, w2, fg, final_norm, make_jobs=None):
    t = x.shape[0]
    steps = t // FFN_ROWS
    row = pl.BlockSpec((FFN_ROWS, D_MODEL), lambda i: (i, 0))
    (y,), cast = _call(
        functools.partial(_ffn_kernel, final_norm=final_norm),
        (x, g, w1, w3, w2, fg),
        [row, _resident((1, D_MODEL)), _resident((D_MODEL, D_FF)), _resident((D_MODEL, D_FF)),
         _resident((D_FF, D_MODEL)), _resident((1, D_MODEL))],
        [row], [jax.ShapeDtypeStruct((t, D_MODEL), F32)],
        make_jobs(steps) if make_jobs else (),
        grid=(steps,), name="ffn_final" if final_norm else "ffn")
    return y, cast


def _cumsum_rows(mask_bf16, la):
    hi = la.astype(BF16)
    lo = (la - hi.astype(F32)).astype(BF16)
    return _dot(mask_bf16, hi) + _dot(mask_bf16, lo)


def _decay_column(b_last_row):
    return jnp.transpose(jnp.broadcast_to(jnp.exp(b_last_row), (8, GLA_DK)))[:, 0:1]


def _gla_wide_task(q_ref, k_ref, v_ref, la_ref, gn, o_ref, s_ref):
    c = GLA_WIDE
    ri = lax.broadcasted_iota(jnp.int32, (c, c), 0)
    ci = lax.broadcasted_iota(jnp.int32, (c, c), 1)
    causal = ci <= ri
    cum_incl = causal.astype(BF16)
    row_sl = [slice(ic * c, (ic + 1) * c) for ic in range(ROW_TILE // c)]
    b_all = [_dot(cum_incl, la_ref[rows, :].astype(BF16)) for rows in row_sl]
    units = [(ic, hd) for ic in range(len(row_sl)) for hd in range(GLA_HEADS)]

    def factors(ic, hd):
        rows, kc = row_sl[ic], slice(hd * GLA_DK, (hd + 1) * GLA_DK)
        b = b_all[ic][:, kc]
        qg = (q_ref[rows, kc].astype(F32) * jnp.exp(b)).astype(BF16)
        kg = (k_ref[rows, kc].astype(F32) * jnp.exp(-b)).astype(BF16)
        s = lax.dot_general(qg, kg, (((1,), (1,)), ((), ())), preferred_element_type=F32)
        return qg, kg, s, b[c - 1:c, :]

    nxt = factors(*units[0])
    yield
    for n, (ic, hd) in enumerate(units):
        qg, kg, s, b_last = nxt
        if n + 1 < len(units):
            nxt = factors(*units[n + 1])
        rows, vc = row_sl[ic], slice(hd * GLA_DV, (hd + 1) * GLA_DV)
        v = v_ref[rows, vc]
        s_old = s_ref[hd]
        inter = _dot(qg, s_old.astype(BF16))
        upd = lax.dot_general(kg, v, (((0,), (0,)), ((), ())), preferred_element_type=F32)
        p = jnp.where(causal, s, 0.0).astype(BF16)
        o = _dot(p, v) + inter
        s_ref[hd] = (s_old + upd) * _decay_column(b_last)
        o_ref[rows, vc] = _rms(o, gn).astype(BF16)
        yield


def _gla_anchored_chunks(q_ref, k_ref, v_ref, la_ref, gn, o_ref, s_ref):
    c = GLA_CHUNK
    ri = lax.broadcasted_iota(jnp.int32, (c, c), 0)
    ci = lax.broadcasted_iota(jnp.int32, (c, c), 1)
    causal = ci <= ri
    cum_incl = causal.astype(BF16)
    cum_anchor = (ci < (ri // GLA_SUB) * GLA_SUB).astype(BF16)
    krow = lax.broadcasted_iota(jnp.int32, (c, 1), 0)
    kcol = lax.broadcasted_iota(jnp.int32, (GLA_SUB, c), 1)

    def chunk(ic, carry):
        r0 = pl.multiple_of(ic * c, c)
        la = la_ref[pl.ds(r0, c), :]
        b_all = _cumsum_rows(cum_incl, la)
        beta_all = _cumsum_rows(cum_anchor, la)
        for hd in range(GLA_HEADS):
            kc = slice(hd * GLA_DK, (hd + 1) * GLA_DK)
            vc = slice(hd * GLA_DV, (hd + 1) * GLA_DV)
            q = q_ref[pl.ds(r0, c), kc].astype(F32)
            k = k_ref[pl.ds(r0, c), kc].astype(F32)
            v = v_ref[pl.ds(r0, c), vc]
            b = b_all[:, kc]
            beta = beta_all[:, kc]
            qa = (q * jnp.exp(b - beta)).astype(BF16)
            rows = []
            for blk in range(c // GLA_SUB):
                r_lo, r_hi = blk * GLA_SUB, (blk + 1) * GLA_SUB
                e = jnp.where(krow < r_lo, beta[r_lo:r_lo + 1, :] - b, EXP_TO_ZERO)
                ka = (k * jnp.exp(e)).astype(BF16)
                sc = lax.dot_general(qa[r_lo:r_hi, :], ka, (((1,), (1,)), ((), ())), preferred_element_type=F32)
                qb, bb = q[r_lo:r_hi, :], b[r_lo:r_hi, :]
                for j in range(r_lo, r_hi):
                    w = jnp.exp(jnp.minimum(bb - b[j:j + 1, :], 0.0))
                    sc = jnp.where(kcol == j, jnp.sum(qb * k[j:j + 1, :] * w, axis=1, keepdims=True), sc)
                rows.append(sc)
            p = jnp.where(causal, jnp.concatenate(rows, axis=0), 0.0).astype(BF16)
            s_old = s_ref[hd]
            o = _dot(p, v) + _dot((q * jnp.exp(b)).astype(BF16), s_old.astype(BF16))
            b_last = b[c - 1:c, :]
            k_dec = (k * jnp.exp(b_last - b)).astype(BF16)
            upd = lax.dot_general(k_dec, v, (((0,), (0,)), ((), ())), preferred_element_type=F32)
            s_ref[hd] = s_old * _decay_column(b_last) + upd
            o_ref[pl.ds(r0, c), vc] = _rms(o, gn).astype(BF16)
        return carry

    lax.fori_loop(0, ROW_TILE // c, chunk, 0)


def _interleave(tasks):
    tasks = list(tasks)
    while tasks:
        for t in list(tasks):
            try:
                next(t)
            except StopIteration:
                tasks.remove(t)


def _mix_in_task(x_ref, g_ref, wq_ref, wk_ref, wv_ref, wr_ref, wa_ref, wu_ref, wga_ref, wgb_ref, wal_ref, bal_ref,
                 sr_ref, z_ref, sga_ref, sgb_ref, ubuf, tile_in_seq, gla_inputs):
    h = _rms(x_ref[...], g_ref[...]).astype(BF16)
    a_code = _dot(h, wa_ref[...]).astype(BF16)
    yield
    u = _dot(h, wu_ref[...])
    yield
    r = _dot(h, wr_ref[...])
    sr_ref[...] = (r * _sigmoid(r)).astype(BF16)
    yield
    zg = _dot(a_code, wal_ref[...]) + bal_ref[...]
    yield
    sga_ref[...] = _sigmoid(_dot(h, wga_ref[...])).astype(BF16)
    yield
    sgb_ref[...] = _sigmoid(_dot(h, wgb_ref[...])).astype(BF16)
    yield

    ubuf[POOL_HALO:POOL_HALO + ROW_TILE, :] = u
    pos = lax.broadcasted_iota(jnp.int32, (ROW_TILE, 1), 0) + tile_in_seq * ROW_TILE
    for g, w in enumerate(POOL_WINDOWS):
        cols = slice(g * POOL_GC, (g + 1) * POOL_GC)
        s = ubuf[:, cols]
        sh = 1
        while sh < w:
            s = s + pltpu.roll(s, sh, 0)
            sh *= 2
        cnt = jnp.minimum(pos + 1, w).astype(F32)
        z_ref[:, cols] = (s[POOL_HALO:, :] / cnt - u[:, cols]).astype(BF16)
    ubuf[0:POOL_HALO, :] = ubuf[ROW_TILE:ROW_TILE + POOL_HALO, :]

    q = (_dot(h, wq_ref[...]) * (GLA_DK ** -0.5)).astype(BF16)
    yield
    v = _dot(h, wv_ref[...]).astype(BF16)
    yield
    k = _dot(h, wk_ref[...]).astype(BF16)
    la = (jnp.minimum(zg, 0.0) - jnp.log(1.0 + jnp.exp(-jnp.abs(zg)))) * (1.0 / GLA_GATE_TEMP)
    span = None
    for r0 in range(0, ROW_TILE, GLA_WIDE):
        sp = jnp.max(-jnp.sum(la[r0:r0 + GLA_WIDE, :], axis=0, keepdims=True))
        span = sp if span is None else jnp.maximum(span, sp)
    gla_inputs.extend((q, k, v, la, span))


def _mix_gla_kernel(x_ref, g_ref, wt_a_ref, wt_b_ref, wal_ref, bal_ref, gn_ref,
                    sr_ref, z_ref, sga_ref, sgb_ref, o_ref,
                    q_s, k_s, v_s, la_s, s_ref, ubuf, span_s,
                    wq_ref, wk_ref, wv_ref, wr_ref, wa_ref, wu_ref, wga_ref, wgb_ref, *, tiles_per_seq, n_tiles):
    step = pl.program_id(0)
    mix_tile = jnp.minimum(step, n_tiles - 1)
    gla_tile = step - 1

    @pl.when(step == 0)
    def _():
        q_s[...] = jnp.zeros_like(q_s)
        k_s[...] = jnp.zeros_like(k_s)
        v_s[...] = jnp.zeros_like(v_s)
        la_s[...] = jnp.zeros_like(la_s)
        span_s[0] = 0.0
        lo = 0
        for w_ref, n in zip((wq_ref, wk_ref, wv_ref, wr_ref), W_IN_SIZES[:W_IN_SPLIT]):
            w_ref[...] = wt_a_ref[lo:lo + n, :].T
            lo += n
        lane = lax.broadcasted_iota(jnp.int32, (D_MODEL, LANE), 1)
        wa_ref[...] = jnp.where(lane < GLA_RANK, wt_b_ref[0:LANE, :].T, jnp.zeros((), BF16))
        lo = GLA_RANK
        for w_ref, n in zip((wu_ref, wga_ref, wgb_ref), W_IN_SIZES[W_IN_SPLIT + 1:]):
            w_ref[...] = wt_b_ref[lo:lo + n, :].T
            lo += n

    @pl.when(mix_tile % tiles_per_seq == 0)
    def _():
        ubuf[0:POOL_HALO, :] = jnp.zeros((POOL_HALO, POOL_WIDTH), F32)

    @pl.when(jnp.logical_or(step == 0, gla_tile % tiles_per_seq == 0))
    def _():
        s_ref[...] = jnp.zeros_like(s_ref)

    mix_args = (x_ref, g_ref, wq_ref, wk_ref, wv_ref, wr_ref, wa_ref, wu_ref, wga_ref, wgb_ref, wal_ref, bal_ref,
                sr_ref, z_ref, sga_ref, sgb_ref, ubuf, mix_tile % tiles_per_seq)
    gla_args = (q_s, k_s, v_s, la_s, gn_ref[...], o_ref, s_ref)

    def run(gla_tasks):
        nxt = []
        _interleave(gla_tasks + [_mix_in_task(*mix_args, nxt)])
        q_s[...], k_s[...], v_s[...], la_s[...], span_s[0] = nxt

    wide_ok = span_s[0] < GLA_WIDE_MAX_SPAN
    last = step == n_tiles

    @pl.when(jnp.logical_and(wide_ok, jnp.logical_not(last)))
    def _():
        run([_gla_wide_task(*gla_args)])

    @pl.when(jnp.logical_and(wide_ok, last))
    def _():
        _interleave([_gla_wide_task(*gla_args)])

    @pl.when(jnp.logical_not(wide_ok))
    def _():
        _gla_anchored_chunks(*gla_args)
        pl.when(jnp.logical_not(last))(lambda: run([]))


def _mix_gla(x, g, wt_a, wt_b, wal, bal, gn, seq):
    t = x.shape[0]
    n_tiles = t // ROW_TILE

    def cur(n):
        return pl.BlockSpec((ROW_TILE, n), lambda i: (jnp.minimum(i, n_tiles - 1), 0))

    prev = pl.BlockSpec((ROW_TILE, GLA_VW), lambda i: (jnp.maximum(i - 1, 0), 0))

    def out(n):
        return jax.ShapeDtypeStruct((t, n), BF16)

    outs, _ = _call(
        functools.partial(_mix_gla_kernel, tiles_per_seq=seq // ROW_TILE, n_tiles=n_tiles),
        (x, g, wt_a, wt_b, wal, bal, gn),
        [cur(D_MODEL), _resident((1, D_MODEL)), _resident(wt_a.shape), _resident(wt_b.shape),
         _resident((LANE, GLA_QK)), _resident((1, GLA_QK)), _resident((1, GLA_DV))],
        [cur(GLA_VW), cur(POOL_WIDTH), cur(D_MODEL), cur(D_MODEL), prev],
        [out(GLA_VW), out(POOL_WIDTH), out(D_MODEL), out(D_MODEL), out(GLA_VW)],
        grid=(n_tiles + 1,),
        scratch_shapes=[pltpu.VMEM((ROW_TILE, GLA_QK), BF16), pltpu.VMEM((ROW_TILE, GLA_QK), BF16),
                        pltpu.VMEM((ROW_TILE, GLA_VW), BF16), pltpu.VMEM((ROW_TILE, GLA_QK), F32),
                        pltpu.VMEM((GLA_HEADS, GLA_DK, GLA_DV), F32),
                        pltpu.VMEM((POOL_HALO + ROW_TILE, POOL_WIDTH), F32),
                        pltpu.SMEM((1,), F32),
                        *(pltpu.VMEM((D_MODEL, LANE if n == GLA_RANK else n), BF16) for n in W_IN_SIZES)],
        name="mix_gla")
    return outs


def _mem_kv_kernel(m_ref, g_ref, wk_ref, wv_ref, k_ref, v_ref, wk_s, wv_s):
    @pl.when(pl.program_id(0) == 0)
    def _():
        wk_s[...] = wk_ref[...].astype(BF16)
        wv_s[...] = wv_ref[...].astype(BF16)

    m = _rms(m_ref[...], g_ref[...]).astype(BF16)
    k_ref[...] = _dot(m, wk_s[...]).astype(BF16)
    v_ref[...] = _dot(m, wv_s[...]).astype(BF16)


def _mem_kv(mem, g, wk, wv, mem_len, make_jobs):
    t = mem.shape[0]
    steps = t // mem_len
    row = pl.BlockSpec((mem_len, D_MODEL), lambda i: (i, 0))
    out = jax.ShapeDtypeStruct((t, D_MODEL), BF16)
    return _call(
        _mem_kv_kernel, (mem, g, wk, wv),
        [row, _resident((1, D_MODEL)), _resident((D_MODEL, D_MODEL)), _resident((D_MODEL, D_MODEL))],
        [row, row], [out, out], make_jobs(steps), grid=(steps,),
        scratch_shapes=[pltpu.VMEM((D_MODEL, D_MODEL), BF16)] * 2, name="mem_kv")


def _mix_xattn_kernel(x_ref, o_ref, sr_ref, z_ref, sga_ref, sgb_ref, wua_ref, pm_ref, ps_ref, wub_ref, wmo_ref,
                      xg_ref, wq_ref, k_ref, v_ref, wo_ref, y_ref):
    zs = []
    for g in range(POOL_GROUPS):
        cols = slice(g * POOL_GC, (g + 1) * POOL_GC)
        zs.append((_dot(z_ref[:, cols], pm_ref[g].astype(BF16)) * ps_ref[:, cols]).astype(BF16))
    ya = _dot(o_ref[...] * sr_ref[...], wua_ref[...])
    yb = _dot(jnp.concatenate(zs, axis=1), wub_ref[...])
    merged = (sga_ref[...].astype(F32) * ya + sgb_ref[...].astype(F32) * yb).astype(BF16)
    x = x_ref[...] + _dot(merged, wmo_ref[...])

    inv_rms = lax.rsqrt(jnp.mean(x * x, axis=-1, keepdims=True) + EPS)
    q = (_dot((x * xg_ref[...]).astype(BF16), wq_ref[...]) * (inv_rms * (XA_HD ** -0.5))).astype(BF16)

    def scores(hd):
        cols = slice(hd * XA_HD, (hd + 1) * XA_HD)
        return lax.dot_general(q[:, cols], k_ref[:, cols], (((1,), (1,)), ((), ())), preferred_element_type=F32)

    outs = []
    s = scores(0)
    for hd in range(XA_HEADS):
        s_next = scores(hd + 1) if hd + 1 < XA_HEADS else None
        e = jnp.exp(s - jnp.max(s, axis=-1, keepdims=True))
        o = _dot(e.astype(BF16), v_ref[:, hd * XA_HD:(hd + 1) * XA_HD])
        outs.append((o / jnp.sum(e, axis=-1, keepdims=True)).astype(BF16))
        s = s_next
    y_ref[...] = x + _dot(jnp.concatenate(outs, axis=1), wo_ref[...])


def _mix_xattn(x, o, sr, z, sga, sgb, wua, pm, ps, wub, wmo, xg, wq, km, vm, wo, seq, mem_len):
    t = x.shape[0]
    steps = t // XATTN_ROWS
    tiles_per_seq = seq // XATTN_ROWS

    def row(n):
        return pl.BlockSpec((XATTN_ROWS, n), lambda i: (i, 0))

    mem =pl.BlockSpec((mem_len, D_MODEL), lambda i: (i // tiles_per_seq, 0))
    (y,), _ = _call(
        _mix_xattn_kernel, (x, o, sr, z, sga, sgb, wua, pm, ps, wub, wmo, xg, wq, km, vm, wo),
        [row(D_MODEL), row(GLA_VW), row(GLA_VW), row(POOL_WIDTH), row(D_MODEL), row(D_MODEL),
         _resident((GLA_VW, D_MODEL)), _resident((POOL_GROUPS, POOL_GC, POOL_GC)),
         _resident((1, POOL_WIDTH)), _resident((POOL_WIDTH, D_MODEL)), _resident((D_MODEL, D_MODEL)),
         _resident((1, D_MODEL)), _resident((D_MODEL, D_MODEL)), mem, mem, _resident((D_MODEL, D_MODEL))],
        [row(D_MODEL)], [jax.ShapeDtypeStruct((t, D_MODEL), F32)],
        grid=(steps,), name="mix_xattn")
    return y


def _layer(x, mem, ffn1_norm, ffn1_w1, ffn1_w3, ffn1_w2, mix_norm, w_in, w_alpha, b_alpha, gla_head_norm,
           w_up_a, pool_mix, pool_scale, w_up_b, w_mix_out, xa_norm, mem_norm, xa_wq, xa_wk, xa_wv, xa_wo,
           ffn2_norm, ffn2_w1, ffn2_w3, ffn2_w2, final_norm, last):
    batch, seq, _ = x.shape
    mem_len = mem.shape[1]
    assert seq % ROW_TILE == 0 and ROW_TILE % GLA_CHUNK == 0 and ROW_TILE % GLA_WIDE == 0

    def vec(a):
        return a.reshape(1, -1).astype(F32)

    def casts(*ws):
        return lambda n_steps: tuple(_cast_rows_job(w, n_steps) for w in ws)

    (km, vm), ffn1_w = _mem_kv(mem.reshape(batch * mem_len, D_MODEL), vec(mem_norm), xa_wk, xa_wv, mem_len,
                               casts(ffn1_w1, ffn1_w3, ffn1_w2))
    later = (w_up_a, w_up_b, w_mix_out, xa_wq, xa_wo, ffn2_w1, ffn2_w3, ffn2_w2)
    x1, cast = _ffn(x.reshape(batch * seq, D_MODEL), vec(ffn1_norm), *ffn1_w, vec(final_norm), False,
                    lambda n_steps: (_w_in_job(jnp.swapaxes(w_in, 0, 1), n_steps), *casts(*later)(n_steps)))
    wt_a, wt_b, wua, wub, wmo, xwq, xwo, *ffn2_w = cast
    wal = jnp.pad(w_alpha.astype(BF16), ((0, LANE - GLA_RANK), (0, 0)))
    sr, z, sga, sgb, o = _mix_gla(x1, vec(mix_norm), wt_a, wt_b, wal, vec(b_alpha), vec(gla_head_norm), seq)
    x3 = _mix_xattn(x1, o, sr, z, sga, sgb, wua, pool_mix, vec(pool_scale), wub, wmo, vec(xa_norm), xwq, km, vm, xwo,
                    seq, mem_len)
    x4, _ = _ffn(x3, vec(ffn2_norm), *ffn2_w, vec(final_norm), last)
    return x4.reshape(batch, seq, D_MODEL)


def kernel(x, mem, ffn1_norm, ffn1_w1, ffn1_w3, ffn1_w2, mix_norm, w_in, w_alpha, b_alpha, gla_head_norm, w_up_a,
           pool_mix, pool_scale, w_up_b, w_mix_out, xa_norm, mem_norm, xa_wq, xa_wk, xa_wv, xa_wo, ffn2_norm,
           ffn2_w1, ffn2_w3, ffn2_w2, final_norm):
    depth = ffn1_norm.shape[0]
    for l in range(depth):
        last = l == depth - 1
        x = _layer(x, mem, ffn1_norm[l], ffn1_w1[l], ffn1_w3[l], ffn1_w2[l], mix_norm[l], w_in[l], w_alpha[l],
                   b_alpha[l], gla_head_norm[l], w_up_a[l], pool_mix[l], pool_scale[l], w_up_b[l], w_mix_out[l],
                   xa_norm[l], mem_norm[l], xa_wq[l], xa_wk[l], xa_wv[l], xa_wo[l], ffn2_norm[l], ffn2_w1[l],
                   ffn2_w3[l], ffn2_w2[l], final_norm, last)
    return x
```

```python
import functools
from typing import Callable, NamedTuple

import jax
import jax.numpy as jnp
from jax import lax
from jax.experimental import pallas as pl
from jax.experimental.pallas import tpu as pltpu

F32 = jnp.float32
BF16 = jnp.bfloat16

D_MODEL = 1024
D_FF = 2816
EPS = 1e-6
GLA_HEADS = 4
GLA_DK = 128
GLA_DV = 256
GLA_QK = GLA_HEADS * GLA_DK
GLA_VW = GLA_HEADS * GLA_DV
GLA_RANK = 16
GLA_GATE_TEMP = 16.0
GLA_CHUNK = 64
GLA_SUB = 16
GLA_WIDE = 256
GLA_WIDE_MAX_SPAN = 60.0
EXP_TO_ZERO = -1e30
POOL_GROUPS = 4
POOL_GC = 128
POOL_WIDTH = POOL_GROUPS * POOL_GC
POOL_WINDOWS = (2, 4, 8, 16)
POOL_HALO = 16
XA_HEADS = 4
XA_HD = 256

LANE = 128
BF16_ROWS = 16
W_IN_SIZES = (GLA_QK, GLA_QK, GLA_VW, GLA_VW, GLA_RANK, POOL_WIDTH, D_MODEL, D_MODEL)
W_IN_SPLIT = 4
ROW_TILE = 512
MEM_BATCHES_PER_STEP = 2
XATTN_ROWS = 1024
FFN_ROWS = 1024
FF_SPLITS = (0, 768, 1536, 2304, D_FF)
VMEM_LIMIT = 56 * 1024 * 1024


def _rms(x, g):
    return x * lax.rsqrt(jnp.mean(x * x, axis=-1, keepdims=True) + EPS) * g


def _sigmoid(x):
    return 0.5 * jnp.tanh(0.5 * x) + 0.5


def _dot(a, b):
    return jnp.dot(a, b, preferred_element_type=F32)


def _resident(shape):
    nd = len(shape)
    return pl.BlockSpec(shape, lambda *_: (0,) * nd, pipeline_mode=pl.Buffered(1))


def _params():
    return pltpu.CompilerParams(dimension_semantics=("arbitrary",), vmem_limit_bytes=VMEM_LIMIT)


class _CastJob(NamedTuple):
    inputs: tuple
    in_specs: tuple
    out_specs: tuple
    out_shapes: tuple
    body: Callable


def _cast_rows_job(w, n_steps):
    rows, cols = w.shape
    per_step = -(-rows // n_steps)
    block = next(b for b in range(BF16_ROWS, rows + 1, BF16_ROWS) if rows % b == 0 and b >= per_step)
    n_blocks = rows // block
    spec = pl.BlockSpec((block, cols), lambda i: ((i * n_blocks) // n_steps, 0))

    def body(ins, outs):
        outs[0][...] = ins[0][...].astype(BF16)

    return _CastJob((w,), (spec,), (spec,), (jax.ShapeDtypeStruct((rows, cols), BF16),), body)


def _w_in_job(w_in_t, n_steps):
    cols = w_in_t.shape[1]
    rows_a = sum(W_IN_SIZES[:W_IN_SPLIT])
    rows_b = sum(W_IN_SIZES[W_IN_SPLIT:])
    assert w_in_t.shape[0] == rows_a + rows_b and rows_a % n_steps == 0
    blk_a = rows_a // n_steps
    per_b = -(-rows_b // n_steps)
    blk_b = next(r for r in range(BF16_ROWS, rows_b + 1, BF16_ROWS) if rows_b % r == 0 and r >= per_b)
    last_b = rows_b // blk_b - 1
    assert blk_a % BF16_ROWS == 0
    in_a = pl.BlockSpec((blk_a, cols), lambda i: (i, 0))
    in_b = pl.BlockSpec((pl.Element(blk_b), pl.Element(cols)),
                        lambda i: (pl.multiple_of(rows_a + blk_b * jnp.minimum(i, last_b), BF16_ROWS), 0))
    out_b = pl.BlockSpec((blk_b, cols), lambda i: (jnp.minimum(i, last_b), 0))

    def body(ins, outs):
        for i_ref, o_ref in zip(ins, outs):
            o_ref[...] = i_ref[...].astype(BF16)

    return _CastJob((w_in_t, w_in_t), (in_a, in_b), (in_a, out_b),
                    (jax.ShapeDtypeStruct((rows_a, cols), BF16), jax.ShapeDtypeStruct((rows_b, cols), BF16)), body)


def _call(kernel_fn, args, in_specs, out_specs, out_shape, jobs=(), *, grid, scratch_shapes=(), name):
    n_in, n_out = len(args), len(out_specs)
    job_args = [a for j in jobs for a in j.inputs]
    job_outs = [o for j in jobs for o in j.out_shapes]

    def body(*refs):
        j_in = refs[n_in:n_in + len(job_args)]
        first_out = n_in + len(job_args)
        j_out = refs[first_out + n_out:first_out + n_out + len(job_outs)]
        kernel_fn(*refs[:n_in], *refs[first_out:first_out + n_out], *refs[first_out + n_out + len(job_outs):])
        for j in jobs:
            j.body(j_in[:len(j.inputs)], j_out[:len(j.out_shapes)])
            j_in, j_out = j_in[len(j.inputs):], j_out[len(j.out_shapes):]

    outs = pl.pallas_call(
        body,
        grid=grid,
        in_specs=[*in_specs, *(sp for j in jobs for sp in j.in_specs)],
        out_specs=[*out_specs, *(sp for j in jobs for sp in j.out_specs)],
        out_shape=[*out_shape, *job_outs],
        scratch_shapes=list(scratch_shapes),
        compiler_params=_params(),
        name=name,
    )(*args, *job_args)
    return outs[:n_out], outs[n_out:]


def _ffn_kernel(x_ref, g_ref, w1_ref, w3_ref, w2_ref, fg_ref, o_ref, *, final_norm):
    x = x_ref[...]
    inv_rms = lax.rsqrt(jnp.mean(x * x, axis=-1, keepdims=True) + EPS)
    h = (x * g_ref[...]).astype(BF16)
    acc = jnp.zeros_like(x)
    for lo, hi in zip(FF_SPLITS[:-1], FF_SPLITS[1:]):
        a = _dot(h, w1_ref[:, lo:hi]) * inv_rms
        b = _dot(h, w3_ref[:, lo:hi]) * inv_rms
        act = (a * _sigmoid(a) * b).astype(BF16)
        acc = acc + _dot(act, w2_ref[lo:hi, :])
    y = x + 0.5 * acc
    if final_norm:
        y = _rms(y, fg_ref[...])
    o_ref[...] = y


def _ffn(x, g, w1, w3, w2, fg, final_norm, make_jobs=None):
    t = x.shape[0]
    steps = t // FFN_ROWS
    row = pl.BlockSpec((FFN_ROWS, D_MODEL), lambda i: (i, 0))
    (y,), cast = _call(
        functools.partial(_ffn_kernel, final_norm=final_norm),
        (x, g, w1, w3, w2, fg),
        [row, _resident((1, D_MODEL)), _resident((D_MODEL, D_FF)), _resident((D_MODEL, D_FF)),
         _resident((D_FF, D_MODEL)), _resident((1, D_MODEL))],
        [row], [jax.ShapeDtypeStruct((t, D_MODEL), F32)],
        make_jobs(steps) if make_jobs else (),
        grid=(steps,), name="ffn_final" if final_norm else "ffn")
    return y, cast


def _cumsum_rows(mask_bf16, la):
    hi = la.astype(BF16)
    lo = (la - hi.astype(F32)).astype(BF16)
    return _dot(mask_bf16, hi) + _dot(mask_bf16, lo)


def _decay_column(b_last_row):
    return jnp.transpose(jnp.broadcast_to(jnp.exp(b_last_row), (8, GLA_DK)))[:, 0:1]


def _gla_wide_task(q_ref, k_ref, v_ref, la_ref, gn, o_ref, s_ref):
    c = GLA_WIDE
    ri = lax.broadcasted_iota(jnp.int32, (c, c), 0)
    ci = lax.broadcasted_iota(jnp.int32, (c, c), 1)
    causal = ci <= ri
    cum_incl = causal.astype(BF16)
    row_sl = [slice(ic * c, (ic + 1) * c) for ic in range(ROW_TILE // c)]
    b_all = [_dot(cum_incl, la_ref[rows, :].astype(BF16)) for rows in row_sl]
    units = [(ic, hd) for ic in range(len(row_sl)) for hd in range(GLA_HEADS)]

    def factors(ic, hd):
        rows, kc = row_sl[ic], slice(hd * GLA_DK, (hd + 1) * GLA_DK)
        b = b_all[ic][:, kc]
        qg = (q_ref[rows, kc].astype(F32) * jnp.exp(b)).astype(BF16)
        kg = (k_ref[rows, kc].astype(F32) * jnp.exp(-b)).astype(BF16)
        s = lax.dot_general(qg, kg, (((1,), (1,)), ((), ())), preferred_element_type=F32)
        return qg, kg, s, b[c - 1:c, :]

    nxt = factors(*units[0])
    yield
    for n, (ic, hd) in enumerate(units):
        qg, kg, s, b_last = nxt
        if n + 1 < len(units):
            nxt = factors(*units[n + 1])
        rows, vc = row_sl[ic], slice(hd * GLA_DV, (hd + 1) * GLA_DV)
        v = v_ref[rows, vc]
        s_old = s_ref[hd]
        inter = _dot(qg, s_old.astype(BF16))
        upd = lax.dot_general(kg, v, (((0,), (0,)), ((), ())), preferred_element_type=F32)
        p = jnp.where(causal, s, 0.0).astype(BF16)
        o = _dot(p, v) + inter
        s_ref[hd] = (s_old + upd) * _decay_column(b_last)
        o_ref[rows, vc] = _rms(o, gn).astype(BF16)
        yield


def _gla_anchored_chunks(q_ref, k_ref, v_ref, la_ref, gn, o_ref, s_ref):
    c = GLA_CHUNK
    ri = lax.broadcasted_iota(jnp.int32, (c, c), 0)
    ci = lax.broadcasted_iota(jnp.int32, (c, c), 1)
    causal = ci <= ri
    cum_incl = causal.astype(BF16)
    cum_anchor = (ci < (ri // GLA_SUB) * GLA_SUB).astype(BF16)
    krow = lax.broadcasted_iota(jnp.int32, (c, 1), 0)
    kcol = lax.broadcasted_iota(jnp.int32, (GLA_SUB, c), 1)

    def chunk(ic, carry):
        r0 = pl.multiple_of(ic * c, c)
        la = la_ref[pl.ds(r0, c), :]
        b_all = _cumsum_rows(cum_incl, la)
        beta_all = _cumsum_rows(cum_anchor, la)
        for hd in range(GLA_HEADS):
            kc = slice(hd * GLA_DK, (hd + 1) * GLA_DK)
            vc = slice(hd * GLA_DV, (hd + 1) * GLA_DV)
            q = q_ref[pl.ds(r0, c), kc].astype(F32)
            k = k_ref[pl.ds(r0, c), kc].astype(F32)
            v = v_ref[pl.ds(r0, c), vc]
            b = b_all[:, kc]
            beta = beta_all[:, kc]
            qa = (q * jnp.exp(b - beta)).astype(BF16)
            rows = []
            for blk in range(c // GLA_SUB):
                r_lo, r_hi = blk * GLA_SUB, (blk + 1) * GLA_SUB
                e = jnp.where(krow < r_lo, beta[r_lo:r_lo + 1, :] - b, EXP_TO_ZERO)
                ka = (k * jnp.exp(e)).astype(BF16)
                sc = lax.dot_general(qa[r_lo:r_hi, :], ka, (((1,), (1,)), ((), ())), preferred_element_type=F32)
                qb, bb = q[r_lo:r_hi, :], b[r_lo:r_hi, :]
                for j in range(r_lo, r_hi):
                    w = jnp.exp(jnp.minimum(bb - b[j:j + 1, :], 0.0))
                    sc = jnp.where(kcol == j, jnp.sum(qb * k[j:j + 1, :] * w, axis=1, keepdims=True), sc)
                rows.append(sc)
            p = jnp.where(causal, jnp.concatenate(rows, axis=0), 0.0).astype(BF16)
            s_old = s_ref[hd]
            o = _dot(p, v) + _dot((q * jnp.exp(b)).astype(BF16), s_old.astype(BF16))
            b_last = b[c - 1:c, :]
            k_dec = (k * jnp.exp(b_last - b)).astype(BF16)
            upd = lax.dot_general(k_dec, v, (((0,), (0,)), ((), ())), preferred_element_type=F32)
            s_ref[hd] = s_old * _decay_column(b_last) + upd
            o_ref[pl.ds(r0, c), vc] = _rms(o, gn).astype(BF16)
        return carry

    lax.fori_loop(0, ROW_TILE // c, chunk, 0)


def _interleave(tasks):
    tasks = list(tasks)
    while tasks:
        for t in list(tasks):
            try:
                next(t)
            except StopIteration:
                tasks.remove(t)


def _mix_in_task(x_ref, g_ref, wq_ref, wk_ref, wv_ref, wr_ref, wa_ref, wu_ref, wga_ref, wgb_ref, wal_ref, bal_ref,
                 sr_ref, z_ref, sga_ref, sgb_ref, ubuf, tile_in_seq, gla_inputs):
    h = _rms(x_ref[...], g_ref[...]).astype(BF16)
    a_code = _dot(h, wa_ref[...]).astype(BF16)
    yield
    u = _dot(h, wu_ref[...])
    yield
    r = _dot(h, wr_ref[...])
    sr_ref[...] = (r * _sigmoid(r)).astype(BF16)
    yield
    zg = _dot(a_code, wal_ref[...]) + bal_ref[...]
    yield
    sga_ref[...] = _sigmoid(_dot(h, wga_ref[...])).astype(BF16)
    yield
    sgb_ref[...] = _sigmoid(_dot(h, wgb_ref[...])).astype(BF16)
    yield

    ubuf[POOL_HALO:POOL_HALO + ROW_TILE, :] = u
    pos = lax.broadcasted_iota(jnp.int32, (ROW_TILE, 1), 0) + tile_in_seq * ROW_TILE
    for g, w in enumerate(POOL_WINDOWS):
        cols = slice(g * POOL_GC, (g + 1) * POOL_GC)
        s = ubuf[:, cols]
        sh = 1
        while sh < w:
            s = s + pltpu.roll(s, sh, 0)
            sh *= 2
        cnt = jnp.minimum(pos + 1, w).astype(F32)
        z_ref[:, cols] = (s[POOL_HALO:, :] / cnt - u[:, cols]).astype(BF16)
    ubuf[0:POOL_HALO, :] = ubuf[ROW_TILE:ROW_TILE + POOL_HALO, :]

    q = (_dot(h, wq_ref[...]) * (GLA_DK ** -0.5)).astype(BF16)
    yield
    v = _dot(h, wv_ref[...]).astype(BF16)
    yield
    k = _dot(h, wk_ref[...]).astype(BF16)
    la = (jnp.minimum(zg, 0.0) - jnp.log(1.0 + jnp.exp(-jnp.abs(zg)))) * (1.0 / GLA_GATE_TEMP)
    span = None
    for r0 in range(0, ROW_TILE, GLA_WIDE):
        sp = jnp.max(-jnp.sum(la[r0:r0 + GLA_WIDE, :], axis=0, keepdims=True))
        span = sp if span is None else jnp.maximum(span, sp)
    gla_inputs.extend((q, k, v, la, span))


def _mix_gla_kernel(x_ref, g_ref, wt_a_ref, wt_b_ref, wal_ref, bal_ref, gn_ref,
                    sr_ref, z_ref, sga_ref, sgb_ref, o_ref,
                    q_s, k_s, v_s, la_s, s_ref, ubuf, span_s,
                    wq_ref, wk_ref, wv_ref, wr_ref, wa_ref, wu_ref, wga_ref, wgb_ref, *, tiles_per_seq, n_tiles):
    step = pl.program_id(0)
    mix_tile = jnp.minimum(step, n_tiles - 1)
    gla_tile = step - 1

    @pl.when(step == 0)
    def _():
        q_s[...] = jnp.zeros_like(q_s)
        k_s[...] = jnp.zeros_like(k_s)
        v_s[...] = jnp.zeros_like(v_s)
        la_s[...] = jnp.zeros_like(la_s)
        span_s[0] = 0.0
        lo = 0
        for w_ref, n in zip((wq_ref, wk_ref, wv_ref, wr_ref), W_IN_SIZES[:W_IN_SPLIT]):
            w_ref[...] = wt_a_ref[lo:lo + n, :].T
            lo += n
        lane = lax.broadcasted_iota(jnp.int32, (D_MODEL, LANE), 1)
        wa_ref[...] = jnp.where(lane < GLA_RANK, wt_b_ref[0:LANE, :].T, jnp.zeros((), BF16))
        lo = GLA_RANK
        for w_ref, n in zip((wu_ref, wga_ref, wgb_ref), W_IN_SIZES[W_IN_SPLIT + 1:]):
            w_ref[...] = wt_b_ref[lo:lo + n, :].T
            lo += n

    @pl.when(mix_tile % tiles_per_seq == 0)
    def _():
        ubuf[0:POOL_HALO, :] = jnp.zeros((POOL_HALO, POOL_WIDTH), F32)

    @pl.when(jnp.logical_or(step == 0, gla_tile % tiles_per_seq == 0))
    def _():
        s_ref[...] = jnp.zeros_like(s_ref)

    mix_args = (x_ref, g_ref, wq_ref, wk_ref, wv_ref, wr_ref, wa_ref, wu_ref, wga_ref, wgb_ref, wal_ref, bal_ref,
                sr_ref, z_ref, sga_ref, sgb_ref, ubuf, mix_tile % tiles_per_seq)
    gla_args = (q_s, k_s, v_s, la_s, gn_ref[...], o_ref, s_ref)

    def run(gla_tasks):
        nxt = []
        _interleave(gla_tasks + [_mix_in_task(*mix_args, nxt)])
        q_s[...], k_s[...], v_s[...], la_s[...], span_s[0] = nxt

    wide_ok = span_s[0] < GLA_WIDE_MAX_SPAN
    last = step == n_tiles

    @pl.when(jnp.logical_and(wide_ok, jnp.logical_not(last)))
    def _():
        run([_gla_wide_task(*gla_args)])

    @pl.when(jnp.logical_and(wide_ok, last))
    def _():
        _interleave([_gla_wide_task(*gla_args)])

    @pl.when(jnp.logical_not(wide_ok))
    def _():
        _gla_anchored_chunks(*gla_args)
        pl.when(jnp.logical_not(last))(lambda: run([]))


def _mix_gla(x, g, wt_a, wt_b, wal, bal, gn, seq):
    t = x.shape[0]
    n_tiles = t // ROW_TILE

    def cur(n):
        return pl.BlockSpec((ROW_TILE, n), lambda i: (jnp.minimum(i, n_tiles - 1), 0))

    prev = pl.BlockSpec((ROW_TILE, GLA_VW), lambda i: (jnp.maximum(i - 1, 0), 0))

    def out(n):
        return jax.ShapeDtypeStruct((t, n), BF16)

    outs, _ = _call(
        functools.partial(_mix_gla_kernel, tiles_per_seq=seq // ROW_TILE, n_tiles=n_tiles),
        (x, g, wt_a, wt_b, wal, bal, gn),
        [cur(D_MODEL), _resident((1, D_MODEL)), _resident(wt_a.shape), _resident(wt_b.shape),
         _resident((LANE, GLA_QK)), _resident((1, GLA_QK)), _resident((1, GLA_DV))],
        [cur(GLA_VW), cur(POOL_WIDTH), cur(D_MODEL), cur(D_MODEL), prev],
        [out(GLA_VW), out(POOL_WIDTH), out(D_MODEL), out(D_MODEL), out(GLA_VW)],
        grid=(n_tiles + 1,),
        scratch_shapes=[pltpu.VMEM((ROW_TILE, GLA_QK), BF16), pltpu.VMEM((ROW_TILE, GLA_QK), BF16),
                        pltpu.VMEM((ROW_TILE, GLA_VW), BF16), pltpu.VMEM((ROW_TILE, GLA_QK), F32),
                        pltpu.VMEM((GLA_HEADS, GLA_DK, GLA_DV), F32),
                        pltpu.VMEM((POOL_HALO + ROW_TILE, POOL_WIDTH), F32),
                        pltpu.SMEM((1,), F32),
                        *(pltpu.VMEM((D_MODEL, LANE if n == GLA_RANK else n), BF16) for n in W_IN_SIZES)],
        name="mix_gla")
    return outs


def _mem_kv_kernel(m_ref, g_ref, wk_ref, wv_ref, k_ref, v_ref, wk_s, wv_s):
    @pl.when(pl.program_id(0) == 0)
    def _():
        wk_s[...] = wk_ref[...].astype(BF16)
        wv_s[...] = wv_ref[...].astype(BF16)

    m = _rms(m_ref[...], g_ref[...]).astype(BF16)
    k_ref[...] = _dot(m, wk_s[...]).astype(BF16)
    v_ref[...] = _dot(m, wv_s[...]).astype(BF16)


def _mem_kv(mem, g, wk, wv, mem_len, make_jobs):
    t = mem.shape[0]
    rows = MEM_BATCHES_PER_STEP * mem_len
    assert t % rows == 0
    steps = t // rows
    row = pl.BlockSpec((rows, D_MODEL), lambda i: (i, 0))
    out = jax.ShapeDtypeStruct((t, D_MODEL), BF16)
    return _call(
        _mem_kv_kernel, (mem, g, wk, wv),
        [row, _resident((1, D_MODEL)), _resident((D_MODEL, D_MODEL)), _resident((D_MODEL, D_MODEL))],
        [row, row], [out, out], make_jobs(steps), grid=(steps,),
        scratch_shapes=[pltpu.VMEM((D_MODEL, D_MODEL), BF16)] * 2, name="mem_kv")


def _mix_xattn_kernel(x_ref, o_ref, sr_ref, z_ref, sga_ref, sgb_ref, wua_ref, pm_ref, ps_ref, wub_ref, wmo_ref,
                      xg_ref, wq_ref, k_ref, v_ref, wo_ref, y_ref):
    zs = []
    for g in range(POOL_GROUPS):
        cols = slice(g * POOL_GC, (g + 1) * POOL_GC)
        zs.append((_dot(z_ref[:, cols], pm_ref[g].astype(BF16)) * ps_ref[:, cols]).astype(BF16))
    ya = _dot(o_ref[...] * sr_ref[...], wua_ref[...])
    yb = _dot(jnp.concatenate(zs, axis=1), wub_ref[...])
    merged = (sga_ref[...].astype(F32) * ya + sgb_ref[...].astype(F32) * yb).astype(BF16)
    x = x_ref[...] + _dot(merged, wmo_ref[...])

    inv_rms = lax.rsqrt(jnp.mean(x * x, axis=-1, keepdims=True) + EPS)
    q = (_dot((x * xg_ref[...]).astype(BF16), wq_ref[...]) * (inv_rms * (XA_HD ** -0.5))).astype(BF16)

    def scores(hd):
        cols = slice(hd * XA_HD, (hd + 1) * XA_HD)
        return lax.dot_general(q[:, cols], k_ref[:, cols], (((1,), (1,)), ((), ())), preferred_element_type=F32)

    outs = []
    s = scores(0)
    for hd in range(XA_HEADS):
        s_next = scores(hd + 1) if hd + 1 < XA_HEADS else None
        e = jnp.exp(s - jnp.max(s, axis=-1, keepdims=True))
        o = _dot(e.astype(BF16), v_ref[:, hd * XA_HD:(hd + 1) * XA_HD])
        outs.append((o / jnp.sum(e, axis=-1, keepdims=True)).astype(BF16))
        s = s_next
    y_ref[...] = x + _dot(jnp.concatenate(outs, axis=1), wo_ref[...])


def _mix_xattn(x, o, sr, z, sga, sgb, wua, pm, ps, wub, wmo, xg, wq, km, vm, wo, seq, mem_len):
    t = x.shape[0]
    steps = t // XATTN_ROWS
    tiles_per_seq = seq // XATTN_ROWS

    def row(n):
        return pl.BlockSpec((XATTN_ROWS, n), lambda i: (i, 0))

    mem =pl.BlockSpec((mem_len, D_MODEL), lambda i: (i // tiles_per_seq, 0))
    (y,), _ = _call(
        _mix_xattn_kernel, (x, o, sr, z, sga, sgb, wua, pm, ps, wub, wmo, xg, wq, km, vm, wo),
        [row(D_MODEL), row(GLA_VW), row(GLA_VW), row(POOL_WIDTH), row(D_MODEL), row(D_MODEL),
         _resident((GLA_VW, D_MODEL)), _resident((POOL_GROUPS, POOL_GC, POOL_GC)),
         _resident((1, POOL_WIDTH)), _resident((POOL_WIDTH, D_MODEL)), _resident((D_MODEL, D_MODEL)),
         _resident((1, D_MODEL)), _resident((D_MODEL, D_MODEL)), mem, mem, _resident((D_MODEL, D_MODEL))],
        [row(D_MODEL)], [jax.ShapeDtypeStruct((t, D_MODEL), F32)],
        grid=(steps,), name="mix_xattn")
    return y


def _layer(x, mem, ffn1_norm, ffn1_w1, ffn1_w3, ffn1_w2, mix_norm, w_in, w_alpha, b_alpha, gla_head_norm,
           w_up_a, pool_mix, pool_scale, w_up_b, w_mix_out, xa_norm, mem_norm, xa_wq, xa_wk, xa_wv, xa_wo,
           ffn2_norm, ffn2_w1, ffn2_w3, ffn2_w2, final_norm, last):
    batch, seq, _ = x.shape
    mem_len = mem.shape[1]
    assert seq % ROW_TILE == 0 and ROW_TILE % GLA_CHUNK == 0 and ROW_TILE % GLA_WIDE == 0

    def vec(a):
        return a.reshape(1, -1).astype(F32)

    def casts(*ws):
        return lambda n_steps: tuple(_cast_rows_job(w, n_steps) for w in ws)

    (km, vm), ffn1_w = _mem_kv(mem.reshape(batch * mem_len, D_MODEL), vec(mem_norm), xa_wk, xa_wv, mem_len,
                               casts(ffn1_w1, ffn1_w3, ffn1_w2))
    later = (w_up_a, w_up_b, w_mix_out, xa_wq, xa_wo, ffn2_w1, ffn2_w3, ffn2_w2)
    x1, cast = _ffn(x.reshape(batch * seq, D_MODEL), vec(ffn1_norm), *ffn1_w, vec(final_norm), False,
                    lambda n_steps: (_w_in_job(jnp.swapaxes(w_in, 0, 1), n_steps), *casts(*later)(n_steps)))
    wt_a, wt_b, wua, wub, wmo, xwq, xwo, *ffn2_w = cast
    wal = jnp.pad(w_alpha.astype(BF16), ((0, LANE - GLA_RANK), (0, 0)))
    sr, z, sga, sgb, o = _mix_gla(x1, vec(mix_norm), wt_a, wt_b, wal, vec(b_alpha), vec(gla_head_norm), seq)
    x3 = _mix_xattn(x1, o, sr, z, sga, sgb, wua, pool_mix, vec(pool_scale), wub, wmo, vec(xa_norm), xwq, km, vm, xwo,
                    seq, mem_len)
    x4, _ = _ffn(x3, vec(ffn2_norm), *ffn2_w, vec(final_norm), last)
    return x4.reshape(batch, seq, D_MODEL)


def kernel(x, mem, ffn1_norm, ffn1_w1, ffn1_w3, ffn1_w2, mix_norm, w_in, w_alpha, b_alpha, gla_head_norm, w_up_a,
           pool_mix, pool_scale, w_up_b, w_mix_out, xa_norm, mem_norm, xa_wq, xa_wk, xa_wv, xa_wo, ffn2_norm,
           ffn2_w1, ffn2_w3, ffn2_w2, final_norm):
    depth = ffn1_norm.shape[0]
    for l in range(depth):
        last = l == depth - 1
        x = _layer(x, mem, ffn1_norm[l], ffn1_w1[l], ffn1_w3[l], ffn1_w2[l], mix_norm[l], w_in[l], w_alpha[l],
                   b_alpha[l], gla_head_norm[l], w_up_a[l], pool_mix[l], pool_scale[l], w_up_b[l], w_mix_out[l],
                   xa_norm[l], mem_norm[l], xa_wq[l], xa_wk[l], xa_wv[l], xa_wo[l], ffn2_norm[l], ffn2_w1[l],
                   ffn2_w3[l], ffn2_w2[l], final_norm, last)
    return x
```
